```python
import math
import jax, jax.numpy as jnp
from jax import lax
import numpy as np

D_MODEL = 1024
BATCH = 8
SEQ = 8192
DEPTH = 2

HEAD_DIM = 64
POOL_WINDOWS = (2, 4, 8, 16)
POOL_GROUPS = 4
POOL_GROUP_DIM = 64
POOL_WIDTH = POOL_GROUPS * POOL_GROUP_DIM
DIFF_HEADS = 4
DIFF_V_DIM = 2 * HEAD_DIM
DIFF_QK_WIDTH = DIFF_HEADS * 2 * HEAD_DIM
DIFF_WIDTH = DIFF_HEADS * DIFF_V_DIM
SB_HEADS = 4
SB_WIDTH = SB_HEADS * HEAD_DIM
FOX_HEADS = 4
FOX_WIDTH = FOX_HEADS * HEAD_DIM
N_BRANCHES = 4
BRANCH_WIDTHS = (POOL_WIDTH, DIFF_WIDTH, SB_WIDTH, FOX_WIDTH)
MIX_WIDTH = POOL_WIDTH + DIFF_WIDTH + SB_WIDTH + FOX_WIDTH
IN_SPLITS = (POOL_WIDTH,
             DIFF_QK_WIDTH, DIFF_QK_WIDTH, DIFF_WIDTH,
             SB_WIDTH, SB_WIDTH, SB_WIDTH,
             FOX_WIDTH, FOX_WIDTH, FOX_WIDTH, FOX_HEADS,
             N_BRANCHES * D_MODEL)
D_IN = sum(IN_SPLITS)
D_FF = -(-(8 * D_MODEL) // (3 * 256)) * 256
Q_BLOCK = 128
ROPE_THETA = 10000.0
NORM_EPS = 1e-6

kernel_name = 'hybrid_gated_parallel_mixers'


def _rms_norm(x, gain):
    xf = x.astype(jnp.float32)
    y = xf * lax.rsqrt(jnp.mean(xf * xf, axis=-1, keepdims=True) + NORM_EPS)
    return (y * gain.astype(jnp.float32)).astype(x.dtype)


def _rope(x, positions):
    half = x.shape[-1] // 2
    inv_freq = ROPE_THETA ** (-jnp.arange(half, dtype=jnp.float32) / half)
    ang = positions.astype(jnp.float32)[:, None] * inv_freq[None, :]
    cos, sin = jnp.cos(ang), jnp.sin(ang)
    xf = x.astype(jnp.float32)
    x1, x2 = xf[..., :half], xf[..., half:]
    return jnp.concatenate([x1 * cos - x2 * sin, x2 * cos + x1 * sin], axis=-1).astype(x.dtype)


def _heads(t, n_heads):
    b, s, _ = t.shape
    return t.reshape(b, s, n_heads, -1).transpose(0, 2, 1, 3)


def _block_positions(i, seq_len):
    q_pos = i * Q_BLOCK + jnp.arange(Q_BLOCK)
    k_pos = jnp.arange(seq_len)
    return q_pos[:, None], k_pos[None, :]


def _sweep_query_blocks(block_fn, seq_len):
    out = lax.map(block_fn, jnp.arange(seq_len // Q_BLOCK))
    n_blocks, b, h, qb, dv = out.shape
    return out.transpose(1, 0, 3, 2, 4).reshape(b, n_blocks * qb, h * dv)


def _diff_lambda_init(layer):
    return 0.8 - 0.6 * math.exp(-0.3 * layer)


def _pool_mixer(u, pool_w, pool_scale):
    b, s, _ = u.shape
    uf = u.reshape(b, s, POOL_GROUPS, POOL_GROUP_DIM).astype(jnp.float32)
    csum = jnp.cumsum(uf, axis=1)
    t = jnp.arange(s)
    outs = []
    for g, w in enumerate(POOL_WINDOWS):
        c = csum[:, :, g]
        lagged = jnp.pad(c, ((0, 0), (w, 0), (0, 0)))[:, :s]
        count = jnp.minimum(t + 1, w).astype(jnp.float32)[None, :, None]
        outs.append((c - lagged) / count - uf[:, :, g])
    pooled = jnp.stack(outs, axis=2).astype(u.dtype)
    mixed = jnp.einsum('bsgc,gcd->bsgd', pooled, pool_w)
    return mixed.reshape(b, s, POOL_WIDTH) * pool_scale


def _diff_attention(q, k, v, q_gain, k_gain, subln_gain, lam_q1, lam_k1, lam_q2, lam_k2,
                    lambda_init, positions):
    b, s, _ = q.shape
    q = q.reshape(b, s, DIFF_HEADS, 2, HEAD_DIM).transpose(0, 2, 3, 1, 4)
    k = k.reshape(b, s, DIFF_HEADS, 2, HEAD_DIM).transpose(0, 2, 3, 1, 4)
    q = _rope(_rms_norm(q, q_gain), positions)
    k = _rope(_rms_norm(k, k_gain), positions)
    v = _heads(v, DIFF_HEADS)
    lam = (jnp.exp(jnp.sum(lam_q1.astype(jnp.float32) * lam_k1.astype(jnp.float32)))
           - jnp.exp(jnp.sum(lam_q2.astype(jnp.float32) * lam_k2.astype(jnp.float32)))
           + lambda_init)
    scale = HEAD_DIM ** -0.5

    def block(i):
        qb = lax.dynamic_slice_in_dim(q, i * Q_BLOCK, Q_BLOCK, axis=3)
        logits = jnp.einsum('bhmqd,bhmkd->bhmqk', qb, k).astype(jnp.float32) * scale
        qp, kp = _block_positions(i, s)
        probs = jax.nn.softmax(jnp.where(kp <= qp, logits, -jnp.inf), axis=-1)
        weights = probs[:, :, 0] - lam * probs[:, :, 1]
        return jnp.einsum('bhqk,bhkd->bhqd', weights.astype(v.dtype), v)

    o = _sweep_query_blocks(block, s).reshape(b, s, DIFF_HEADS, DIFF_V_DIM)
    o = _rms_norm(o, subln_gain) * (1.0 - lambda_init)
    return o.reshape(b, s, DIFF_WIDTH)


def _stick_breaking_attention(q, k, v):
    b, s, _ = q.shape
    q, k, v = _heads(q, SB_HEADS), _heads(k, SB_HEADS), _heads(v, SB_HEADS)
    scale = HEAD_DIM ** -0.5

    def block(i):
        qb = lax.dynamic_slice_in_dim(q, i * Q_BLOCK, Q_BLOCK, axis=2)
        z = jnp.einsum('bhqd,bhkd->bhqk', qb, k).astype(jnp.float32) * scale
        qp, kp = _block_positions(i, s)
        strict = kp < qp
        log_not_break = jnp.where(strict, jax.nn.log_sigmoid(-z), 0.0)
        later = lax.cumsum(log_not_break, axis=3, reverse=True) - log_not_break
        weights = jnp.where(strict, jnp.exp(jax.nn.log_sigmoid(z) + later), 0.0)
        return jnp.einsum('bhqk,bhkd->bhqd', weights.astype(v.dtype), v)

    return _sweep_query_blocks(block, s)


def _forgetting_attention(q, k, v, f_logit, b_forget, q_gain, k_gain):
    b, s, _ = q.shape
    q = _rms_norm(_heads(q, FOX_HEADS), q_gain)
    k = _rms_norm(_heads(k, FOX_HEADS), k_gain)
    v = _heads(v, FOX_HEADS)
    log_f = jax.nn.log_sigmoid(f_logit.astype(jnp.float32) + b_forget.astype(jnp.float32))
    cum = jnp.cumsum(log_f, axis=1).transpose(0, 2, 1)
    scale = HEAD_DIM ** -0.5

    def block(i):
        qb = lax.dynamic_slice_in_dim(q, i * Q_BLOCK, Q_BLOCK, axis=2)
        cq = lax.dynamic_slice_in_dim(cum, i * Q_BLOCK, Q_BLOCK, axis=2)
        logits = (jnp.einsum('bhqd,bhkd->bhqk', qb, k).astype(jnp.float32) * scale
                  + cq[..., :, None] - cum[..., None, :])
        qp, kp = _block_positions(i, s)
        probs = jax.nn.softmax(jnp.where(kp <= qp, logits, -jnp.inf), axis=-1)
        return jnp.einsum('bhqk,bhkd->bhqd', probs.astype(v.dtype), v)

    return _sweep_query_blocks(block, s)


def _hybrid_mixer(h, w_in, b_gate, b_forget, pool_w, pool_scale, diff_q_norm, diff_k_norm,
                  diff_subln, lam_q1, lam_k1, lam_q2, lam_k2, fox_q_norm, fox_k_norm,
                  w_branch, w_out, lambda_init, positions):
    b, s, _ = h.shape
    proj = jnp.einsum('bsd,dn->bsn', h, w_in)
    offsets = np.cumsum(IN_SPLITS)[:-1].tolist()
    (u_pool, dq, dk, dv, sq, sk, sv, fq, fk, fv, f_logit, g_logit) = jnp.split(proj, offsets, axis=-1)

    o_pool = _pool_mixer(u_pool, pool_w, pool_scale)
    o_diff = _diff_attention(dq, dk, dv, diff_q_norm, diff_k_norm, diff_subln,
                             lam_q1, lam_k1, lam_q2, lam_k2, lambda_init, positions)
    o_sb = _stick_breaking_attention(sq, sk, sv)
    o_fox = _forgetting_attention(fq, fk, fv, f_logit, b_forget, fox_q_norm, fox_k_norm)

    gates = jax.nn.sigmoid(g_logit.reshape(b, s, N_BRANCHES, D_MODEL) + b_gate)
    row_starts = [0] + np.cumsum(BRANCH_WIDTHS)[:-1].tolist()
    merged = jnp.zeros_like(h)
    for n, o in enumerate((o_pool, o_diff, o_sb, o_fox)):
        w_n = w_branch[row_starts[n]:row_starts[n] + BRANCH_WIDTHS[n]]
        merged = merged + gates[:, :, n] * jnp.einsum('bsc,cd->bsd', o, w_n)
    return jnp.einsum('bsd,de->bse', merged, w_out)


def _swiglu(h, w_up, w_down):
    gate, up = jnp.split(jnp.einsum('bsd,df->bsf', h, w_up), 2, axis=-1)
    return jnp.einsum('bsf,fd->bsd', jax.nn.silu(gate) * up, w_down)


def _normal(k, shape, scale):
    return scale * jax.random.normal(k, shape, jnp.float32)


def setup_inputs(seed: int = 0) -> dict:
    key = jax.random.key(seed)
    ks = jax.random.split(key, 22)
    branch_row_scale = jnp.concatenate(
        [jnp.full((w,), w ** -0.5, jnp.float32) for w in BRANCH_WIDTHS])[:, None]
    return {
        'x': _normal(ks[0], (BATCH, SEQ, D_MODEL), 1.0),
        'attn_norm': 1.0 + _normal(ks[1], (DEPTH, D_MODEL), 0.02),
        'ffn_norm': 1.0 + _normal(ks[2], (DEPTH, D_MODEL), 0.02),
        'w_in': _normal(ks[3], (DEPTH, D_MODEL, D_IN), D_MODEL ** -0.5),
        'b_gate': _normal(ks[4], (DEPTH, N_BRANCHES, D_MODEL), 0.1),
        'b_forget': 3.0 + _normal(ks[5], (DEPTH, FOX_HEADS), 0.5),
        'pool_w': _normal(ks[6], (DEPTH, POOL_GROUPS, POOL_GROUP_DIM, POOL_GROUP_DIM), POOL_GROUP_DIM ** -0.5),
        'pool_scale': 1.0 + _normal(ks[7], (DEPTH, POOL_WIDTH), 0.02),
        'diff_q_norm': 1.0 + _normal(ks[8], (DEPTH, HEAD_DIM), 0.02),
        'diff_k_norm': 1.0 + _normal(ks[9], (DEPTH, HEAD_DIM), 0.02),
        'diff_subln': 1.0 + _normal(ks[10], (DEPTH, DIFF_V_DIM), 0.02),
        'lam_q1': _normal(ks[11], (DEPTH, HEAD_DIM), 0.1),
        'lam_k1': _normal(ks[12], (DEPTH, HEAD_DIM), 0.1),
        'lam_q2': _normal(ks[13], (DEPTH, HEAD_DIM), 0.1),
        'lam_k2': _normal(ks[14], (DEPTH, HEAD_DIM), 0.1),
        'fox_q_norm': 1.0 + _normal(ks[15], (DEPTH, HEAD_DIM), 0.02),
        'fox_k_norm': 1.0 + _normal(ks[16], (DEPTH, HEAD_DIM), 0.02),
        'w_branch': _normal(ks[17], (DEPTH, MIX_WIDTH, D_MODEL), 1.0) * branch_row_scale,
        'w_out': _normal(ks[18], (DEPTH, D_MODEL, D_MODEL), D_MODEL ** -0.5),
        'w_ffn_up': _normal(ks[19], (DEPTH, D_MODEL, 2 * D_FF), D_MODEL ** -0.5),
        'w_ffn_down': _normal(ks[20], (DEPTH, D_FF, D_MODEL), D_FF ** -0.5),
    }


def reference(x, attn_norm, ffn_norm, w_in, b_gate, b_forget, pool_w, pool_scale,
              diff_q_norm, diff_k_norm, diff_subln, lam_q1, lam_k1, lam_q2, lam_k2,
              fox_q_norm, fox_k_norm, w_branch, w_out, w_ffn_up, w_ffn_down):
    positions = jnp.arange(x.shape[1], dtype=jnp.int32)
    for layer in range(DEPTH):
        h = _rms_norm(x, attn_norm[layer])
        x = x + _hybrid_mixer(h, w_in[layer], b_gate[layer], b_forget[layer], pool_w[layer],
                              pool_scale[layer], diff_q_norm[layer], diff_k_norm[layer],
                              diff_subln[layer], lam_q1[layer], lam_k1[layer], lam_q2[layer],
                              lam_k2[layer], fox_q_norm[layer], fox_k_norm[layer],
                              w_branch[layer], w_out[layer], _diff_lambda_init(layer), positions)
        h = _rms_norm(x, ffn_norm[layer])
        x = x + _swiglu(h, w_ffn_up[layer], w_ffn_down[layer])
    return x
```

```python
import functools
import math

import numpy as np
import jax
import jax.numpy as jnp
from jax import lax
from jax.experimental import pallas as pl
from jax.experimental.pallas import tpu as pltpu

F32 = jnp.float32
BF16 = jnp.bfloat16

HEAD_DIM = 64
POOL_WINDOWS = (2, 4, 8, 16)
POOL_WIDTH = 256
DIFF_HEADS = 4
DIFF_QK_WIDTH = 512
DIFF_WIDTH = 512
SB_WIDTH = 256
FOX_HEADS = 4
FOX_WIDTH = 256
N_BRANCHES = 4
BRANCH_WIDTHS = (POOL_WIDTH, DIFF_WIDTH, SB_WIDTH, FOX_WIDTH)
MAIN_WIDTH = 3328
ROPE_THETA = 10000.0
NORM_EPS = 1e-6
QK_SCALE = HEAD_DIM ** -0.5

LANES = 128
ATTN_BLOCK = 256
ROW_TILE = 512
POOL_HALO = 16
AUG_LANE = 64
VMEM_LIMIT = 56 * 1024 * 1024


def _diff_lambda_init(layer):
    return 0.8 - 0.6 * math.exp(-0.3 * layer)


def _cparams(n_axes):
    return pltpu.CompilerParams(dimension_semantics=("arbitrary",) * n_axes,
                                vmem_limit_bytes=VMEM_LIMIT)


def _full(shape):
    return pl.BlockSpec(shape, lambda *_: (0,) * len(shape))


def _rms_rows(x, gain):
    ms = jnp.mean(x * x, axis=-1, keepdims=True)
    return x * lax.rsqrt(ms + NORM_EPS) * gain


def _bdot(a, b):
    return jnp.dot(a, b, preferred_element_type=F32)


def _nt_dot(a, b):
    return lax.dot_general(a, b, (((1,), (1,)), ((), ())), preferred_element_type=F32)


def _split_head_pair(q):
    lane = lax.broadcasted_iota(jnp.int32, q.shape, 1)
    qf = q.astype(F32)
    return (jnp.where(lane < HEAD_DIM, qf, 0.0).astype(q.dtype),
            jnp.where(lane >= HEAD_DIM, qf, 0.0).astype(q.dtype))


def _split3(v):
    a1 = v.astype(BF16)
    r1 = v - a1.astype(F32)
    a2 = r1.astype(BF16)
    r2 = r1 - a2.astype(F32)
    a3 = r2.astype(BF16)
    return a1, a2, a3


def _rope_table_kernel(inv_ref, cos_ref, sin_ref):
    rows = cos_ref.shape[0]
    base = pl.program_id(0) * rows
    pos = (base + lax.broadcasted_iota(jnp.int32, (rows, LANES), 0)).astype(F32)
    ang = pos * inv_ref[...]
    lane = lax.broadcasted_iota(jnp.int32, (rows, LANES), 1)
    first_half = (lane & (HEAD_DIM // 2)) == 0
    s = jnp.sin(ang)
    cos_ref[...] = jnp.cos(ang)
    sin_ref[...] = jnp.where(first_half, -s, s)


def _rope_tables(seq):
    half = HEAD_DIM // 2
    inv_freq = ROPE_THETA ** (-jnp.arange(half, dtype=F32) / half)
    inv = jnp.tile(inv_freq, LANES // half)[None, :]
    rows = min(seq, ROW_TILE)
    return pl.pallas_call(
        _rope_table_kernel,
        grid=(seq // rows,),
        in_specs=[_full((1, LANES))],
        out_specs=[pl.BlockSpec((rows, LANES), lambda i: (i, 0))] * 2,
        out_shape=[jax.ShapeDtypeStruct((seq, LANES), F32)] * 2,
        compiler_params=_cparams(1),
        name="rope_tables",
    )(inv)


def _proj_kernel(x_ref, an_ref, w_ref, bf_ref, cos_ref, sin_ref, gains_ref, gmat_ref,
                 ltri_ref, sel_ref,
                 u_ref, dq_ref, dk_ref, dv_ref, sq_ref, sk_ref, sv_ref, fq_ref, fk_ref, fv_ref,
                 carry_ref):
    tm = x_ref.shape[0]
    h = _rms_rows(x_ref[...], an_ref[...]).astype(BF16)

    def proj(lo, width):
        return _bdot(h, w_ref[:, lo:lo + width])

    def group_norm(y, gain):
        ms = _bdot((y * y).astype(BF16), gmat_ref[...])
        return y * lax.rsqrt(ms + NORM_EPS) * gain

    cos = cos_ref[...]
    sin = sin_ref[...]
    lane = lax.broadcasted_iota(jnp.int32, (tm, LANES), 1)
    first_half = (lane & (HEAD_DIM // 2)) == 0
    low_head = lane < HEAD_DIM

    def rope(y):
        partner = jnp.where(first_half, pltpu.roll(y, LANES - HEAD_DIM // 2, 1),
                            pltpu.roll(y, HEAD_DIM // 2, 1))
        return y * cos + partner * sin

    u_ref[...] = proj(0, 256)

    gq = gains_ref[0:1, :]
    gk = gains_ref[1:2, :]
    for c in range(2):
        yq = group_norm(proj(256 + 256 * c, 256), gq)
        yk = group_norm(proj(768 + 256 * c, 256), gk)
        for hh in range(2):
            sl = slice(hh * LANES, (hh + 1) * LANES)
            col = slice(256 * c + hh * LANES, 256 * c + (hh + 1) * LANES)
            dq_ref[:, col] = (rope(yq[:, sl]) * QK_SCALE).astype(BF16)
            dk_ref[:, col] = rope(yk[:, sl]).astype(BF16)
        dv_ref[:, 256 * c:256 * (c + 1)] = proj(1280 + 256 * c, 256).astype(BF16)

    sq_ref[...] = (proj(1792, 256) * QK_SCALE).astype(BF16)
    sk_ref[...] = proj(2048, 256).astype(BF16)
    sv_ref[...] = proj(2304, 256).astype(BF16)
    fv_ref[...] = proj(3072, 256).astype(BF16)

    fl = proj(MAIN_WIDTH, LANES) + bf_ref[...]
    logf = jnp.minimum(fl, 0.0) - jnp.log1p(jnp.exp(-jnp.abs(fl)))

    @pl.when(pl.program_id(1) == 0)
    def _():
        carry_ref[...] = jnp.zeros_like(carry_ref)

    carry = carry_ref[0:1, :]
    sub = ltri_ref.shape[0]
    pieces = []
    for sb in range(tm // sub):
        a1, a2, a3 = _split3(logf[sb * sub:(sb + 1) * sub, :])
        ltri = ltri_ref[...]
        cum = _bdot(ltri, a1) + _bdot(ltri, a2) + _bdot(ltri, a3) + carry
        carry = cum[sub - 1:sub, :]
        pieces.append(cum)
    carry_ref[0:1, :] = carry
    cum = jnp.concatenate(pieces, axis=0) if len(pieces) > 1 else pieces[0]

    c1, c2, c3 = (c.astype(F32) for c in _split3(cum))
    packed = jnp.where(lane < 4, c1, jnp.where(lane < 8, c2, jnp.where(lane < 12, c3,
                       jnp.where(lane == 12, 1.0, 0.0))))
    aug = _bdot(packed.astype(BF16), sel_ref[...])

    yq = group_norm(proj(2560, 256), gains_ref[2:3, :]) * QK_SCALE
    yk = group_norm(proj(2816, 256), gains_ref[3:4, :])
    for src, dst, off in ((yq, fq_ref, 0), (yk, fk_ref, 4 * LANES)):
        for pair in range(2):
            v = src[:, pair * LANES:(pair + 1) * LANES]
            heads = (v, pltpu.roll(v, HEAD_DIM, 1))
            for hh in range(2):
                col = (2 * pair + hh) * LANES
                dst[:, col:col + LANES] = jnp.where(
                    low_head, heads[hh], aug[:, off + col:off + col + LANES]).astype(BF16)


def _proj_call(xf, an, w_all, bf_pad, cos, sin, gains, gmat, ltri, sel, batch, seq):
    n, d = xf.shape
    tm = min(seq, ROW_TILE)
    ns = seq // tm
    row = lambda b, s: (b * ns + s, 0)
    widths = (256, 512, 512, 512, 256, 256, 256, 512, 512, 256)
    dtypes = (F32,) + (BF16,) * 9
    return pl.pallas_call(
        _proj_kernel,
        grid=(batch, ns),
        in_specs=[pl.BlockSpec((tm, d), row), _full(an.shape), _full(w_all.shape), _full(bf_pad.shape),
                  pl.BlockSpec((tm, LANES), lambda b, s: (s, 0)),
                  pl.BlockSpec((tm, LANES), lambda b, s: (s, 0)),
                  _full(gains.shape), _full(gmat.shape), _full(ltri.shape), _full(sel.shape)],
        out_specs=[pl.BlockSpec((tm, w), row) for w in widths],
        out_shape=[jax.ShapeDtypeStruct((n, w), t) for w, t in zip(widths, dtypes)],
        scratch_shapes=[pltpu.VMEM((8, LANES), F32)],
        compiler_params=_cparams(2),
        name="norm_in_proj",
    )(xf, an, w_all, bf_pad, cos, sin, gains, gmat, ltri, sel)


def _causal_mask(rows, t, strict):
    r = lax.broadcasted_iota(jnp.int32, (rows, t), 0) & (t - 1)
    c = lax.broadcasted_iota(jnp.int32, (rows, t), 1)
    return (c < r) if strict else (c <= r)


def _softmax_sweep(q, k_ref, v_ref, i, t, k_cols=None):
    rows = q.shape[0]
    mask = _causal_mask(rows, t, strict=False)

    def step(j, carry, masked):
        m, l, acc = carry
        start = pl.multiple_of(j * t, t)
        kb = k_ref[pl.ds(start, t), :] if k_cols is None else k_ref[pl.ds(start, t), k_cols]
        vb = v_ref[pl.ds(start, t), :]
        s = _nt_dot(q, kb)
        if masked:
            s = jnp.where(mask, s, -jnp.inf)
        m_new = jnp.maximum(m, s.max(axis=1, keepdims=True))
        alpha = jnp.exp(m - m_new)
        p = jnp.exp(s - m_new)
        l = alpha * l + p.sum(axis=1, keepdims=True)
        acc = alpha * acc + _bdot(p.astype(BF16), vb)
        return m_new, l, acc

    init = (jnp.full((rows, 1), -1e30, F32), jnp.zeros((rows, 1), F32),
            jnp.zeros((rows, v_ref.shape[1]), F32))
    carry = lax.fori_loop(0, i, lambda j, c: step(j, c, False), init)
    _, l, acc = step(i, carry, True)
    return acc, l


def _diff_kernel(q_ref, k_ref, v_ref, lam_ref, sub_ref, o_ref, *, lambda_init):
    t = q_ref.shape[0]
    i = pl.program_id(2)
    q_lo, q_hi = _split_head_pair(q_ref[...])
    qq = jnp.concatenate([q_lo, q_hi], axis=0)
    acc, l = _softmax_sweep(qq, k_ref, v_ref, i, t)
    o = acc / l
    lp = lam_ref[...]
    lam = (jnp.exp(jnp.sum(lp[0:1, :] * lp[1:2, :], axis=1, keepdims=True))
           - jnp.exp(jnp.sum(lp[2:3, :] * lp[3:4, :], axis=1, keepdims=True)) + lambda_init)
    od = o[:t, :] - lam * o[t:, :]
    o_ref[...] = (_rms_rows(od, sub_ref[...]) * (1.0 - lambda_init)).astype(o_ref.dtype)


def _sb_kernel(q_ref, k_ref, v_ref, o_ref):
    t = q_ref.shape[0]
    i = pl.program_id(2)
    q_lo, q_hi = _split_head_pair(q_ref[...])
    lane = lax.broadcasted_iota(jnp.int32, (t, LANES), 1)
    strict = _causal_mask(t, t, strict=True)
    suffix = (lax.broadcasted_iota(jnp.int32, (t, t), 0)
              >= lax.broadcasted_iota(jnp.int32, (t, t), 1)).astype(F32).astype(BF16)

    def head(qh):
        def step(j, carry, masked):
            later, acc = carry
            start = pl.multiple_of(j * t, t)
            kb = k_ref[pl.ds(start, t), :]
            vb = v_ref[pl.ds(start, t), :]
            z = _nt_dot(qh, kb)
            lnb = -(jnp.maximum(z, 0.0) + jnp.log1p(jnp.exp(-jnp.abs(z))))
            if masked:
                lnb = jnp.where(strict, lnb, 0.0)
            hi = lnb.astype(BF16)
            lo = (lnb - hi.astype(F32)).astype(BF16)
            csum = _bdot(hi, suffix) + _bdot(lo, suffix)
            w = jnp.exp(z + csum + later)
            if masked:
                w = jnp.where(strict, w, 0.0)
            acc = acc + _bdot(w.astype(BF16), vb)
            return later + csum[:, 0:1], acc

        carry = step(i, (jnp.zeros((t, 1), F32), jnp.zeros((t, LANES), F32)), True)
        _, acc = lax.fori_loop(0, i, lambda n, c: step(i - 1 - n, c, False), carry)
        return acc

    acc_lo = head(q_lo)
    acc_hi = head(q_hi)
    o_ref[...] = jnp.where(lane < HEAD_DIM, acc_lo, acc_hi).astype(o_ref.dtype)


def _fox_kernel(q_ref, k_ref, v_ref, o_ref):
    t = q_ref.shape[0]
    i = pl.program_id(2)
    outs = []
    for hh in range(2):
        cols = slice(hh * LANES, (hh + 1) * LANES)
        acc, l = _softmax_sweep(q_ref[:, cols], k_ref, v_ref, i, t, k_cols=cols)
        outs.append(acc / l)
    lane = lax.broadcasted_iota(jnp.int32, (t, LANES), 1)
    o_ref[...] = jnp.where(lane < HEAD_DIM, outs[0], outs[1]).astype(o_ref.dtype)


def _attention_call(body, name, q, k, v, extra, batch, seq, n_groups, q_width, out_width):
    n = q.shape[0]
    t = min(seq, ATTN_BLOCK)
    nq = seq // t
    return pl.pallas_call(
        body,
        grid=(batch, n_groups, nq),
        in_specs=[pl.BlockSpec((t, q_width), lambda b, g, i: (b * nq + i, g)),
                  pl.BlockSpec((seq, q_width), lambda b, g, i: (b, g)),
                  pl.BlockSpec((seq, LANES), lambda b, g, i: (b, g))] + [_full(e.shape) for e in extra],
        out_specs=pl.BlockSpec((t, LANES), lambda b, g, i: (b * nq + i, g)),
        out_shape=jax.ShapeDtypeStruct((n, n_groups * LANES), BF16),
        compiler_params=_cparams(3),
        name=name,
    )(q, k, v, *extra)


def _merge_kernel(x_ref, u_ref, halo_ref, od_ref, os_ref, of_ref, an_ref, pw_ref, ps_ref,
                  wbr_ref, wg_ref, bg_ref, wout_ref, o_ref):
    tm, d = x_ref.shape
    x = x_ref[...]
    h = _rms_rows(x, an_ref[...]).astype(BF16)

    u = u_ref[...]
    first_tile = pl.program_id(1) == 0
    halo = jnp.where(first_tile, 0.0, halo_ref[...])
    ext = jnp.concatenate([halo, u], axis=0)
    lane = lax.broadcasted_iota(jnp.int32, (tm, POOL_WIDTH), 1)
    group = lane >> 6
    win = jnp.zeros_like(u)
    width = jnp.zeros((tm, POOL_WIDTH), jnp.int32)
    span = 1
    for g, w in enumerate(POOL_WINDOWS):
        while span < w:
            ext = ext + pltpu.roll(ext, span, 0)
            span *= 2
        win = jnp.where(group == g, ext[POOL_HALO:, :], win)
        width = jnp.where(group == g, w, width)
    pos = pl.program_id(1) * tm + lax.broadcasted_iota(jnp.int32, (tm, POOL_WIDTH), 0)
    count = jnp.minimum(pos + 1, width).astype(F32)
    pooled = win / count - u
    o_pool = (_bdot(pooled.astype(BF16), pw_ref[...]) * ps_ref[...]).astype(BF16)

    merged = jnp.zeros((tm, d), F32)
    lo = 0
    for nb, o in enumerate((o_pool, od_ref[...], os_ref[...], of_ref[...])):
        width_n = BRANCH_WIDTHS[nb]
        y = _bdot(o, wbr_ref[lo:lo + width_n, :])
        gate = jax.nn.sigmoid(_bdot(h, wg_ref[:, nb * d:(nb + 1) * d]) + bg_ref[nb:nb + 1, :])
        merged = merged + gate * y
        lo += width_n
    o_ref[...] = x + _bdot(merged.astype(BF16), wout_ref[...])


def _merge_call(xf, u, od, os_, of, an, pw, ps, wbr, wg, bg, wout, batch, seq):
    n, d = xf.shape
    tm = min(seq, ROW_TILE)
    ns = seq // tm
    row = lambda b, s: (b * ns + s, 0)
    halo_blocks = tm // POOL_HALO
    halo = lambda b, s: (jnp.maximum((b * ns + s) * halo_blocks - 1, 0), 0)
    return pl.pallas_call(
        _merge_kernel,
        grid=(batch, ns),
        in_specs=[pl.BlockSpec((tm, d), row), pl.BlockSpec((tm, POOL_WIDTH), row),
                  pl.BlockSpec((POOL_HALO, POOL_WIDTH), halo),
                  pl.BlockSpec((tm, DIFF_WIDTH), row), pl.BlockSpec((tm, SB_WIDTH), row),
                  pl.BlockSpec((tm, FOX_WIDTH), row),
                  _full(an.shape), _full(pw.shape), _full(ps.shape), _full(wbr.shape),
                  _full(wg.shape), _full(bg.shape), _full(wout.shape)],
        out_specs=pl.BlockSpec((tm, d), row),
        out_shape=jax.ShapeDtypeStruct((n, d), F32),
        compiler_params=_cparams(2),
        name="merge_out_proj",
    )(xf, u, u, od, os_, of, an, pw, ps, wbr, wg, bg, wout)


def _ffn_kernel(x_ref, fn_ref, wup_ref, wdn_ref, o_ref, *, chunk):
    x = x_ref[...]
    h = _rms_rows(x, fn_ref[...]).astype(BF16)
    d_ff = wdn_ref.shape[0]
    acc = x
    for c in range(d_ff // chunk):
        gate = _bdot(h, wup_ref[:, c * chunk:(c + 1) * chunk])
        up = _bdot(h, wup_ref[:, d_ff + c * chunk:d_ff + (c + 1) * chunk])
        act = (gate * jax.nn.sigmoid(gate) * up).astype(BF16)
        acc = acc + _bdot(act, wdn_ref[c * chunk:(c + 1) * chunk, :])
    o_ref[...] = acc


def _ffn_call(xf, fn, wup, wdn):
    n, d = xf.shape
    tm = min(n, ROW_TILE)
    return pl.pallas_call(
        functools.partial(_ffn_kernel, chunk=256),
        grid=(n // tm,),
        in_specs=[pl.BlockSpec((tm, d), lambda i: (i, 0)), _full(fn.shape), _full(wup.shape),
                  _full(wdn.shape)],
        out_specs=pl.BlockSpec((tm, d), lambda i: (i, 0)),
        out_shape=jax.ShapeDtypeStruct((n, d), F32),
        compiler_params=_cparams(1),
        name="swiglu_ffn",
    )(xf, fn, wup, wdn)


def _group_mean_matrix():
    g = np.kron(np.eye(256 // HEAD_DIM), np.full((HEAD_DIM, HEAD_DIM), 1.0 / HEAD_DIM))
    return jnp.asarray(g, BF16)


def _lower_tri(nrows):
    return jnp.asarray(np.tril(np.ones((nrows, nrows))), BF16)


def _decay_selector():
    sel = np.zeros((LANES, 2 * FOX_HEADS * LANES), np.float32)
    for hd in range(FOX_HEADS):
        qb = hd * LANES + AUG_LANE
        kb = (FOX_HEADS + hd) * LANES + AUG_LANE
        for term in range(3):
            sel[4 * term + hd, qb + term] = 1.0
            sel[12, qb + 3 + term] = 1.0
            sel[12, kb + term] = 1.0
            sel[4 * term + hd, kb + 3 + term] = -1.0
    return jnp.asarray(sel, BF16)


def kernel(x, attn_norm, ffn_norm, w_in, b_gate, b_forget, pool_w, pool_scale, diff_q_norm,
           diff_k_norm, diff_subln, lam_q1, lam_k1, lam_q2, lam_k2, fox_q_norm, fox_k_norm,
           w_branch, w_out, w_ffn_up, w_ffn_down):
    batch, seq, d = x.shape
    depth = attn_norm.shape[0]
    assert seq % min(seq, ROW_TILE) == 0 and seq % min(seq, ATTN_BLOCK) == 0
    xf = x.reshape(batch * seq, d)
    cos, sin = _rope_tables(seq)
    gmat = _group_mean_matrix()
    ltri = _lower_tri(min(seq, 256))
    sel = _decay_selector()

    for layer in range(depth):
        w = w_in[layer]
        wf = w[:, MAIN_WIDTH:MAIN_WIDTH + FOX_HEADS]
        wf_pad = jnp.concatenate([wf, wf, wf, jnp.zeros((d, LANES - 3 * FOX_HEADS), w.dtype)], axis=1)
        w_all = jnp.concatenate([w[:, :MAIN_WIDTH], wf_pad], axis=1).astype(BF16)
        bfg = b_forget[layer]
        bf_pad = jnp.concatenate([bfg, bfg, bfg, jnp.zeros((LANES - 3 * FOX_HEADS,), F32)])[None, :]
        gains = jnp.stack([jnp.tile(g[layer], 256 // HEAD_DIM)
                           for g in (diff_q_norm, diff_k_norm, fox_q_norm, fox_k_norm)])
        an = attn_norm[layer][None, :]

        u, dq, dk, dv, sq, sk, sv, fq, fk, fv = _proj_call(
            xf, an, w_all, bf_pad, cos, sin, gains, gmat, ltri, sel, batch, seq)

        lam = jnp.stack([lam_q1[layer], lam_k1[layer], lam_q2[layer], lam_k2[layer]])
        od = _attention_call(
            functools.partial(_diff_kernel, lambda_init=_diff_lambda_init(layer)), "diff_attention",
            dq, dk, dv, (lam, diff_subln[layer][None, :]), batch, seq, DIFF_HEADS, LANES, DIFF_WIDTH)
        os_ = _attention_call(_sb_kernel, "stick_breaking_attention", sq, sk, sv, (), batch, seq,
                              2, LANES, SB_WIDTH)
        of = _attention_call(_fox_kernel, "forgetting_attention", fq, fk, fv, (), batch, seq,
                             2, 2 * LANES, FOX_WIDTH)

        pw = jax.scipy.linalg.block_diag(*[pool_w[layer, g] for g in range(len(POOL_WINDOWS))])
        x1 = _merge_call(xf, u, od, os_, of, an, pw.astype(BF16), pool_scale[layer][None, :],
                         w_branch[layer].astype(BF16), w[:, MAIN_WIDTH + FOX_HEADS:].astype(BF16),
                         b_gate[layer], w_out[layer].astype(BF16), batch, seq)
        xf = _ffn_call(x1, ffn_norm[layer][None, :], w_ffn_up[layer].astype(BF16),
                       w_ffn_down[layer].astype(BF16))
    return xf.reshape(batch, seq, d)
```

```python
import functools
import math

import numpy as np
import jax
import jax.numpy as jnp
from jax import lax
from jax.experimental import pallas as pl
from jax.experimental.pallas import tpu as pltpu

F32 = jnp.float32
BF16 = jnp.bfloat16

HEAD_DIM = 64
POOL_WINDOWS = (2, 4, 8, 16)
POOL_WIDTH = 256
DIFF_HEADS = 4
DIFF_QK_WIDTH = 512
DIFF_WIDTH = 512
SB_WIDTH = 256
FOX_HEADS = 4
FOX_WIDTH = 256
N_BRANCHES = 4
BRANCH_WIDTHS = (POOL_WIDTH, DIFF_WIDTH, SB_WIDTH, FOX_WIDTH)
MAIN_WIDTH = 3328
ROPE_THETA = 10000.0
NORM_EPS = 1e-6
QK_SCALE = HEAD_DIM ** -0.5

LANES = 128
ATTN_Q_BLOCK = 512
SOFTMAX_K_BLOCK = 512
SB_K_BLOCK = 256
ROW_TILE = 512
ROW_CHUNK = 128
POOL_HALO = 16
AUG_LANE = 64
VMEM_LIMIT = 56 * 1024 * 1024


def _diff_lambda_init(layer):
    return 0.8 - 0.6 * math.exp(-0.3 * layer)


def _cparams(n_axes):
    return pltpu.CompilerParams(dimension_semantics=("arbitrary",) * n_axes,
                                vmem_limit_bytes=VMEM_LIMIT)


def _full(shape):
    return pl.BlockSpec(shape, lambda *_: (0,) * len(shape))


def _rms_rows(x, gain):
    ms = jnp.mean(x * x, axis=-1, keepdims=True)
    return x * lax.rsqrt(ms + NORM_EPS) * gain


def _bdot(a, b):
    return jnp.dot(a, b, preferred_element_type=F32)


def _nt_dot(a, b):
    return lax.dot_general(a, b, (((1,), (1,)), ((), ())), preferred_element_type=F32)


def _split_head_pair(q):
    lane = lax.broadcasted_iota(jnp.int32, q.shape, 1)
    qf = q.astype(F32)
    return (jnp.where(lane < HEAD_DIM, qf, 0.0).astype(q.dtype),
            jnp.where(lane >= HEAD_DIM, qf, 0.0).astype(q.dtype))


def _split3(v):
    a1 = v.astype(BF16)
    r1 = v - a1.astype(F32)
    a2 = r1.astype(BF16)
    r2 = r1 - a2.astype(F32)
    a3 = r2.astype(BF16)
    return a1, a2, a3


def _rope_table_kernel(inv_ref, cos_ref, sin_ref):
    rows = cos_ref.shape[0]
    base = pl.program_id(0) * rows
    pos = (base + lax.broadcasted_iota(jnp.int32, (rows, LANES), 0)).astype(F32)
    ang = pos * inv_ref[...]
    lane = lax.broadcasted_iota(jnp.int32, (rows, LANES), 1)
    first_half = (lane & (HEAD_DIM // 2)) == 0
    s = jnp.sin(ang)
    cos_ref[...] = jnp.cos(ang)
    sin_ref[...] = jnp.where(first_half, -s, s)


def _rope_tables(seq):
    half = HEAD_DIM // 2
    inv_freq = ROPE_THETA ** (-jnp.arange(half, dtype=F32) / half)
    inv = jnp.tile(inv_freq, LANES // half)[None, :]
    rows = min(seq, ROW_TILE)
    return pl.pallas_call(
        _rope_table_kernel,
        grid=(seq // rows,),
        in_specs=[_full((1, LANES))],
        out_specs=[pl.BlockSpec((rows, LANES), lambda i: (i, 0))] * 2,
        out_shape=[jax.ShapeDtypeStruct((seq, LANES), F32)] * 2,
        compiler_params=_cparams(1),
        name="rope_tables",
    )(inv)


def _proj_kernel(x_ref, an_ref, w_ref, bf_ref, cos_ref, sin_ref, gains_ref, gmat_ref,
                 ltri_ref, sel_ref,
                 u_ref, dq_ref, dk_ref, dv_ref, sq_ref, sk_ref, sv_ref, fq_ref, fk_ref, fv_ref,
                 carry_ref):
    tm = x_ref.shape[0]
    h = _rms_rows(x_ref[...], an_ref[...]).astype(BF16)

    def proj(lo, width):
        return _bdot(h, w_ref[:, lo:lo + width])

    def group_norm(y, gain):
        ms = _bdot((y * y).astype(BF16), gmat_ref[...])
        return y * lax.rsqrt(ms + NORM_EPS) * gain

    cos = cos_ref[...]
    sin = sin_ref[...]
    lane = lax.broadcasted_iota(jnp.int32, (tm, LANES), 1)
    first_half = (lane & (HEAD_DIM // 2)) == 0
    low_head = lane < HEAD_DIM

    def rope(y):
        partner = jnp.where(first_half, pltpu.roll(y, LANES - HEAD_DIM // 2, 1),
                            pltpu.roll(y, HEAD_DIM // 2, 1))
        return y * cos + partner * sin

    u_ref[...] = proj(0, 256)

    gq = gains_ref[0:1, :]
    gk = gains_ref[1:2, :]
    for c in range(2):
        yq = group_norm(proj(256 + 256 * c, 256), gq)
        yk = group_norm(proj(768 + 256 * c, 256), gk)
        for hh in range(2):
            sl = slice(hh * LANES, (hh + 1) * LANES)
            col = slice(256 * c + hh * LANES, 256 * c + (hh + 1) * LANES)
            dq_ref[:, col] = (rope(yq[:, sl]) * QK_SCALE).astype(BF16)
            dk_ref[:, col] = rope(yk[:, sl]).astype(BF16)
        dv_ref[:, 256 * c:256 * (c + 1)] = proj(1280 + 256 * c, 256).astype(BF16)

    sq_ref[...] = (proj(1792, 256) * QK_SCALE).astype(BF16)
    sk_ref[...] = proj(2048, 256).astype(BF16)
    sv_ref[...] = proj(2304, 256).astype(BF16)
    fv_ref[...] = proj(3072, 256).astype(BF16)

    fl = proj(MAIN_WIDTH, LANES) + bf_ref[...]
    logf = jnp.minimum(fl, 0.0) - jnp.log1p(jnp.exp(-jnp.abs(fl)))

    @pl.when(pl.program_id(1) == 0)
    def _():
        carry_ref[...] = jnp.zeros_like(carry_ref)

    carry = carry_ref[0:1, :]
    sub = ltri_ref.shape[0]
    pieces = []
    for sb in range(tm // sub):
        a1, a2, a3 = _split3(logf[sb * sub:(sb + 1) * sub, :])
        ltri = ltri_ref[...]
        cum = _bdot(ltri, a1) + _bdot(ltri, a2) + _bdot(ltri, a3) + carry
        carry = cum[sub - 1:sub, :]
        pieces.append(cum)
    carry_ref[0:1, :] = carry
    cum = jnp.concatenate(pieces, axis=0) if len(pieces) > 1 else pieces[0]

    c1, c2, c3 = (c.astype(F32) for c in _split3(cum))
    packed = jnp.where(lane < 4, c1, jnp.where(lane < 8, c2, jnp.where(lane < 12, c3,
                       jnp.where(lane == 12, 1.0, 0.0))))
    aug = _bdot(packed.astype(BF16), sel_ref[...])

    yq = group_norm(proj(2560, 256), gains_ref[2:3, :]) * QK_SCALE
    yk = group_norm(proj(2816, 256), gains_ref[3:4, :])
    for src, dst, off in ((yq, fq_ref, 0), (yk, fk_ref, 4 * LANES)):
        for pair in range(2):
            v = src[:, pair * LANES:(pair + 1) * LANES]
            heads = (v, pltpu.roll(v, HEAD_DIM, 1))
            for hh in range(2):
                col = (2 * pair + hh) * LANES
                dst[:, col:col + LANES] = jnp.where(
                    low_head, heads[hh], aug[:, off + col:off + col + LANES]).astype(BF16)


def _proj_call(xf, an, w_all, bf_pad, cos, sin, gains, gmat, ltri, sel, batch, seq):
    n, d = xf.shape
    tm = min(seq, ROW_TILE)
    ns = seq // tm
    row = lambda b, s: (b * ns + s, 0)
    widths = (256, 512, 512, 512, 256, 256, 256, 512, 512, 256)
    dtypes = (F32,) + (BF16,) * 9
    return pl.pallas_call(
        _proj_kernel,
        grid=(batch, ns),
        in_specs=[pl.BlockSpec((tm, d), row), _full(an.shape), _full(w_all.shape), _full(bf_pad.shape),
                  pl.BlockSpec((tm, LANES), lambda b, s: (s, 0)),
                  pl.BlockSpec((tm, LANES), lambda b, s: (s, 0)),
                  _full(gains.shape), _full(gmat.shape), _full(ltri.shape), _full(sel.shape)],
        out_specs=[pl.BlockSpec((tm, w), row) for w in widths],
        out_shape=[jax.ShapeDtypeStruct((n, w), t) for w, t in zip(widths, dtypes)],
        scratch_shapes=[pltpu.VMEM((8, LANES), F32)],
        compiler_params=_cparams(2),
        name="norm_in_proj",
    )(xf, an, w_all, bf_pad, cos, sin, gains, gmat, ltri, sel)


def _key_minus_query(row0, rows, tq, tk):
    r = (row0 + lax.broadcasted_iota(jnp.int32, (rows, tk), 0)) & (tq - 1)
    c = lax.broadcasted_iota(jnp.int32, (rows, tk), 1)
    return c - r


def _softmax_sweep(streams, k_ref, v_ref, scratch, i, tq, tk):
    s_scr, p_scr, m_scr, l_scr, acc_scr = scratch
    total = tq * len(streams)
    n_lane_chunks = tk // LANES
    m_scr[...] = jnp.full(m_scr.shape, -1e30, F32)
    l_scr[...] = jnp.zeros(l_scr.shape, F32)
    acc_scr[...] = jnp.zeros(acc_scr.shape, F32)

    def step(j, limit):
        start = pl.multiple_of(j * tk, tk)
        for n, (q, cols) in enumerate(streams):
            s = _nt_dot(q, k_ref[pl.ds(start, tk), cols])
            if limit is not None:
                s = jnp.where(_key_minus_query(0, tq, tq, tk) <= limit, s, -jnp.inf)
            s_scr[n * tq:(n + 1) * tq, :] = s
        for rc in range(total // ROW_CHUNK):
            rows = slice(rc * ROW_CHUNK, (rc + 1) * ROW_CHUNK)
            mx = s_scr[rows, 0:LANES]
            for c in range(1, n_lane_chunks):
                mx = jnp.maximum(mx, s_scr[rows, c * LANES:(c + 1) * LANES])
            m_old = m_scr[rows, :]
            m_new = jnp.maximum(m_old, mx.max(axis=1, keepdims=True))
            alpha = jnp.exp(m_old - m_new)
            m_scr[rows, :] = m_new
            psum = None
            for c in range(n_lane_chunks):
                p = jnp.exp(s_scr[rows, c * LANES:(c + 1) * LANES] - m_new)
                psum = p if psum is None else psum + p
                p_scr[rows, c * LANES:(c + 1) * LANES] = p.astype(BF16)
            l_scr[rows, :] = alpha * l_scr[rows, :] + psum
            acc_scr[rows, :] = alpha * acc_scr[rows, :]
        acc_scr[...] += _bdot(p_scr[...], v_ref[pl.ds(start, tk), :])

    ratio = tq // tk

    def body(j, _):
        step(j, None)
        return 0

    lax.fori_loop(0, ratio * i, body, 0)
    for d in range(ratio):
        step(ratio * i + d, -d * tk)
    return acc_scr[...], l_scr[...].sum(axis=1, keepdims=True)


def _softmax_scratch(n_rows, tk, v_width):
    return [pltpu.VMEM((n_rows, tk), F32), pltpu.VMEM((n_rows, tk), BF16),
            pltpu.VMEM((n_rows, LANES), F32), pltpu.VMEM((n_rows, LANES), F32),
            pltpu.VMEM((n_rows, v_width), F32)]


def _diff_kernel(q_ref, k_ref, v_ref, lam_ref, sub_ref, o_ref, *scratch, lambda_init, tk):
    tq = q_ref.shape[0]
    i = pl.program_id(2)
    q_lo, q_hi = _split_head_pair(q_ref[...])
    acc, l = _softmax_sweep([(q_lo, slice(None)), (q_hi, slice(None))], k_ref, v_ref, scratch,
                            i, tq, tk)
    o = acc / l
    lp = lam_ref[...]
    lam = (jnp.exp(jnp.sum(lp[0:1, :] * lp[1:2, :], axis=1, keepdims=True))
           - jnp.exp(jnp.sum(lp[2:3, :] * lp[3:4, :], axis=1, keepdims=True)) + lambda_init)
    od = o[:tq, :] - lam * o[tq:, :]
    o_ref[...] = (_rms_rows(od, sub_ref[...]) * (1.0 - lambda_init)).astype(o_ref.dtype)


def _fox_kernel(q_ref, k_ref, v_ref, o_ref, *scratch, tk):
    tq = q_ref.shape[0]
    i = pl.program_id(2)
    streams = [(q_ref[:, hh * LANES:(hh + 1) * LANES], slice(hh * LANES, (hh + 1) * LANES))
               for hh in range(2)]
    acc, l = _softmax_sweep(streams, k_ref, v_ref, scratch, i, tq, tk)
    o = acc / l
    lane = lax.broadcasted_iota(jnp.int32, (tq, LANES), 1)
    o_ref[...] = jnp.where(lane < HEAD_DIM, o[:tq, :], o[tq:, :]).astype(o_ref.dtype)


def _sb_kernel(q_ref, k_ref, v_ref, o_ref, z_scr, hi_scr, lo_scr, c_scr, w_scr, later_scr, acc_scr,
               *, tk):
    tq = q_ref.shape[0]
    i = pl.program_id(2)
    ratio = tq // tk
    heads = _split_head_pair(q_ref[...])
    total = tq * len(heads)
    n_lane_chunks = tk // LANES
    neg_suffix = jnp.where(lax.broadcasted_iota(jnp.int32, (tk, tk), 0)
                           >= lax.broadcasted_iota(jnp.int32, (tk, tk), 1), -1.0, 0.0).astype(BF16)
    later_scr[...] = jnp.zeros(later_scr.shape, F32)
    acc_scr[...] = jnp.zeros(acc_scr.shape, F32)

    def step(j, limit):
        start = pl.multiple_of(j * tk, tk)
        kb = k_ref[pl.ds(start, tk), :]
        for n, qh in enumerate(heads):
            z_scr[n * tq:(n + 1) * tq, :] = _nt_dot(qh, kb)
        for rc in range(total // ROW_CHUNK):
            rows = slice(rc * ROW_CHUNK, (rc + 1) * ROW_CHUNK)
            z = z_scr[rows, :]
            sp = jnp.maximum(z, 0.0) + jnp.log(1.0 + jnp.exp(-jnp.abs(z)))
            if limit is not None:
                sp = jnp.where(_key_minus_query(rc * ROW_CHUNK, ROW_CHUNK, tq, tk) < limit, sp, 0.0)
            hi = sp.astype(BF16)
            hi_scr[rows, :] = hi
            lo_scr[rows, :] = (sp - hi.astype(F32)).astype(BF16)
        c_scr[...] = _bdot(hi_scr[...], neg_suffix) + _bdot(lo_scr[...], neg_suffix)
        for rc in range(total // ROW_CHUNK):
            rows = slice(rc * ROW_CHUNK, (rc + 1) * ROW_CHUNK)
            later = later_scr[rows, :]
            for c in range(n_lane_chunks):
                cols = slice(c * LANES, (c + 1) * LANES)
                w = jnp.exp(z_scr[rows, cols] + c_scr[rows, cols] + later)
                if limit is not None:
                    keep = _key_minus_query(rc * ROW_CHUNK, ROW_CHUNK, tq, tk)[:, cols] < limit
                    w = jnp.where(keep, w, 0.0)
                w_scr[rows, cols] = w.astype(BF16)
            later_scr[rows, :] = later + c_scr[rows, 0:1]
        acc_scr[...] += _bdot(w_scr[...], v_ref[pl.ds(start, tk), :])

    for d in reversed(range(ratio)):
        step(ratio * i + d, -d * tk)

    def body(n, _):
        step(ratio * i - 1 - n, None)
        return 0

    lax.fori_loop(0, ratio * i, body, 0)
    lane = lax.broadcasted_iota(jnp.int32, (tq, LANES), 1)
    o_ref[...] = jnp.where(lane < HEAD_DIM, acc_scr[:tq, :], acc_scr[tq:, :]).astype(o_ref.dtype)


def _sb_scratch(n_rows, tk):
    return [pltpu.VMEM((n_rows, tk), F32), pltpu.VMEM((n_rows, tk), BF16),
            pltpu.VMEM((n_rows, tk), BF16), pltpu.VMEM((n_rows, tk), F32),
            pltpu.VMEM((n_rows, tk), BF16), pltpu.VMEM((n_rows, LANES), F32),
            pltpu.VMEM((n_rows, LANES), F32)]


def _attention_call(body, name, q, k, v, extra, batch, seq, n_groups, q_width, tk, scratch_fn):
    n = q.shape[0]
    tq = min(seq, ATTN_Q_BLOCK)
    tk = min(tk, tq)
    nq = seq // tq
    return pl.pallas_call(
        functools.partial(body, tk=tk),
        grid=(batch, n_groups, nq),
        in_specs=[pl.BlockSpec((tq, q_width), lambda b, g, i: (b * nq + i, g)),
                  pl.BlockSpec((seq, q_width), lambda b, g, i: (b, g)),
                  pl.BlockSpec((seq, LANES), lambda b, g, i: (b, g))] + [_full(e.shape) for e in extra],
        out_specs=pl.BlockSpec((tq, LANES), lambda b, g, i: (b * nq + i, g)),
        out_shape=jax.ShapeDtypeStruct((n, n_groups * LANES), BF16),
        scratch_shapes=scratch_fn(2 * tq, tk),
        compiler_params=_cparams(3),
        name=name,
    )(q, k, v, *extra)


def _merge_kernel(x_ref, u_ref, halo_ref, od_ref, os_ref, of_ref, an_ref, pw_ref, ps_ref,
                  wbr_ref, wg_ref, bg_ref, wout_ref, o_ref):
    tm, d = x_ref.shape
    x = x_ref[...]
    h = _rms_rows(x, an_ref[...]).astype(BF16)

    u = u_ref[...]
    first_tile = pl.program_id(1) == 0
    halo = jnp.where(first_tile, 0.0, halo_ref[...])
    ext = jnp.concatenate([halo, u], axis=0)
    lane = lax.broadcasted_iota(jnp.int32, (tm, POOL_WIDTH), 1)
    group = lane >> 6
    win = jnp.zeros_like(u)
    width = jnp.zeros((tm, POOL_WIDTH), jnp.int32)
    span = 1
    for g, w in enumerate(POOL_WINDOWS):
        while span < w:
            ext = ext + pltpu.roll(ext, span, 0)
            span *= 2
        win = jnp.where(group == g, ext[POOL_HALO:, :], win)
        width = jnp.where(group == g, w, width)
    pos = pl.program_id(1) * tm + lax.broadcasted_iota(jnp.int32, (tm, POOL_WIDTH), 0)
    count = jnp.minimum(pos + 1, width).astype(F32)
    pooled = win / count - u
    o_pool = (_bdot(pooled.astype(BF16), pw_ref[...]) * ps_ref[...]).astype(BF16)

    merged = jnp.zeros((tm, d), F32)
    lo = 0
    for nb, o in enumerate((o_pool, od_ref[...], os_ref[...], of_ref[...])):
        width_n = BRANCH_WIDTHS[nb]
        y = _bdot(o, wbr_ref[lo:lo + width_n, :])
        gate = jax.nn.sigmoid(_bdot(h, wg_ref[:, nb * d:(nb + 1) * d]) + bg_ref[nb:nb + 1, :])
        merged = merged + gate * y
        lo += width_n
    o_ref[...] = x + _bdot(merged.astype(BF16), wout_ref[...])


def _merge_call(xf, u, od, os_, of, an, pw, ps, wbr, wg, bg, wout, batch, seq):
    n, d = xf.shape
    tm = min(seq, ROW_TILE)
    ns = seq // tm
    row = lambda b, s: (b * ns + s, 0)
    halo_blocks = tm // POOL_HALO
    halo = lambda b, s: (jnp.maximum((b * ns + s) * halo_blocks - 1, 0), 0)
    return pl.pallas_call(
        _merge_kernel,
        grid=(batch, ns),
        in_specs=[pl.BlockSpec((tm, d), row), pl.BlockSpec((tm, POOL_WIDTH), row),
                  pl.BlockSpec((POOL_HALO, POOL_WIDTH), halo),
                  pl.BlockSpec((tm, DIFF_WIDTH), row), pl.BlockSpec((tm, SB_WIDTH), row),
                  pl.BlockSpec((tm, FOX_WIDTH), row),
                  _full(an.shape), _full(pw.shape), _full(ps.shape), _full(wbr.shape),
                  _full(wg.shape), _full(bg.shape), _full(wout.shape)],
        out_specs=pl.BlockSpec((tm, d), row),
        out_shape=jax.ShapeDtypeStruct((n, d), F32),
        compiler_params=_cparams(2),
        name="merge_out_proj",
    )(xf, u, u, od, os_, of, an, pw, ps, wbr, wg, bg, wout)


def _ffn_kernel(x_ref, fn_ref, wup_ref, wdn_ref, o_ref, *, chunk):
    x = x_ref[...]
    h = _rms_rows(x, fn_ref[...]).astype(BF16)
    d_ff = wdn_ref.shape[0]
    acc = x
    for c in range(d_ff // chunk):
        gate = _bdot(h, wup_ref[:, c * chunk:(c + 1) * chunk])
        up = _bdot(h, wup_ref[:, d_ff + c * chunk:d_ff + (c + 1) * chunk])
        act = (gate * jax.nn.sigmoid(gate) * up).astype(BF16)
        acc = acc + _bdot(act, wdn_ref[c * chunk:(c + 1) * chunk, :])
    o_ref[...] = acc


def _ffn_call(xf, fn, wup, wdn):
    n, d = xf.shape
    tm = min(n, ROW_TILE)
    return pl.pallas_call(
        functools.partial(_ffn_kernel, chunk=256),
        grid=(n // tm,),
        in_specs=[pl.BlockSpec((tm, d), lambda i: (i, 0)), _full(fn.shape), _full(wup.shape),
                  _full(wdn.shape)],
        out_specs=pl.BlockSpec((tm, d), lambda i: (i, 0)),
        out_shape=jax.ShapeDtypeStruct((n, d), F32),
        compiler_params=_cparams(1),
        name="swiglu_ffn",
    )(xf, fn, wup, wdn)


def _group_mean_matrix():
    g = np.kron(np.eye(256 // HEAD_DIM), np.full((HEAD_DIM, HEAD_DIM), 1.0 / HEAD_DIM))
    return jnp.asarray(g, BF16)


def _lower_tri(nrows):
    return jnp.asarray(np.tril(np.ones((nrows, nrows))), BF16)


def _decay_selector():
    sel = np.zeros((LANES, 2 * FOX_HEADS * LANES), np.float32)
    for hd in range(FOX_HEADS):
        qb = hd * LANES + AUG_LANE
        kb = (FOX_HEADS + hd) * LANES + AUG_LANE
        for term in range(3):
            sel[4 * term + hd, qb + term] = 1.0
            sel[12, qb + 3 + term] = 1.0
            sel[12, kb + term] = 1.0
            sel[4 * term + hd, kb + 3 + term] = -1.0
    return jnp.asarray(sel, BF16)


def kernel(x, attn_norm, ffn_norm, w_in, b_gate, b_forget, pool_w, pool_scale, diff_q_norm,
           diff_k_norm, diff_subln, lam_q1, lam_k1, lam_q2, lam_k2, fox_q_norm, fox_k_norm,
           w_branch, w_out, w_ffn_up, w_ffn_down):
    batch, seq, d = x.shape
    depth = attn_norm.shape[0]
    assert seq % min(seq, ROW_TILE) == 0 and seq % min(seq, ATTN_Q_BLOCK) == 0
    xf = x.reshape(batch * seq, d)
    cos, sin = _rope_tables(seq)
    gmat = _group_mean_matrix()
    ltri = _lower_tri(min(seq, 256))
    sel = _decay_selector()
    softmax_scratch = functools.partial(_softmax_scratch, v_width=LANES)

    for layer in range(depth):
        w = w_in[layer]
        wf = w[:, MAIN_WIDTH:MAIN_WIDTH + FOX_HEADS]
        wf_pad = jnp.concatenate([wf, wf, wf, jnp.zeros((d, LANES - 3 * FOX_HEADS), w.dtype)], axis=1)
        w_all = jnp.concatenate([w[:, :MAIN_WIDTH], wf_pad], axis=1).astype(BF16)
        bfg = b_forget[layer]
        bf_pad = jnp.concatenate([bfg, bfg, bfg, jnp.zeros((LANES - 3 * FOX_HEADS,), F32)])[None, :]
        gains = jnp.stack([jnp.tile(g[layer], 256 // HEAD_DIM)
                           for g in (diff_q_norm, diff_k_norm, fox_q_norm, fox_k_norm)])
        an = attn_norm[layer][None, :]

        u, dq, dk, dv, sq, sk, sv, fq, fk, fv = _proj_call(
            xf, an, w_all, bf_pad, cos, sin, gains, gmat, ltri, sel, batch, seq)

        lam = jnp.stack([lam_q1[layer], lam_k1[layer], lam_q2[layer], lam_k2[layer]])
        od = _attention_call(
            functools.partial(_diff_kernel, lambda_init=_diff_lambda_init(layer)), "diff_attention",
            dq, dk, dv, (lam, diff_subln[layer][None, :]), batch, seq, DIFF_HEADS, LANES,
            SOFTMAX_K_BLOCK, softmax_scratch)
        os_ = _attention_call(_sb_kernel, "stick_breaking_attention", sq, sk, sv, (), batch, seq,
                              2, LANES, SB_K_BLOCK, _sb_scratch)
        of = _attention_call(_fox_kernel, "forgetting_attention", fq, fk, fv, (), batch, seq,
                             2, 2 * LANES, SOFTMAX_K_BLOCK, softmax_scratch)

        pw = jax.scipy.linalg.block_diag(*[pool_w[layer, g] for g in range(len(POOL_WINDOWS))])
        x1 = _merge_call(xf, u, od, os_, of, an, pw.astype(BF16), pool_scale[layer][None, :],
                         w_branch[layer].astype(BF16), w[:, MAIN_WIDTH + FOX_HEADS:].astype(BF16),
                         b_gate[layer], w_out[layer].astype(BF16), batch, seq)
        xf = _ffn_call(x1, ffn_norm[layer][None, :], w_ffn_up[layer].astype(BF16),
                       w_ffn_down[layer].astype(BF16))
    return xf.reshape(batch, seq, d)
```

```python
import functools
import math

import numpy as np
import jax
import jax.numpy as jnp
from jax import lax
from jax.experimental import pallas as pl
from jax.experimental.pallas import tpu as pltpu

F32 = jnp.float32
BF16 = jnp.bfloat16

HEAD_DIM = 64
POOL_WINDOWS = (2, 4, 8, 16)
POOL_WIDTH = 256
DIFF_HEADS = 4
DIFF_QK_WIDTH = 512
DIFF_WIDTH = 512
SB_WIDTH = 256
FOX_HEADS = 4
FOX_WIDTH = 256
N_BRANCHES = 4
BRANCH_WIDTHS = (POOL_WIDTH, DIFF_WIDTH, SB_WIDTH, FOX_WIDTH)
MAIN_WIDTH = 3328
ROPE_THETA = 10000.0
NORM_EPS = 1e-6
QK_SCALE = HEAD_DIM ** -0.5
LOG2E = math.log2(math.e)
Q_SCALE = QK_SCALE * LOG2E

LANES = 128
ATTN_Q_BLOCK = 512
SOFTMAX_K_BLOCK = 512
SB_K_BLOCK = 512
SB_SUM_BLOCK = 256
ROW_TILE = 512
ROW_CHUNK = 128
POOL_HALO = 16
AUG_LANE = 64
VMEM_LIMIT = 56 * 1024 * 1024


def _diff_lambda_init(layer):
    return 0.8 - 0.6 * math.exp(-0.3 * layer)


def _cparams(n_axes):
    return pltpu.CompilerParams(dimension_semantics=("arbitrary",) * n_axes,
                                vmem_limit_bytes=VMEM_LIMIT)


def _full(shape):
    return pl.BlockSpec(shape, lambda *_: (0,) * len(shape))


def _rms_rows(x, gain):
    ms = jnp.mean(x * x, axis=-1, keepdims=True)
    return x * lax.rsqrt(ms + NORM_EPS) * gain


def _bdot(a, b):
    return jnp.dot(a, b, preferred_element_type=F32)


def _nt_dot(a, b):
    return lax.dot_general(a, b, (((1,), (1,)), ((), ())), preferred_element_type=F32)


def _split_head_pair(q):
    lane = lax.broadcasted_iota(jnp.int32, q.shape, 1)
    qf = q.astype(F32)
    return (jnp.where(lane < HEAD_DIM, qf, 0.0).astype(q.dtype),
            jnp.where(lane >= HEAD_DIM, qf, 0.0).astype(q.dtype))


def _split3(v):
    a1 = v.astype(BF16)
    r1 = v - a1.astype(F32)
    a2 = r1.astype(BF16)
    r2 = r1 - a2.astype(F32)
    a3 = r2.astype(BF16)
    return a1, a2, a3


def _rope_table_kernel(inv_ref, cos_ref, sin_ref):
    rows = cos_ref.shape[0]
    base = pl.program_id(0) * rows
    pos = (base + lax.broadcasted_iota(jnp.int32, (rows, LANES), 0)).astype(F32)
    ang = pos * inv_ref[...]
    lane = lax.broadcasted_iota(jnp.int32, (rows, LANES), 1)
    first_half = (lane & (HEAD_DIM // 2)) == 0
    s = jnp.sin(ang)
    cos_ref[...] = jnp.cos(ang)
    sin_ref[...] = jnp.where(first_half, -s, s)


def _rope_tables(seq):
    half = HEAD_DIM // 2
    inv_freq = ROPE_THETA ** (-jnp.arange(half, dtype=F32) / half)
    inv = jnp.tile(inv_freq, LANES // half)[None, :]
    rows = min(seq, ROW_TILE)
    return pl.pallas_call(
        _rope_table_kernel,
        grid=(seq // rows,),
        in_specs=[_full((1, LANES))],
        out_specs=[pl.BlockSpec((rows, LANES), lambda i: (i, 0))] * 2,
        out_shape=[jax.ShapeDtypeStruct((seq, LANES), F32)] * 2,
        compiler_params=_cparams(1),
        name="rope_tables",
    )(inv)


def _proj_kernel(x_ref, an_ref, w_ref, bf_ref, cos_ref, sin_ref, gains_ref, gmat_ref,
                 ltri_ref, sel_ref,
                 u_ref, dq_ref, dk_ref, dv_ref, sq_ref, sk_ref, sv_ref, fq_ref, fk_ref, fv_ref,
                 carry_ref):
    tm = x_ref.shape[0]
    h = _rms_rows(x_ref[...], an_ref[...]).astype(BF16)

    def proj(lo, width):
        return _nt_dot(h, w_ref[lo:lo + width, :])

    def group_norm(y, gain):
        ms = _bdot((y * y).astype(BF16), gmat_ref[...])
        return y * lax.rsqrt(ms + NORM_EPS) * gain

    cos = cos_ref[...]
    sin = sin_ref[...]
    lane = lax.broadcasted_iota(jnp.int32, (tm, LANES), 1)
    first_half = (lane & (HEAD_DIM // 2)) == 0
    low_head = lane < HEAD_DIM

    def rope(y):
        partner = jnp.where(first_half, pltpu.roll(y, LANES - HEAD_DIM // 2, 1),
                            pltpu.roll(y, HEAD_DIM // 2, 1))
        return y * cos + partner * sin

    u_ref[...] = proj(0, 256)

    gq = gains_ref[0:1, :]
    gk = gains_ref[1:2, :]
    for c in range(2):
        yq = group_norm(proj(256 + 256 * c, 256), gq)
        yk = group_norm(proj(768 + 256 * c, 256), gk)
        for hh in range(2):
            sl = slice(hh * LANES, (hh + 1) * LANES)
            col = slice(256 * c + hh * LANES, 256 * c + (hh + 1) * LANES)
            dq_ref[:, col] = (rope(yq[:, sl]) * Q_SCALE).astype(BF16)
            dk_ref[:, col] = rope(yk[:, sl]).astype(BF16)
        dv_ref[:, 256 * c:256 * (c + 1)] = proj(1280 + 256 * c, 256).astype(BF16)

    sq_ref[...] = (proj(1792, 256) * Q_SCALE).astype(BF16)
    sk_ref[...] = proj(2048, 256).astype(BF16)
    sv_ref[...] = proj(2304, 256).astype(BF16)
    fv_ref[...] = proj(3072, 256).astype(BF16)

    fl = proj(MAIN_WIDTH, LANES) + bf_ref[...]
    logf = jnp.minimum(fl, 0.0) - jnp.log1p(jnp.exp(-jnp.abs(fl)))

    @pl.when(pl.program_id(1) == 0)
    def _():
        carry_ref[...] = jnp.zeros_like(carry_ref)

    carry = carry_ref[0:1, :]
    sub = ltri_ref.shape[0]
    pieces = []
    for sb in range(tm // sub):
        a1, a2, a3 = _split3(logf[sb * sub:(sb + 1) * sub, :])
        ltri = ltri_ref[...]
        cum = _bdot(ltri, a1) + _bdot(ltri, a2) + _bdot(ltri, a3) + carry
        carry = cum[sub - 1:sub, :]
        pieces.append(cum)
    carry_ref[0:1, :] = carry
    cum = jnp.concatenate(pieces, axis=0) if len(pieces) > 1 else pieces[0]

    c1, c2, c3 = (c.astype(F32) for c in _split3(cum * LOG2E))
    packed = jnp.where(lane < 4, c1, jnp.where(lane < 8, c2, jnp.where(lane < 12, c3,
                       jnp.where(lane == 12, 1.0, 0.0))))
    aug = _bdot(packed.astype(BF16), sel_ref[...])

    yq = group_norm(proj(2560, 256), gains_ref[2:3, :]) * Q_SCALE
    yk = group_norm(proj(2816, 256), gains_ref[3:4, :])
    for src, dst, off in ((yq, fq_ref, 0), (yk, fk_ref, 4 * LANES)):
        for pair in range(2):
            v = src[:, pair * LANES:(pair + 1) * LANES]
            heads = (v, pltpu.roll(v, HEAD_DIM, 1))
            for hh in range(2):
                col = (2 * pair + hh) * LANES
                dst[:, col:col + LANES] = jnp.where(
                    low_head, heads[hh], aug[:, off + col:off + col + LANES]).astype(BF16)


def _proj_call(xf, an, w_all, bf_pad, cos, sin, gains, gmat, ltri, sel, batch, seq):
    n, d = xf.shape
    tm = min(seq, ROW_TILE)
    ns = seq // tm
    row = lambda b, s: (b * ns + s, 0)
    widths = (256, 512, 512, 512, 256, 256, 256, 512, 512, 256)
    dtypes = (F32,) + (BF16,) * 9
    return pl.pallas_call(
        _proj_kernel,
        grid=(batch, ns),
        in_specs=[pl.BlockSpec((tm, d), row), _full(an.shape), _full(w_all.shape), _full(bf_pad.shape),
                  pl.BlockSpec((tm, LANES), lambda b, s: (s, 0)),
                  pl.BlockSpec((tm, LANES), lambda b, s: (s, 0)),
                  _full(gains.shape), _full(gmat.shape), _full(ltri.shape), _full(sel.shape)],
        out_specs=[pl.BlockSpec((tm, w), row) for w in widths],
        out_shape=[jax.ShapeDtypeStruct((n, w), t) for w, t in zip(widths, dtypes)],
        scratch_shapes=[pltpu.VMEM((8, LANES), F32)],
        compiler_params=_cparams(2),
        name="norm_in_proj",
    )(xf, an, w_all, bf_pad, cos, sin, gains, gmat, ltri, sel)


def _key_minus_query(row0, rows, tq, tk):
    r = (row0 + lax.broadcasted_iota(jnp.int32, (rows, tk), 0)) & (tq - 1)
    c = lax.broadcasted_iota(jnp.int32, (rows, tk), 1)
    return c - r


def _softmax_sweep(streams, k_ref, v_ref, scratch, i, tq, tk):
    s_scr, p_scr, m_scr, l_scr, acc_scr = scratch
    total = tq * len(streams)
    n_lane_chunks = tk // LANES
    m_scr[...] = jnp.full(m_scr.shape, -1e30, F32)
    l_scr[...] = jnp.zeros(l_scr.shape, F32)
    acc_scr[...] = jnp.zeros(acc_scr.shape, F32)

    def step(j, limit):
        start = pl.multiple_of(j * tk, tk)
        for n, (q, cols) in enumerate(streams):
            s = _nt_dot(q, k_ref[pl.ds(start, tk), cols])
            if limit is not None:
                s = jnp.where(_key_minus_query(0, tq, tq, tk) <= limit, s, -jnp.inf)
            s_scr[n * tq:(n + 1) * tq, :] = s
        vb = v_ref[pl.ds(start, tk), :]
        for n in range(len(streams)):
            for rc in range(n * tq // ROW_CHUNK, (n + 1) * tq // ROW_CHUNK):
                rows = slice(rc * ROW_CHUNK, (rc + 1) * ROW_CHUNK)
                mx = s_scr[rows, 0:LANES]
                for c in range(1, n_lane_chunks):
                    mx = jnp.maximum(mx, s_scr[rows, c * LANES:(c + 1) * LANES])
                m_old = m_scr[rows, :]
                m_new = jnp.maximum(m_old, mx.max(axis=1, keepdims=True))
                alpha = jnp.exp2(m_old - m_new)
                m_scr[rows, :] = m_new
                psum = None
                for c in range(n_lane_chunks):
                    p = jnp.exp2(s_scr[rows, c * LANES:(c + 1) * LANES] - m_new)
                    psum = p if psum is None else psum + p
                    p_scr[rows, c * LANES:(c + 1) * LANES] = p.astype(BF16)
                l_scr[rows, :] = alpha * l_scr[rows, :] + psum
                acc_scr[rows, :] = alpha * acc_scr[rows, :]
            acc_scr[n * tq:(n + 1) * tq, :] += _bdot(p_scr[n * tq:(n + 1) * tq, :], vb)

    ratio = tq // tk

    def body(j, _):
        step(j, None)
        return 0

    lax.fori_loop(0, ratio * i, body, 0)
    for d in range(ratio):
        step(ratio * i + d, -d * tk)
    return acc_scr[...], l_scr[...].sum(axis=1, keepdims=True)


def _softmax_scratch(n_rows, tk, v_width):
    return [pltpu.VMEM((n_rows, tk), F32), pltpu.VMEM((n_rows, tk), BF16),
            pltpu.VMEM((n_rows, LANES), F32), pltpu.VMEM((n_rows, LANES), F32),
            pltpu.VMEM((n_rows, v_width), F32)]


def _diff_kernel(q_ref, k_ref, v_ref, lam_ref, sub_ref, o_ref, *scratch, lambda_init, tk):
    tq = q_ref.shape[0]
    i = pl.program_id(2)
    q_lo, q_hi = _split_head_pair(q_ref[...])
    acc, l = _softmax_sweep([(q_lo, slice(None)), (q_hi, slice(None))], k_ref, v_ref, scratch,
                            i, tq, tk)
    o = acc / l
    lp = lam_ref[...]
    lam = (jnp.exp(jnp.sum(lp[0:1, :] * lp[1:2, :], axis=1, keepdims=True))
           - jnp.exp(jnp.sum(lp[2:3, :] * lp[3:4, :], axis=1, keepdims=True)) + lambda_init)
    od = o[:tq, :] - lam * o[tq:, :]
    o_ref[...] = (_rms_rows(od, sub_ref[...]) * (1.0 - lambda_init)).astype(o_ref.dtype)


def _fox_kernel(q_ref, k_ref, v_ref, o_ref, *scratch, tk):
    tq = q_ref.shape[0]
    i = pl.program_id(2)
    streams = [(q_ref[:, hh * LANES:(hh + 1) * LANES], slice(hh * LANES, (hh + 1) * LANES))
               for hh in range(2)]
    acc, l = _softmax_sweep(streams, k_ref, v_ref, scratch, i, tq, tk)
    o = acc / l
    lane = lax.broadcasted_iota(jnp.int32, (tq, LANES), 1)
    o_ref[...] = jnp.where(lane < HEAD_DIM, o[:tq, :], o[tq:, :]).astype(o_ref.dtype)


def _sb_kernel(q_ref, k_ref, v_ref, o_ref, z_scr, hl_scr, c_scr, w_scr, later_scr, acc_scr, *, tk):
    tq = q_ref.shape[0]
    i = pl.program_id(2)
    ratio = tq // tk
    heads = _split_head_pair(q_ref[...])
    sub = min(SB_SUM_BLOCK, tk)
    n_sub = tk // sub
    lanes_per_sub = sub // LANES
    neg_suffix = jnp.where(lax.broadcasted_iota(jnp.int32, (sub, sub), 0)
                           >= lax.broadcasted_iota(jnp.int32, (sub, sub), 1), -1.0, 0.0).astype(BF16)
    neg_suffix2 = jnp.concatenate([neg_suffix, neg_suffix], axis=0)
    later_scr[...] = jnp.zeros(later_scr.shape, F32)
    acc_scr[...] = jnp.zeros(acc_scr.shape, F32)

    def step(j, limit):
        start = pl.multiple_of(j * tk, tk)
        kb = k_ref[pl.ds(start, tk), :]
        vb = v_ref[pl.ds(start, tk), :]
        for n, qh in enumerate(heads):
            z_scr[n * tq:(n + 1) * tq, :] = _nt_dot(qh, kb)
        for n in range(len(heads)):
            head_rows = slice(n * tq, (n + 1) * tq)
            chunks = range(n * tq // ROW_CHUNK, (n + 1) * tq // ROW_CHUNK)
            for rc in chunks:
                rows = slice(rc * ROW_CHUNK, (rc + 1) * ROW_CHUNK)
                z = z_scr[rows, :]
                neg_abs = lax.bitcast_convert_type(
                    lax.bitcast_convert_type(z, jnp.int32) | jnp.int32(-2 ** 31), F32)
                sp = jnp.maximum(z, 0.0) + jnp.log2(1.0 + jnp.exp2(neg_abs))
                if limit is not None:
                    sp = jnp.where(_key_minus_query(rc * ROW_CHUNK, ROW_CHUNK, tq, tk) < limit, sp, 0.0)
                hi = sp.astype(BF16)
                lo = (sp - hi.astype(F32)).astype(BF16)
                for b in range(n_sub):
                    hl_scr[rows, 2 * b * sub:(2 * b + 1) * sub] = hi[:, b * sub:(b + 1) * sub]
                    hl_scr[rows, (2 * b + 1) * sub:(2 * b + 2) * sub] = lo[:, b * sub:(b + 1) * sub]
            for b in range(n_sub):
                c_scr[head_rows, b * sub:(b + 1) * sub] = _bdot(
                    hl_scr[head_rows, 2 * b * sub:(2 * b + 2) * sub], neg_suffix2)
            for rc in chunks:
                rows = slice(rc * ROW_CHUNK, (rc + 1) * ROW_CHUNK)
                later = later_scr[rows, :]
                for b in reversed(range(n_sub)):
                    for c in range(b * lanes_per_sub, (b + 1) * lanes_per_sub):
                        cols = slice(c * LANES, (c + 1) * LANES)
                        w = jnp.exp2(z_scr[rows, cols] + c_scr[rows, cols] + later)
                        if limit is not None:
                            keep = _key_minus_query(rc * ROW_CHUNK, ROW_CHUNK, tq, tk)[:, cols] < limit
                            w = jnp.where(keep, w, 0.0)
                        w_scr[rows, cols] = w.astype(BF16)
                    later = later + c_scr[rows, b * sub:b * sub + 1]
                later_scr[rows, :] = later
            acc_scr[head_rows, :] += _bdot(w_scr[head_rows, :], vb)

    for d in reversed(range(ratio)):
        step(ratio * i + d, -d * tk)

    def body(n, _):
        step(ratio * i - 1 - n, None)
        return 0

    lax.fori_loop(0, ratio * i, body, 0)
    lane = lax.broadcasted_iota(jnp.int32, (tq, LANES), 1)
    o_ref[...] = jnp.where(lane < HEAD_DIM, acc_scr[:tq, :], acc_scr[tq:, :]).astype(o_ref.dtype)


def _sb_scratch(n_rows, tk):
    return [pltpu.VMEM((n_rows, tk), F32), pltpu.VMEM((n_rows, 2 * tk), BF16),
            pltpu.VMEM((n_rows, tk), F32), pltpu.VMEM((n_rows, tk), BF16),
            pltpu.VMEM((n_rows, LANES), F32), pltpu.VMEM((n_rows, LANES), F32)]


def _attention_call(body, name, q, k, v, extra, batch, seq, n_groups, q_width, tk, scratch_fn):
    n = q.shape[0]
    tq = min(seq, ATTN_Q_BLOCK)
    tk = min(tk, tq)
    nq = seq // tq
    return pl.pallas_call(
        functools.partial(body, tk=tk),
        grid=(batch, n_groups, nq),
        in_specs=[pl.BlockSpec((tq, q_width), lambda b, g, i: (b * nq + i, g)),
                  pl.BlockSpec((seq, q_width), lambda b, g, i: (b, g)),
                  pl.BlockSpec((seq, LANES), lambda b, g, i: (b, g))] + [_full(e.shape) for e in extra],
        out_specs=pl.BlockSpec((tq, LANES), lambda b, g, i: (b * nq + i, g)),
        out_shape=jax.ShapeDtypeStruct((n, n_groups * LANES), BF16),
        scratch_shapes=scratch_fn(2 * tq, tk),
        compiler_params=_cparams(3),
        name=name,
    )(q, k, v, *extra)


def _merge_kernel(x_ref, u_ref, halo_ref, od_ref, os_ref, of_ref, an_ref, pw_ref, ps_ref,
                  wbr_ref, wg_ref, bg_ref, wout_ref, o_ref):
    tm, d = x_ref.shape
    x = x_ref[...]
    h = _rms_rows(x, an_ref[...]).astype(BF16)

    u = u_ref[...]
    first_tile = pl.program_id(1) == 0
    halo = jnp.where(first_tile, 0.0, halo_ref[...])
    ext = jnp.concatenate([halo, u], axis=0)
    lane = lax.broadcasted_iota(jnp.int32, (tm, POOL_WIDTH), 1)
    group = lane >> 6
    win = jnp.zeros_like(u)
    width = jnp.zeros((tm, POOL_WIDTH), jnp.int32)
    span = 1
    for g, w in enumerate(POOL_WINDOWS):
        while span < w:
            ext = ext + pltpu.roll(ext, span, 0)
            span *= 2
        win = jnp.where(group == g, ext[POOL_HALO:, :], win)
        width = jnp.where(group == g, w, width)
    pos = pl.program_id(1) * tm + lax.broadcasted_iota(jnp.int32, (tm, POOL_WIDTH), 0)
    count = jnp.minimum(pos + 1, width).astype(F32)
    pooled = win / count - u
    o_pool = (_bdot(pooled.astype(BF16), pw_ref[...]) * ps_ref[...]).astype(BF16)

    merged = jnp.zeros((tm, d), F32)
    lo = 0
    for nb, o in enumerate((o_pool, od_ref[...], os_ref[...], of_ref[...])):
        width_n = BRANCH_WIDTHS[nb]
        y = _bdot(o, wbr_ref[lo:lo + width_n, :])
        gate = jax.nn.sigmoid(_nt_dot(h, wg_ref[nb * d:(nb + 1) * d, :]) + bg_ref[nb:nb + 1, :])
        merged = merged + gate * y
        lo += width_n
    o_ref[...] = x + _bdot(merged.astype(BF16), wout_ref[...])


def _merge_call(xf, u, od, os_, of, an, pw, ps, wbr, wg, bg, wout, batch, seq):
    n, d = xf.shape
    tm = min(seq, ROW_TILE)
    ns = seq // tm
    row = lambda b, s: (b * ns + s, 0)
    halo_blocks = tm // POOL_HALO
    halo = lambda b, s: (jnp.maximum((b * ns + s) * halo_blocks - 1, 0), 0)
    return pl.pallas_call(
        _merge_kernel,
        grid=(batch, ns),
        in_specs=[pl.BlockSpec((tm, d), row), pl.BlockSpec((tm, POOL_WIDTH), row),
                  pl.BlockSpec((POOL_HALO, POOL_WIDTH), halo),
                  pl.BlockSpec((tm, DIFF_WIDTH), row), pl.BlockSpec((tm, SB_WIDTH), row),
                  pl.BlockSpec((tm, FOX_WIDTH), row),
                  _full(an.shape), _full(pw.shape), _full(ps.shape), _full(wbr.shape),
                  _full(wg.shape), _full(bg.shape), _full(wout.shape)],
        out_specs=pl.BlockSpec((tm, d), row),
        out_shape=jax.ShapeDtypeStruct((n, d), F32),
        compiler_params=_cparams(2),
        name="merge_out_proj",
    )(xf, u, u, od, os_, of, an, pw, ps, wbr, wg, bg, wout)


def _ffn_kernel(x_ref, fn_ref, wup_ref, wdn_ref, o_ref, *, chunk):
    x = x_ref[...]
    h = _rms_rows(x, fn_ref[...]).astype(BF16)
    d_ff = wdn_ref.shape[0]
    acc = x
    for c in range(d_ff // chunk):
        gate = _bdot(h, wup_ref[:, c * chunk:(c + 1) * chunk])
        up = _bdot(h, wup_ref[:, d_ff + c * chunk:d_ff + (c + 1) * chunk])
        act = (gate * jax.nn.sigmoid(gate) * up).astype(BF16)
        acc = acc + _bdot(act, wdn_ref[c * chunk:(c + 1) * chunk, :])
    o_ref[...] = acc


def _ffn_call(xf, fn, wup, wdn):
    n, d = xf.shape
    tm = min(n, ROW_TILE)
    return pl.pallas_call(
        functools.partial(_ffn_kernel, chunk=256),
        grid=(n // tm,),
        in_specs=[pl.BlockSpec((tm, d), lambda i: (i, 0)), _full(fn.shape), _full(wup.shape),
                  _full(wdn.shape)],
        out_specs=pl.BlockSpec((tm, d), lambda i: (i, 0)),
        out_shape=jax.ShapeDtypeStruct((n, d), F32),
        compiler_params=_cparams(1),
        name="swiglu_ffn",
    )(xf, fn, wup, wdn)


def _group_mean_matrix():
    g = np.kron(np.eye(256 // HEAD_DIM), np.full((HEAD_DIM, HEAD_DIM), 1.0 / HEAD_DIM))
    return jnp.asarray(g, BF16)


def _lower_tri(nrows):
    return jnp.asarray(np.tril(np.ones((nrows, nrows))), BF16)


def _decay_selector():
    sel = np.zeros((LANES, 2 * FOX_HEADS * LANES), np.float32)
    for hd in range(FOX_HEADS):
        qb = hd * LANES + AUG_LANE
        kb = (FOX_HEADS + hd) * LANES + AUG_LANE
        for term in range(3):
            sel[4 * term + hd, qb + term] = 1.0
            sel[12, qb + 3 + term] = 1.0
            sel[12, kb + term] = 1.0
            sel[4 * term + hd, kb + 3 + term] = -1.0
    return jnp.asarray(sel, BF16)


def kernel(x, attn_norm, ffn_norm, w_in, b_gate, b_forget, pool_w, pool_scale, diff_q_norm,
           diff_k_norm, diff_subln, lam_q1, lam_k1, lam_q2, lam_k2, fox_q_norm, fox_k_norm,
           w_branch, w_out, w_ffn_up, w_ffn_down):
    batch, seq, d = x.shape
    depth = attn_norm.shape[0]
    assert seq % min(seq, ROW_TILE) == 0 and seq % min(seq, ATTN_Q_BLOCK) == 0
    xf = x.reshape(batch * seq, d)
    cos, sin = _rope_tables(seq)
    gmat = _group_mean_matrix()
    ltri = _lower_tri(min(seq, 256))
    sel = _decay_selector()
    softmax_scratch = functools.partial(_softmax_scratch, v_width=LANES)

    for layer in range(depth):
        wt = jnp.swapaxes(w_in[layer], 0, 1)
        wf = wt[MAIN_WIDTH:MAIN_WIDTH + FOX_HEADS]
        wf_pad = jnp.concatenate([wf, wf, wf, jnp.zeros((LANES - 3 * FOX_HEADS, d), wt.dtype)], axis=0)
        w_all = jnp.concatenate([wt[:MAIN_WIDTH], wf_pad], axis=0).astype(BF16)
        bfg = b_forget[layer]
        bf_pad = jnp.concatenate([bfg, bfg, bfg, jnp.zeros((LANES - 3 * FOX_HEADS,), F32)])[None, :]
        gains = jnp.stack([jnp.tile(g[layer], 256 // HEAD_DIM)
                           for g in (diff_q_norm, diff_k_norm, fox_q_norm, fox_k_norm)])
        an = attn_norm[layer][None, :]

        u, dq, dk, dv, sq, sk, sv, fq, fk, fv = _proj_call(
            xf, an, w_all, bf_pad, cos, sin, gains, gmat, ltri, sel, batch, seq)

        lam = jnp.stack([lam_q1[layer], lam_k1[layer], lam_q2[layer], lam_k2[layer]])
        od = _attention_call(
            functools.partial(_diff_kernel, lambda_init=_diff_lambda_init(layer)), "diff_attention",
            dq, dk, dv, (lam, diff_subln[layer][None, :]), batch, seq, DIFF_HEADS, LANES,
            SOFTMAX_K_BLOCK, softmax_scratch)
        os_ = _attention_call(_sb_kernel, "stick_breaking_attention", sq, sk, sv, (), batch, seq,
                              2, LANES, SB_K_BLOCK, _sb_scratch)
        of = _attention_call(_fox_kernel, "forgetting_attention", fq, fk, fv, (), batch, seq,
                             2, 2 * LANES, SOFTMAX_K_BLOCK, softmax_scratch)

        pw = jax.scipy.linalg.block_diag(*[pool_w[layer, g] for g in range(len(POOL_WINDOWS))])
        x1 = _merge_call(xf, u, od, os_, of, an, pw.astype(BF16), pool_scale[layer][None, :],
                         w_branch[layer].astype(BF16), wt[MAIN_WIDTH + FOX_HEADS:].astype(BF16),
                         b_gate[layer], w_out[layer].astype(BF16), batch, seq)
        xf = _ffn_call(x1, ffn_norm[layer][None, :], w_ffn_up[layer].astype(BF16),
                       w_ffn_down[layer].astype(BF16))
    return xf.reshape(batch, seq, d)
```

```python
import functools
import math

import numpy as np
import jax
import jax.numpy as jnp
from jax import lax
from jax.experimental import pallas as pl
from jax.experimental.pallas import tpu as pltpu

F32 = jnp.float32
BF16 = jnp.bfloat16

HEAD_DIM = 64
POOL_WINDOWS = (2, 4, 8, 16)
POOL_WIDTH = 256
DIFF_HEADS = 4
DIFF_QK_WIDTH = 512
DIFF_WIDTH = 512
SB_WIDTH = 256
FOX_HEADS = 4
FOX_WIDTH = 256
N_BRANCHES = 4
BRANCH_WIDTHS = (POOL_WIDTH, DIFF_WIDTH, SB_WIDTH, FOX_WIDTH)
MAIN_WIDTH = 3328
ROPE_THETA = 10000.0
NORM_EPS = 1e-6
QK_SCALE = HEAD_DIM ** -0.5
LOG2E = math.log2(math.e)
Q_SCALE = QK_SCALE * LOG2E

LANES = 128
ATTN_Q_BLOCK = 512
SOFTMAX_K_BLOCK = 512
SB_K_BLOCK = 512
SB_SUM_BLOCK = 256
DIFF_GROUPS = 2
ROW_TILE = 512
ROW_CHUNK = 128
POOL_HALO = 16
AUG_LANE = 64
VMEM_LIMIT = 56 * 1024 * 1024


def _diff_lambda_init(layer):
    return 0.8 - 0.6 * math.exp(-0.3 * layer)


def _cparams(n_axes):
    return pltpu.CompilerParams(dimension_semantics=("arbitrary",) * n_axes,
                                vmem_limit_bytes=VMEM_LIMIT)


def _full(shape):
    return pl.BlockSpec(shape, lambda *_: (0,) * len(shape))


def _rms_rows(x, gain):
    ms = jnp.mean(x * x, axis=-1, keepdims=True)
    return x * lax.rsqrt(ms + NORM_EPS) * gain


def _bdot(a, b):
    return jnp.dot(a, b, preferred_element_type=F32)


def _nt_dot(a, b):
    return lax.dot_general(a, b, (((1,), (1,)), ((), ())), preferred_element_type=F32)


def _split_head_pair(q):
    lane = lax.broadcasted_iota(jnp.int32, q.shape, 1)
    qf = q.astype(F32)
    return (jnp.where(lane < HEAD_DIM, qf, 0.0).astype(q.dtype),
            jnp.where(lane >= HEAD_DIM, qf, 0.0).astype(q.dtype))


def _split3(v):
    a1 = v.astype(BF16)
    r1 = v - a1.astype(F32)
    a2 = r1.astype(BF16)
    r2 = r1 - a2.astype(F32)
    a3 = r2.astype(BF16)
    return a1, a2, a3


def _rope_table_kernel(inv_ref, cos_ref, sin_ref):
    rows = cos_ref.shape[0]
    base = pl.program_id(0) * rows
    pos = (base + lax.broadcasted_iota(jnp.int32, (rows, LANES), 0)).astype(F32)
    ang = pos * inv_ref[...]
    lane = lax.broadcasted_iota(jnp.int32, (rows, LANES), 1)
    first_half = (lane & (HEAD_DIM // 2)) == 0
    s = jnp.sin(ang)
    cos_ref[...] = jnp.cos(ang)
    sin_ref[...] = jnp.where(first_half, -s, s)


def _rope_tables(seq):
    half = HEAD_DIM // 2
    inv_freq = ROPE_THETA ** (-jnp.arange(half, dtype=F32) / half)
    inv = jnp.tile(inv_freq, LANES // half)[None, :]
    rows = min(seq, ROW_TILE)
    return pl.pallas_call(
        _rope_table_kernel,
        grid=(seq // rows,),
        in_specs=[_full((1, LANES))],
        out_specs=[pl.BlockSpec((rows, LANES), lambda i: (i, 0))] * 2,
        out_shape=[jax.ShapeDtypeStruct((seq, LANES), F32)] * 2,
        compiler_params=_cparams(1),
        name="rope_tables",
    )(inv)


def _proj_kernel(x_ref, an_ref, w_ref, bf_ref, cos_ref, sin_ref, gains_ref, gmat_ref,
                 ltri_ref, sel_ref,
                 u_ref, dq_ref, dk_ref, dv_ref, sq_ref, sk_ref, sv_ref, fq_ref, fk_ref, fv_ref,
                 carry_ref):
    tm = x_ref.shape[0]
    h = _rms_rows(x_ref[...], an_ref[...]).astype(BF16)

    def proj(lo, width):
        return _nt_dot(h, w_ref[lo:lo + width, :])

    def group_norm(y, gain):
        ms = _bdot((y * y).astype(BF16), gmat_ref[...])
        return y * lax.rsqrt(ms + NORM_EPS) * gain

    cos = cos_ref[...]
    sin = sin_ref[...]
    lane = lax.broadcasted_iota(jnp.int32, (tm, LANES), 1)
    first_half = (lane & (HEAD_DIM // 2)) == 0
    low_head = lane < HEAD_DIM

    def rope(y):
        partner = jnp.where(first_half, pltpu.roll(y, LANES - HEAD_DIM // 2, 1),
                            pltpu.roll(y, HEAD_DIM // 2, 1))
        return y * cos + partner * sin

    u_ref[...] = proj(0, 256)

    gq = gains_ref[0:1, :]
    gk = gains_ref[1:2, :]
    for c in range(2):
        yq = group_norm(proj(256 + 256 * c, 256), gq)
        yk = group_norm(proj(768 + 256 * c, 256), gk)
        for hh in range(2):
            sl = slice(hh * LANES, (hh + 1) * LANES)
            col = slice(256 * c + hh * LANES, 256 * c + (hh + 1) * LANES)
            dq_ref[:, col] = (rope(yq[:, sl]) * Q_SCALE).astype(BF16)
            dk_ref[:, col] = rope(yk[:, sl]).astype(BF16)
        dv_ref[:, 256 * c:256 * (c + 1)] = proj(1280 + 256 * c, 256).astype(BF16)

    sq_ref[...] = (proj(1792, 256) * Q_SCALE).astype(BF16)
    sk_ref[...] = proj(2048, 256).astype(BF16)
    sv_ref[...] = proj(2304, 256).astype(BF16)
    fv_ref[...] = proj(3072, 256).astype(BF16)

    fl = proj(MAIN_WIDTH, LANES) + bf_ref[...]
    logf = jnp.minimum(fl, 0.0) - jnp.log1p(jnp.exp(-jnp.abs(fl)))

    @pl.when(pl.program_id(1) == 0)
    def _():
        carry_ref[...] = jnp.zeros_like(carry_ref)

    carry = carry_ref[0:1, :]
    sub = ltri_ref.shape[0]
    pieces = []
    for sb in range(tm // sub):
        a1, a2, a3 = _split3(logf[sb * sub:(sb + 1) * sub, :])
        ltri = ltri_ref[...]
        cum = _bdot(ltri, a1) + _bdot(ltri, a2) + _bdot(ltri, a3) + carry
        carry = cum[sub - 1:sub, :]
        pieces.append(cum)
    carry_ref[0:1, :] = carry
    cum = jnp.concatenate(pieces, axis=0) if len(pieces) > 1 else pieces[0]

    c1, c2, c3 = (c.astype(F32) for c in _split3(cum * LOG2E))
    packed = jnp.where(lane < 4, c1, jnp.where(lane < 8, c2, jnp.where(lane < 12, c3,
                       jnp.where(lane == 12, 1.0, 0.0))))
    aug = _bdot(packed.astype(BF16), sel_ref[...])

    yq = group_norm(proj(2560, 256), gains_ref[2:3, :]) * Q_SCALE
    yk = group_norm(proj(2816, 256), gains_ref[3:4, :])
    for src, dst, off in ((yq, fq_ref, 0), (yk, fk_ref, 4 * LANES)):
        for pair in range(2):
            v = src[:, pair * LANES:(pair + 1) * LANES]
            heads = (v, pltpu.roll(v, HEAD_DIM, 1))
            for hh in range(2):
                col = (2 * pair + hh) * LANES
                dst[:, col:col + LANES] = jnp.where(
                    low_head, heads[hh], aug[:, off + col:off + col + LANES]).astype(BF16)


def _proj_call(xf, an, w_all, bf_pad, cos, sin, gains, gmat, ltri, sel, batch, seq):
    n, d = xf.shape
    tm = min(seq, ROW_TILE)
    ns = seq // tm
    row = lambda b, s: (b * ns + s, 0)
    widths = (256, 512, 512, 512, 256, 256, 256, 512, 512, 256)
    dtypes = (F32,) + (BF16,) * 9
    return pl.pallas_call(
        _proj_kernel,
        grid=(batch, ns),
        in_specs=[pl.BlockSpec((tm, d), row), _full(an.shape), _full(w_all.shape), _full(bf_pad.shape),
                  pl.BlockSpec((tm, LANES), lambda b, s: (s, 0)),
                  pl.BlockSpec((tm, LANES), lambda b, s: (s, 0)),
                  _full(gains.shape), _full(gmat.shape), _full(ltri.shape), _full(sel.shape)],
        out_specs=[pl.BlockSpec((tm, w), row) for w in widths],
        out_shape=[jax.ShapeDtypeStruct((n, w), t) for w, t in zip(widths, dtypes)],
        scratch_shapes=[pltpu.VMEM((8, LANES), F32)],
        compiler_params=_cparams(2),
        name="norm_in_proj",
    )(xf, an, w_all, bf_pad, cos, sin, gains, gmat, ltri, sel)


def _key_minus_query(row0, rows, tq, tk):
    r = (row0 + lax.broadcasted_iota(jnp.int32, (rows, tk), 0)) & (tq - 1)
    c = lax.broadcasted_iota(jnp.int32, (rows, tk), 1)
    return c - r


def _softmax_sweep(streams, k_ref, v_ref, scratch, i, tq, tk):
    s_scr, p_scr, m_scr, l_scr, acc_scr = scratch
    total = tq * len(streams)
    n_lane_chunks = tk // LANES
    m_scr[...] = jnp.full(m_scr.shape, -1e30, F32)
    l_scr[...] = jnp.zeros(l_scr.shape, F32)
    acc_scr[...] = jnp.zeros(acc_scr.shape, F32)

    def step(j, limit):
        start = pl.multiple_of(j * tk, tk)
        for n, (q, cols, _) in enumerate(streams):
            s = _nt_dot(q, k_ref[pl.ds(start, tk), cols])
            if limit is not None:
                s = jnp.where(_key_minus_query(0, tq, tq, tk) <= limit, s, -jnp.inf)
            s_scr[n * tq:(n + 1) * tq, :] = s
        for n, (_, _, vcols) in enumerate(streams):
            for rc in range(n * tq // ROW_CHUNK, (n + 1) * tq // ROW_CHUNK):
                rows = slice(rc * ROW_CHUNK, (rc + 1) * ROW_CHUNK)
                mx = s_scr[rows, 0:LANES]
                for c in range(1, n_lane_chunks):
                    mx = jnp.maximum(mx, s_scr[rows, c * LANES:(c + 1) * LANES])
                m_old = m_scr[rows, :]
                m_new = jnp.maximum(m_old, mx.max(axis=1, keepdims=True))
                alpha = jnp.exp2(m_old - m_new)
                m_scr[rows, :] = m_new
                psum = None
                for c in range(n_lane_chunks):
                    p = jnp.exp2(s_scr[rows, c * LANES:(c + 1) * LANES] - m_new)
                    psum = p if psum is None else psum + p
                    p_scr[rows, c * LANES:(c + 1) * LANES] = p.astype(BF16)
                l_scr[rows, :] = alpha * l_scr[rows, :] + psum
                acc_scr[rows, :] = alpha * acc_scr[rows, :]
            acc_scr[n * tq:(n + 1) * tq, :] += _bdot(p_scr[n * tq:(n + 1) * tq, :],
                                                     v_ref[pl.ds(start, tk), vcols])

    ratio = tq // tk

    def body(j, _):
        step(j, None)
        return 0

    lax.fori_loop(0, ratio * i, body, 0)
    for d in range(ratio):
        step(ratio * i + d, -d * tk)
    return acc_scr[...], l_scr[...].sum(axis=1, keepdims=True)


def _softmax_scratch(n_rows, tk, v_width):
    return [pltpu.VMEM((n_rows, tk), F32), pltpu.VMEM((n_rows, tk), BF16),
            pltpu.VMEM((n_rows, LANES), F32), pltpu.VMEM((n_rows, LANES), F32),
            pltpu.VMEM((n_rows, v_width), F32)]


def _diff_kernel(q_ref, k_ref, v_ref, lam_ref, sub_ref, o_ref, *scratch, lambda_init, tk):
    tq = q_ref.shape[0]
    i = pl.program_id(2)
    n_heads = q_ref.shape[1] // LANES
    streams = []
    for hd in range(n_heads):
        cols = slice(hd * LANES, (hd + 1) * LANES)
        for qh in _split_head_pair(q_ref[:, cols]):
            streams.append((qh, cols, cols))
    acc, l = _softmax_sweep(streams, k_ref, v_ref, scratch, i, tq, tk)
    o = acc / l
    lp = lam_ref[...]
    lam = (jnp.exp(jnp.sum(lp[0:1, :] * lp[1:2, :], axis=1, keepdims=True))
           - jnp.exp(jnp.sum(lp[2:3, :] * lp[3:4, :], axis=1, keepdims=True)) + lambda_init)
    for hd in range(n_heads):
        r0 = 2 * hd * tq
        od = o[r0:r0 + tq, :] - lam * o[r0 + tq:r0 + 2 * tq, :]
        o_ref[:, hd * LANES:(hd + 1) * LANES] = (
            _rms_rows(od, sub_ref[...]) * (1.0 - lambda_init)).astype(o_ref.dtype)


def _fox_kernel(q_ref, k_ref, v_ref, o_ref, *scratch, tk):
    tq = q_ref.shape[0]
    i = pl.program_id(2)
    n_heads = q_ref.shape[1] // LANES
    streams = [(q_ref[:, hd * LANES:(hd + 1) * LANES], slice(hd * LANES, (hd + 1) * LANES),
                slice((hd // 2) * LANES, (hd // 2 + 1) * LANES)) for hd in range(n_heads)]
    acc, l = _softmax_sweep(streams, k_ref, v_ref, scratch, i, tq, tk)
    o = acc / l
    lane = lax.broadcasted_iota(jnp.int32, (tq, LANES), 1)
    for pair in range(n_heads // 2):
        r0 = 2 * pair * tq
        o_ref[:, pair * LANES:(pair + 1) * LANES] = jnp.where(
            lane < HEAD_DIM, o[r0:r0 + tq, :], o[r0 + tq:r0 + 2 * tq, :]).astype(o_ref.dtype)


def _sb_kernel(q_ref, k_ref, v_ref, o_ref, z_scr, sp_scr, c_scr, w_scr, later_scr, acc_scr, *, tk):
    tq = q_ref.shape[0]
    i = pl.program_id(2)
    ratio = tq // tk
    n_pairs = q_ref.shape[1] // LANES
    heads = []
    for pair in range(n_pairs):
        kv_cols = slice(pair * LANES, (pair + 1) * LANES)
        heads += [(qh, kv_cols) for qh in _split_head_pair(q_ref[:, kv_cols])]
    sub = min(SB_SUM_BLOCK, tk)
    n_sub = tk // sub
    lanes_per_sub = sub // LANES
    neg_suffix = jnp.where(lax.broadcasted_iota(jnp.int32, (sub, sub), 0)
                           >= lax.broadcasted_iota(jnp.int32, (sub, sub), 1), -1.0, 0.0).astype(BF16)
    later_scr[...] = jnp.zeros(later_scr.shape, F32)
    acc_scr[...] = jnp.zeros(acc_scr.shape, F32)

    def step(j, limit):
        start = pl.multiple_of(j * tk, tk)
        for n, (qh, kv_cols) in enumerate(heads):
            z_scr[n * tq:(n + 1) * tq, :] = _nt_dot(qh, k_ref[pl.ds(start, tk), kv_cols])
        for n, (_, kv_cols) in enumerate(heads):
            head_rows = slice(n * tq, (n + 1) * tq)
            chunks = range(n * tq // ROW_CHUNK, (n + 1) * tq // ROW_CHUNK)
            for rc in chunks:
                rows = slice(rc * ROW_CHUNK, (rc + 1) * ROW_CHUNK)
                z = z_scr[rows, :]
                neg_abs = lax.bitcast_convert_type(
                    lax.bitcast_convert_type(z, jnp.int32) | jnp.int32(-2 ** 31), F32)
                sp = jnp.maximum(z, 0.0) + jnp.log2(1.0 + jnp.exp2(neg_abs))
                if limit is not None:
                    sp = jnp.where(_key_minus_query(rc * ROW_CHUNK, ROW_CHUNK, tq, tk) < limit, sp, 0.0)
                sp_scr[rows, :] = sp.astype(BF16)
            for b in range(n_sub):
                c_scr[head_rows, b * sub:(b + 1) * sub] = _bdot(
                    sp_scr[head_rows, b * sub:(b + 1) * sub], neg_suffix)
            for rc in chunks:
                rows = slice(rc * ROW_CHUNK, (rc + 1) * ROW_CHUNK)
                later = later_scr[rows, :]
                for b in reversed(range(n_sub)):
                    for c in range(b * lanes_per_sub, (b + 1) * lanes_per_sub):
                        cols = slice(c * LANES, (c + 1) * LANES)
                        w = jnp.exp2(z_scr[rows, cols] + c_scr[rows, cols] + later)
                        if limit is not None:
                            keep = _key_minus_query(rc * ROW_CHUNK, ROW_CHUNK, tq, tk)[:, cols] < limit
                            w = jnp.where(keep, w, 0.0)
                        w_scr[rows, cols] = w.astype(BF16)
                    later = later + c_scr[rows, b * sub:b * sub + 1]
                later_scr[rows, :] = later
            acc_scr[head_rows, :] += _bdot(w_scr[head_rows, :], v_ref[pl.ds(start, tk), kv_cols])

    for d in reversed(range(ratio)):
        step(ratio * i + d, -d * tk)

    def body(n, _):
        step(ratio * i - 1 - n, None)
        return 0

    lax.fori_loop(0, ratio * i, body, 0)
    lane = lax.broadcasted_iota(jnp.int32, (tq, LANES), 1)
    for pair in range(n_pairs):
        r0 = 2 * pair * tq
        o_ref[:, pair * LANES:(pair + 1) * LANES] = jnp.where(
            lane < HEAD_DIM, acc_scr[r0:r0 + tq, :], acc_scr[r0 + tq:r0 + 2 * tq, :]).astype(o_ref.dtype)


def _sb_scratch(n_rows, tk):
    return [pltpu.VMEM((n_rows, tk), F32), pltpu.VMEM((n_rows, tk), BF16),
            pltpu.VMEM((n_rows, tk), F32), pltpu.VMEM((n_rows, tk), BF16),
            pltpu.VMEM((n_rows, LANES), F32), pltpu.VMEM((n_rows, LANES), F32)]


def _attention_call(body, name, q, k, v, extra, batch, seq, n_groups, tk, scratch_fn):
    n = q.shape[0]
    tq = min(seq, ATTN_Q_BLOCK)
    tk = min(tk, tq)
    nq = seq // tq
    q_width, v_width = q.shape[1] // n_groups, v.shape[1] // n_groups
    return pl.pallas_call(
        functools.partial(body, tk=tk),
        grid=(batch, n_groups, nq),
        in_specs=[pl.BlockSpec((tq, q_width), lambda b, g, i: (b * nq + i, g)),
                  pl.BlockSpec((seq, q_width), lambda b, g, i: (b, g)),
                  pl.BlockSpec((seq, v_width), lambda b, g, i: (b, g))] + [_full(e.shape) for e in extra],
        out_specs=pl.BlockSpec((tq, v_width), lambda b, g, i: (b * nq + i, g)),
        out_shape=jax.ShapeDtypeStruct((n, v.shape[1]), BF16),
        scratch_shapes=scratch_fn(2 * tq * v_width // LANES, tk),
        compiler_params=_cparams(3),
        name=name,
    )(q, k, v, *extra)


def _merge_kernel(x_ref, u_ref, halo_ref, od_ref, os_ref, of_ref, an_ref, pw_ref, ps_ref,
                  wbr_ref, wg_ref, bg_ref, wout_ref, o_ref):
    tm, d = x_ref.shape
    x = x_ref[...]
    h = _rms_rows(x, an_ref[...]).astype(BF16)

    u = u_ref[...]
    first_tile = pl.program_id(1) == 0
    halo = jnp.where(first_tile, 0.0, halo_ref[...])
    ext = jnp.concatenate([halo, u], axis=0)
    lane = lax.broadcasted_iota(jnp.int32, (tm, POOL_WIDTH), 1)
    group = lane >> 6
    win = jnp.zeros_like(u)
    width = jnp.zeros((tm, POOL_WIDTH), jnp.int32)
    span = 1
    for g, w in enumerate(POOL_WINDOWS):
        while span < w:
            ext = ext + pltpu.roll(ext, span, 0)
            span *= 2
        win = jnp.where(group == g, ext[POOL_HALO:, :], win)
        width = jnp.where(group == g, w, width)
    pos = pl.program_id(1) * tm + lax.broadcasted_iota(jnp.int32, (tm, POOL_WIDTH), 0)
    count = jnp.minimum(pos + 1, width).astype(F32)
    pooled = win / count - u
    o_pool = (_bdot(pooled.astype(BF16), pw_ref[...]) * ps_ref[...]).astype(BF16)

    merged = jnp.zeros((tm, d), F32)
    lo = 0
    for nb, o in enumerate((o_pool, od_ref[...], os_ref[...], of_ref[...])):
        width_n = BRANCH_WIDTHS[nb]
        y = _bdot(o, wbr_ref[lo:lo + width_n, :])
        gate = jax.nn.sigmoid(_nt_dot(h, wg_ref[nb * d:(nb + 1) * d, :]) + bg_ref[nb:nb + 1, :])
        merged = merged + gate * y
        lo += width_n
    o_ref[...] = x + _bdot(merged.astype(BF16), wout_ref[...])


def _merge_call(xf, u, od, os_, of, an, pw, ps, wbr, wg, bg, wout, batch, seq):
    n, d = xf.shape
    tm = min(seq, ROW_TILE)
    ns = seq // tm
    row = lambda b, s: (b * ns + s, 0)
    halo_blocks = tm // POOL_HALO
    halo = lambda b, s: (jnp.maximum((b * ns + s) * halo_blocks - 1, 0), 0)
    return pl.pallas_call(
        _merge_kernel,
        grid=(batch, ns),
        in_specs=[pl.BlockSpec((tm, d), row), pl.BlockSpec((tm, POOL_WIDTH), row),
                  pl.BlockSpec((POOL_HALO, POOL_WIDTH), halo),
                  pl.BlockSpec((tm, DIFF_WIDTH), row), pl.BlockSpec((tm, SB_WIDTH), row),
                  pl.BlockSpec((tm, FOX_WIDTH), row),
                  _full(an.shape), _full(pw.shape), _full(ps.shape), _full(wbr.shape),
                  _full(wg.shape), _full(bg.shape), _full(wout.shape)],
        out_specs=pl.BlockSpec((tm, d), row),
        out_shape=jax.ShapeDtypeStruct((n, d), F32),
        compiler_params=_cparams(2),
        name="merge_out_proj",
    )(xf, u, u, od, os_, of, an, pw, ps, wbr, wg, bg, wout)


def _ffn_kernel(x_ref, fn_ref, wup_ref, wdn_ref, o_ref, *, chunk):
    x = x_ref[...]
    h = _rms_rows(x, fn_ref[...]).astype(BF16)
    d_ff = wdn_ref.shape[0]
    acc = x
    for c in range(d_ff // chunk):
        gate = _bdot(h, wup_ref[:, c * chunk:(c + 1) * chunk])
        up = _bdot(h, wup_ref[:, d_ff + c * chunk:d_ff + (c + 1) * chunk])
        act = (gate * jax.nn.sigmoid(gate) * up).astype(BF16)
        acc = acc + _bdot(act, wdn_ref[c * chunk:(c + 1) * chunk, :])
    o_ref[...] = acc


def _ffn_call(xf, fn, wup, wdn):
    n, d = xf.shape
    tm = min(n, ROW_TILE)
    return pl.pallas_call(
        functools.partial(_ffn_kernel, chunk=256),
        grid=(n // tm,),
        in_specs=[pl.BlockSpec((tm, d), lambda i: (i, 0)), _full(fn.shape), _full(wup.shape),
                  _full(wdn.shape)],
        out_specs=pl.BlockSpec((tm, d), lambda i: (i, 0)),
        out_shape=jax.ShapeDtypeStruct((n, d), F32),
        compiler_params=_cparams(1),
        name="swiglu_ffn",
    )(xf, fn, wup, wdn)


def _group_mean_matrix():
    g = np.kron(np.eye(256 // HEAD_DIM), np.full((HEAD_DIM, HEAD_DIM), 1.0 / HEAD_DIM))
    return jnp.asarray(g, BF16)


def _lower_tri(nrows):
    return jnp.asarray(np.tril(np.ones((nrows, nrows))), BF16)


def _decay_selector():
    sel = np.zeros((LANES, 2 * FOX_HEADS * LANES), np.float32)
    for hd in range(FOX_HEADS):
        qb = hd * LANES + AUG_LANE
        kb = (FOX_HEADS + hd) * LANES + AUG_LANE
        for term in range(3):
            sel[4 * term + hd, qb + term] = 1.0
            sel[12, qb + 3 + term] = 1.0
            sel[12, kb + term] = 1.0
            sel[4 * term + hd, kb + 3 + term] = -1.0
    return jnp.asarray(sel, BF16)


def kernel(x, attn_norm, ffn_norm, w_in, b_gate, b_forget, pool_w, pool_scale, diff_q_norm,
           diff_k_norm, diff_subln, lam_q1, lam_k1, lam_q2, lam_k2, fox_q_norm, fox_k_norm,
           w_branch, w_out, w_ffn_up, w_ffn_down):
    batch, seq, d = x.shape
    depth = attn_norm.shape[0]
    assert seq % min(seq, ROW_TILE) == 0 and seq % min(seq, ATTN_Q_BLOCK) == 0
    xf = x.reshape(batch * seq, d)
    cos, sin = _rope_tables(seq)
    gmat = _group_mean_matrix()
    ltri = _lower_tri(min(seq, 256))
    sel = _decay_selector()
    softmax_scratch = functools.partial(_softmax_scratch, v_width=LANES)

    for layer in range(depth):
        wt = jnp.swapaxes(w_in[layer], 0, 1)
        wf = wt[MAIN_WIDTH:MAIN_WIDTH + FOX_HEADS]
        wf_pad = jnp.concatenate([wf, wf, wf, jnp.zeros((LANES - 3 * FOX_HEADS, d), wt.dtype)], axis=0)
        w_all = jnp.concatenate([wt[:MAIN_WIDTH], wf_pad], axis=0).astype(BF16)
        bfg = b_forget[layer]
        bf_pad = jnp.concatenate([bfg, bfg, bfg, jnp.zeros((LANES - 3 * FOX_HEADS,), F32)])[None, :]
        gains = jnp.stack([jnp.tile(g[layer], 256 // HEAD_DIM)
                           for g in (diff_q_norm, diff_k_norm, fox_q_norm, fox_k_norm)])
        an = attn_norm[layer][None, :]

        u, dq, dk, dv, sq, sk, sv, fq, fk, fv = _proj_call(
            xf, an, w_all, bf_pad, cos, sin, gains, gmat, ltri, sel, batch, seq)

        lam = jnp.stack([lam_q1[layer], lam_k1[layer], lam_q2[layer], lam_k2[layer]])
        od = _attention_call(
            functools.partial(_diff_kernel, lambda_init=_diff_lambda_init(layer)), "diff_attention",
            dq, dk, dv, (lam, diff_subln[layer][None, :]), batch, seq, DIFF_GROUPS,
            SOFTMAX_K_BLOCK, softmax_scratch)
        os_ = _attention_call(_sb_kernel, "stick_breaking_attention", sq, sk, sv, (), batch, seq,
                              1, SB_K_BLOCK, _sb_scratch)
        of = _attention_call(_fox_kernel, "forgetting_attention", fq, fk, fv, (), batch, seq,
                             1, SOFTMAX_K_BLOCK, softmax_scratch)

        pw = jax.scipy.linalg.block_diag(*[pool_w[layer, g] for g in range(len(POOL_WINDOWS))])
        x1 = _merge_call(xf, u, od, os_, of, an, pw.astype(BF16), pool_scale[layer][None, :],
                         w_branch[layer].astype(BF16), wt[MAIN_WIDTH + FOX_HEADS:].astype(BF16),
                         b_gate[layer], w_out[layer].astype(BF16), batch, seq)
        xf = _ffn_call(x1, ffn_norm[layer][None, :], w_ffn_up[layer].astype(BF16),
                       w_ffn_down[layer].astype(BF16))
    return xf.reshape(batch, seq, d)
```

```python
import functools
import math

import numpy as np
import jax
import jax.numpy as jnp
from jax import lax
from jax.experimental import pallas as pl
from jax.experimental.pallas import tpu as pltpu

F32 = jnp.float32
BF16 = jnp.bfloat16

HEAD_DIM = 64
POOL_WINDOWS = (2, 4, 8, 16)
POOL_WIDTH = 256
DIFF_HEADS = 4
DIFF_QK_WIDTH = 512
DIFF_WIDTH = 512
SB_WIDTH = 256
FOX_HEADS = 4
FOX_WIDTH = 256
N_BRANCHES = 4
BRANCH_WIDTHS = (POOL_WIDTH, DIFF_WIDTH, SB_WIDTH, FOX_WIDTH)
MAIN_WIDTH = 3328
ROPE_THETA = 10000.0
NORM_EPS = 1e-6
QK_SCALE = HEAD_DIM ** -0.5
LOG2E = math.log2(math.e)
Q_SCALE = QK_SCALE * LOG2E

LANES = 128
ATTN_Q_BLOCK = 512
SOFTMAX_K_BLOCK = 512
SB_SUM_BLOCK = 256
DIFF_SB_GROUPS = 2
SOFTPLUS_LINEAR = 64.0
ROW_TILE = 512
ROW_CHUNK = 128
POOL_HALO = 16
AUG_LANE = 64
VMEM_LIMIT = 56 * 1024 * 1024


def _diff_lambda_init(layer):
    return 0.8 - 0.6 * math.exp(-0.3 * layer)


def _cparams(n_axes):
    return pltpu.CompilerParams(dimension_semantics=("arbitrary",) * n_axes,
                                vmem_limit_bytes=VMEM_LIMIT)


def _full(shape):
    return pl.BlockSpec(shape, lambda *_: (0,) * len(shape))


def _rms_rows(x, gain):
    ms = jnp.mean(x * x, axis=-1, keepdims=True)
    return x * lax.rsqrt(ms + NORM_EPS) * gain


def _bdot(a, b):
    return jnp.dot(a, b, preferred_element_type=F32)


def _nt_dot(a, b):
    return lax.dot_general(a, b, (((1,), (1,)), ((), ())), preferred_element_type=F32)


def _split_head_pair(q):
    lane = lax.broadcasted_iota(jnp.int32, q.shape, 1)
    qf = q.astype(F32)
    return (jnp.where(lane < HEAD_DIM, qf, 0.0).astype(q.dtype),
            jnp.where(lane >= HEAD_DIM, qf, 0.0).astype(q.dtype))


def _split3(v):
    a1 = v.astype(BF16)
    r1 = v - a1.astype(F32)
    a2 = r1.astype(BF16)
    r2 = r1 - a2.astype(F32)
    a3 = r2.astype(BF16)
    return a1, a2, a3


def _rope_table_kernel(inv_ref, cos_ref, sin_ref):
    rows = cos_ref.shape[0]
    base = pl.program_id(0) * rows
    pos = (base + lax.broadcasted_iota(jnp.int32, (rows, LANES), 0)).astype(F32)
    ang = pos * inv_ref[...]
    lane = lax.broadcasted_iota(jnp.int32, (rows, LANES), 1)
    first_half = (lane & (HEAD_DIM // 2)) == 0
    s = jnp.sin(ang)
    cos_ref[...] = jnp.cos(ang)
    sin_ref[...] = jnp.where(first_half, -s, s)


def _rope_tables(seq):
    half = HEAD_DIM // 2
    inv_freq = ROPE_THETA ** (-jnp.arange(half, dtype=F32) / half)
    inv = jnp.tile(inv_freq, LANES // half)[None, :]
    rows = min(seq, ROW_TILE)
    return pl.pallas_call(
        _rope_table_kernel,
        grid=(seq // rows,),
        in_specs=[_full((1, LANES))],
        out_specs=[pl.BlockSpec((rows, LANES), lambda i: (i, 0))] * 2,
        out_shape=[jax.ShapeDtypeStruct((seq, LANES), F32)] * 2,
        compiler_params=_cparams(1),
        name="rope_tables",
    )(inv)


def _proj_kernel(x_ref, an_ref, w_ref, bf_ref, cos_ref, sin_ref, gains_ref, gmat_ref,
                 ltri_ref, sel_ref,
                 u_ref, dq_ref, dk_ref, dv_ref, sq_ref, sk_ref, sv_ref, fq_ref, fk_ref, fv_ref,
                 carry_ref):
    tm = x_ref.shape[0]
    h = _rms_rows(x_ref[...], an_ref[...]).astype(BF16)

    def proj(lo, width):
        return _nt_dot(h, w_ref[lo:lo + width, :])

    def group_norm(y, gain):
        ms = _bdot((y * y).astype(BF16), gmat_ref[...])
        return y * lax.rsqrt(ms + NORM_EPS) * gain

    cos = cos_ref[...]
    sin = sin_ref[...]
    lane = lax.broadcasted_iota(jnp.int32, (tm, LANES), 1)
    first_half = (lane & (HEAD_DIM // 2)) == 0
    low_head = lane < HEAD_DIM

    def rope(y):
        partner = jnp.where(first_half, pltpu.roll(y, LANES - HEAD_DIM // 2, 1),
                            pltpu.roll(y, HEAD_DIM // 2, 1))
        return y * cos + partner * sin

    u_ref[...] = proj(0, 256)

    gq = gains_ref[0:1, :]
    gk = gains_ref[1:2, :]
    for c in range(2):
        yq = group_norm(proj(256 + 256 * c, 256), gq)
        yk = group_norm(proj(768 + 256 * c, 256), gk)
        for hh in range(2):
            sl = slice(hh * LANES, (hh + 1) * LANES)
            col = slice(256 * c + hh * LANES, 256 * c + (hh + 1) * LANES)
            dq_ref[:, col] = (rope(yq[:, sl]) * Q_SCALE).astype(BF16)
            dk_ref[:, col] = rope(yk[:, sl]).astype(BF16)
        dv_ref[:, 256 * c:256 * (c + 1)] = proj(1280 + 256 * c, 256).astype(BF16)

    sq_ref[...] = (proj(1792, 256) * Q_SCALE).astype(BF16)
    sk_ref[...] = proj(2048, 256).astype(BF16)
    sv_ref[...] = proj(2304, 256).astype(BF16)
    fv_ref[...] = proj(3072, 256).astype(BF16)

    fl = proj(MAIN_WIDTH, LANES) + bf_ref[...]
    logf = jnp.minimum(fl, 0.0) - jnp.log1p(jnp.exp(-jnp.abs(fl)))

    @pl.when(pl.program_id(1) == 0)
    def _():
        carry_ref[...] = jnp.zeros_like(carry_ref)

    carry = carry_ref[0:1, :]
    sub = ltri_ref.shape[0]
    pieces = []
    for sb in range(tm // sub):
        a1, a2, a3 = _split3(logf[sb * sub:(sb + 1) * sub, :])
        ltri = ltri_ref[...]
        cum = _bdot(ltri, a1) + _bdot(ltri, a2) + _bdot(ltri, a3) + carry
        carry = cum[sub - 1:sub, :]
        pieces.append(cum)
    carry_ref[0:1, :] = carry
    cum = jnp.concatenate(pieces, axis=0) if len(pieces) > 1 else pieces[0]

    c1, c2, c3 = (c.astype(F32) for c in _split3(cum * LOG2E))
    packed = jnp.where(lane < 4, c1, jnp.where(lane < 8, c2, jnp.where(lane < 12, c3,
                       jnp.where(lane == 12, 1.0, 0.0))))
    aug = _bdot(packed.astype(BF16), sel_ref[...])

    yq = group_norm(proj(2560, 256), gains_ref[2:3, :]) * Q_SCALE
    yk = group_norm(proj(2816, 256), gains_ref[3:4, :])
    for src, dst, off in ((yq, fq_ref, 0), (yk, fk_ref, 4 * LANES)):
        for pair in range(2):
            v = src[:, pair * LANES:(pair + 1) * LANES]
            heads = (v, pltpu.roll(v, HEAD_DIM, 1))
            for hh in range(2):
                col = (2 * pair + hh) * LANES
                dst[:, col:col + LANES] = jnp.where(
                    low_head, heads[hh], aug[:, off + col:off + col + LANES]).astype(BF16)


def _proj_call(xf, an, w_all, bf_pad, cos, sin, gains, gmat, ltri, sel, batch, seq):
    n, d = xf.shape
    tm = min(seq, ROW_TILE)
    ns = seq // tm
    row = lambda b, s: (b * ns + s, 0)
    widths = (256, 512, 512, 512, 256, 256, 256, 512, 512, 256)
    dtypes = (F32,) + (BF16,) * 9
    return pl.pallas_call(
        _proj_kernel,
        grid=(batch, ns),
        in_specs=[pl.BlockSpec((tm, d), row), _full(an.shape), _full(w_all.shape), _full(bf_pad.shape),
                  pl.BlockSpec((tm, LANES), lambda b, s: (s, 0)),
                  pl.BlockSpec((tm, LANES), lambda b, s: (s, 0)),
                  _full(gains.shape), _full(gmat.shape), _full(ltri.shape), _full(sel.shape)],
        out_specs=[pl.BlockSpec((tm, w), row) for w in widths],
        out_shape=[jax.ShapeDtypeStruct((n, w), t) for w, t in zip(widths, dtypes)],
        scratch_shapes=[pltpu.VMEM((8, LANES), F32)],
        compiler_params=_cparams(2),
        name="norm_in_proj",
    )(xf, an, w_all, bf_pad, cos, sin, gains, gmat, ltri, sel)


def _key_minus_query(row0, rows, tq, tk):
    r = (row0 + lax.broadcasted_iota(jnp.int32, (rows, tk), 0)) & (tq - 1)
    c = lax.broadcasted_iota(jnp.int32, (rows, tk), 1)
    return c - r


def _softmax_steps(streams, k_ref, v_ref, scratch, tq, tk):
    s_scr, p_scr, m_scr, l_scr, acc_scr = scratch
    n_lane_chunks = tk // LANES
    m_scr[...] = jnp.full(m_scr.shape, -1e30, F32)
    l_scr[...] = jnp.zeros(l_scr.shape, F32)
    acc_scr[...] = jnp.zeros(acc_scr.shape, F32)

    def step(j, limit):
        start = pl.multiple_of(j * tk, tk)
        for n, (q, cols, _) in enumerate(streams):
            s = _nt_dot(q, k_ref[pl.ds(start, tk), cols])
            if limit is not None:
                s = jnp.where(_key_minus_query(0, tq, tq, tk) <= limit, s, -jnp.inf)
            s_scr[n * tq:(n + 1) * tq, :] = s
        for n, (_, _, vcols) in enumerate(streams):
            for rc in range(n * tq // ROW_CHUNK, (n + 1) * tq // ROW_CHUNK):
                rows = slice(rc * ROW_CHUNK, (rc + 1) * ROW_CHUNK)
                mx = s_scr[rows, 0:LANES]
                for c in range(1, n_lane_chunks):
                    mx = jnp.maximum(mx, s_scr[rows, c * LANES:(c + 1) * LANES])
                m_old = m_scr[rows, :]
                m_new = jnp.maximum(m_old, mx.max(axis=1, keepdims=True))
                alpha = jnp.exp2(m_old - m_new)
                m_scr[rows, :] = m_new
                psum = None
                for c in range(n_lane_chunks):
                    p = jnp.exp2(s_scr[rows, c * LANES:(c + 1) * LANES] - m_new)
                    psum = p if psum is None else psum + p
                    p_scr[rows, c * LANES:(c + 1) * LANES] = p.astype(BF16)
                l_scr[rows, :] = alpha * l_scr[rows, :] + psum
                acc_scr[rows, :] = alpha * acc_scr[rows, :]
            acc_scr[n * tq:(n + 1) * tq, :] += _bdot(p_scr[n * tq:(n + 1) * tq, :],
                                                     v_ref[pl.ds(start, tk), vcols])

    def finish():
        return acc_scr[...] / l_scr[...].sum(axis=1, keepdims=True)

    return step, finish


def _softmax_scratch(n_rows, tk):
    return [pltpu.VMEM((n_rows, tk), F32), pltpu.VMEM((n_rows, tk), BF16),
            pltpu.VMEM((n_rows, LANES), F32), pltpu.VMEM((n_rows, LANES), F32),
            pltpu.VMEM((n_rows, LANES), F32)]


def _sweep_back(steps, i):
    for step in steps:
        step(i, 0)

    def body(n, _):
        for step in steps:
            step(i - 1 - n, None)
        return 0

    lax.fori_loop(0, i, body, 0)


def _diff_streams(q_ref):
    streams = []
    for hd in range(q_ref.shape[1] // LANES):
        cols = slice(hd * LANES, (hd + 1) * LANES)
        streams += [(qh, cols, cols) for qh in _split_head_pair(q_ref[:, cols])]
    return streams


def _diff_finish(o, lam_ref, sub_ref, o_ref, tq, lambda_init):
    lp = lam_ref[...]
    lam = (jnp.exp(jnp.sum(lp[0:1, :] * lp[1:2, :], axis=1, keepdims=True))
           - jnp.exp(jnp.sum(lp[2:3, :] * lp[3:4, :], axis=1, keepdims=True)) + lambda_init)
    for hd in range(o_ref.shape[1] // LANES):
        r0 = 2 * hd * tq
        od = o[r0:r0 + tq, :] - lam * o[r0 + tq:r0 + 2 * tq, :]
        o_ref[:, hd * LANES:(hd + 1) * LANES] = (
            _rms_rows(od, sub_ref[...]) * (1.0 - lambda_init)).astype(o_ref.dtype)


def _fox_kernel(q_ref, k_ref, v_ref, o_ref, *scratch, tk):
    tq = q_ref.shape[0]
    i = pl.program_id(2)
    n_heads = q_ref.shape[1] // LANES
    streams = [(q_ref[:, hd * LANES:(hd + 1) * LANES], slice(hd * LANES, (hd + 1) * LANES),
                slice((hd // 2) * LANES, (hd // 2 + 1) * LANES)) for hd in range(n_heads)]
    step, finish = _softmax_steps(streams, k_ref, v_ref, scratch, tq, tk)
    _sweep_back([step], i)
    o = finish()
    lane = lax.broadcasted_iota(jnp.int32, (tq, LANES), 1)
    for pair in range(n_heads // 2):
        r0 = 2 * pair * tq
        o_ref[:, pair * LANES:(pair + 1) * LANES] = jnp.where(
            lane < HEAD_DIM, o[r0:r0 + tq, :], o[r0 + tq:r0 + 2 * tq, :]).astype(o_ref.dtype)


def _sb_steps(q_ref, k_ref, v_ref, scratch, tq, tk):
    z_scr, sp_scr, c_scr, w_scr, later_scr, acc_scr = scratch
    n_pairs = q_ref.shape[1] // LANES
    heads = []
    for pair in range(n_pairs):
        kv_cols = slice(pair * LANES, (pair + 1) * LANES)
        heads += [(qh, kv_cols) for qh in _split_head_pair(q_ref[:, kv_cols])]
    sub = min(SB_SUM_BLOCK, tk)
    n_sub = tk // sub
    lanes_per_sub = sub // LANES
    neg_suffix = jnp.where(lax.broadcasted_iota(jnp.int32, (sub, sub), 0)
                           >= lax.broadcasted_iota(jnp.int32, (sub, sub), 1), -1.0, 0.0).astype(BF16)
    later_scr[...] = jnp.zeros(later_scr.shape, F32)
    acc_scr[...] = jnp.zeros(acc_scr.shape, F32)

    def step(j, limit):
        start = pl.multiple_of(j * tk, tk)
        for n, (qh, kv_cols) in enumerate(heads):
            z_scr[n * tq:(n + 1) * tq, :] = _nt_dot(qh, k_ref[pl.ds(start, tk), kv_cols])
        for n, (_, kv_cols) in enumerate(heads):
            head_rows = slice(n * tq, (n + 1) * tq)
            chunks = range(n * tq // ROW_CHUNK, (n + 1) * tq // ROW_CHUNK)
            for rc in chunks:
                rows = slice(rc * ROW_CHUNK, (rc + 1) * ROW_CHUNK)
                z = z_scr[rows, :]
                sp = jnp.where(z < SOFTPLUS_LINEAR, jnp.log2(1.0 + jnp.exp2(z)), z)
                if limit is not None:
                    sp = jnp.where(_key_minus_query(rc * ROW_CHUNK, ROW_CHUNK, tq, tk) < limit, sp, 0.0)
                sp_scr[rows, :] = sp.astype(BF16)
            for b in range(n_sub):
                c_scr[head_rows, b * sub:(b + 1) * sub] = _bdot(
                    sp_scr[head_rows, b * sub:(b + 1) * sub], neg_suffix)
            for rc in chunks:
                rows = slice(rc * ROW_CHUNK, (rc + 1) * ROW_CHUNK)
                later = later_scr[rows, :]
                for b in reversed(range(n_sub)):
                    for c in range(b * lanes_per_sub, (b + 1) * lanes_per_sub):
                        cols = slice(c * LANES, (c + 1) * LANES)
                        w = jnp.exp2(z_scr[rows, cols] + c_scr[rows, cols] + later)
                        if limit is not None:
                            keep = _key_minus_query(rc * ROW_CHUNK, ROW_CHUNK, tq, tk)[:, cols] < limit
                            w = jnp.where(keep, w, 0.0)
                        w_scr[rows, cols] = w.astype(BF16)
                    later = later + c_scr[rows, b * sub:b * sub + 1]
                later_scr[rows, :] = later
            acc_scr[head_rows, :] += _bdot(w_scr[head_rows, :], v_ref[pl.ds(start, tk), kv_cols])

    def finish():
        return acc_scr[...]

    return step, finish


def _sb_scratch(n_rows, tk):
    return [pltpu.VMEM((n_rows, tk), F32), pltpu.VMEM((n_rows, tk), BF16),
            pltpu.VMEM((n_rows, tk), F32), pltpu.VMEM((n_rows, tk), BF16),
            pltpu.VMEM((n_rows, LANES), F32), pltpu.VMEM((n_rows, LANES), F32)]


def _diff_sb_kernel(dq_ref, dk_ref, dv_ref, sq_ref, sk_ref, sv_ref, lam_ref, sub_ref, od_ref, os_ref,
                    *scratch, lambda_init, tk):
    tq = dq_ref.shape[0]
    i = pl.program_id(2)
    n_soft = len(_softmax_scratch(8, tk))
    diff_step, diff_out = _softmax_steps(_diff_streams(dq_ref), dk_ref, dv_ref, scratch[:n_soft], tq, tk)
    sb_step, sb_out = _sb_steps(sq_ref, sk_ref, sv_ref, scratch[n_soft:], tq, tk)
    _sweep_back([diff_step, sb_step], i)
    _diff_finish(diff_out(), lam_ref, sub_ref, od_ref, tq, lambda_init)
    acc = sb_out()
    lane = lax.broadcasted_iota(jnp.int32, (tq, LANES), 1)
    for pair in range(os_ref.shape[1] // LANES):
        r0 = 2 * pair * tq
        os_ref[:, pair * LANES:(pair + 1) * LANES] = jnp.where(
            lane < HEAD_DIM, acc[r0:r0 + tq, :], acc[r0 + tq:r0 + 2 * tq, :]).astype(os_ref.dtype)


def _attention_specs(arrays, batch, seq, n_groups, tq):
    nq = seq // tq
    q, k, v = arrays
    qw, vw = q.shape[1] // n_groups, v.shape[1] // n_groups
    return [pl.BlockSpec((tq, qw), lambda b, g, i: (b * nq + i, g)),
            pl.BlockSpec((seq, qw), lambda b, g, i: (b, g)),
            pl.BlockSpec((seq, vw), lambda b, g, i: (b, g))], pl.BlockSpec(
                (tq, vw), lambda b, g, i: (b * nq + i, g))


def _fox_call(q, k, v, batch, seq):
    tq = min(seq, ATTN_Q_BLOCK)
    tk = min(SOFTMAX_K_BLOCK, tq)
    in_specs, out_spec = _attention_specs((q, k, v), batch, seq, 1, tq)
    return pl.pallas_call(
        functools.partial(_fox_kernel, tk=tk),
        grid=(batch, 1, seq // tq),
        in_specs=in_specs,
        out_specs=out_spec,
        out_shape=jax.ShapeDtypeStruct((q.shape[0], v.shape[1]), BF16),
        scratch_shapes=_softmax_scratch(2 * tq * v.shape[1] // LANES, tk),
        compiler_params=_cparams(3),
        name="forgetting_attention",
    )(q, k, v)


def _diff_sb_call(dq, dk, dv, sq, sk, sv, lam, subln, lambda_init, batch, seq):
    tq = min(seq, ATTN_Q_BLOCK)
    tk = min(SOFTMAX_K_BLOCK, tq)
    groups = DIFF_SB_GROUPS
    d_in, d_out = _attention_specs((dq, dk, dv), batch, seq, groups, tq)
    s_in, s_out = _attention_specs((sq, sk, sv), batch, seq, groups, tq)
    n = dq.shape[0]
    return pl.pallas_call(
        functools.partial(_diff_sb_kernel, lambda_init=lambda_init, tk=tk),
        grid=(batch, groups, seq // tq),
        in_specs=d_in + s_in + [_full(lam.shape), _full(subln.shape)],
        out_specs=[d_out, s_out],
        out_shape=[jax.ShapeDtypeStruct((n, dv.shape[1]), BF16),
                   jax.ShapeDtypeStruct((n, sv.shape[1]), BF16)],
        scratch_shapes=(_softmax_scratch(2 * tq * dv.shape[1] // groups // LANES, tk)
                        + _sb_scratch(2 * tq * sv.shape[1] // groups // LANES, tk)),
        compiler_params=_cparams(3),
        name="diff_and_stick_breaking_attention",
    )(dq, dk, dv, sq, sk, sv, lam, subln)


def _merge_kernel(x_ref, u_ref, halo_ref, od_ref, os_ref, of_ref, an_ref, pw_ref, ps_ref,
                  wbr_ref, wg_ref, bg_ref, wout_ref, o_ref):
    tm, d = x_ref.shape
    x = x_ref[...]
    h = _rms_rows(x, an_ref[...]).astype(BF16)

    u = u_ref[...]
    first_tile = pl.program_id(1) == 0
    halo = jnp.where(first_tile, 0.0, halo_ref[...])
    ext = jnp.concatenate([halo, u], axis=0)
    lane = lax.broadcasted_iota(jnp.int32, (tm, POOL_WIDTH), 1)
    group = lane >> 6
    win = jnp.zeros_like(u)
    width = jnp.zeros((tm, POOL_WIDTH), jnp.int32)
    span = 1
    for g, w in enumerate(POOL_WINDOWS):
        while span < w:
            ext = ext + pltpu.roll(ext, span, 0)
            span *= 2
        win = jnp.where(group == g, ext[POOL_HALO:, :], win)
        width = jnp.where(group == g, w, width)
    pos = pl.program_id(1) * tm + lax.broadcasted_iota(jnp.int32, (tm, POOL_WIDTH), 0)
    count = jnp.minimum(pos + 1, width).astype(F32)
    pooled = win / count - u
    o_pool = (_bdot(pooled.astype(BF16), pw_ref[...]) * ps_ref[...]).astype(BF16)

    merged = jnp.zeros((tm, d), F32)
    lo = 0
    for nb, o in enumerate((o_pool, od_ref[...], os_ref[...], of_ref[...])):
        width_n = BRANCH_WIDTHS[nb]
        y = _bdot(o, wbr_ref[lo:lo + width_n, :])
        gate = jax.nn.sigmoid(_nt_dot(h, wg_ref[nb * d:(nb + 1) * d, :]) + bg_ref[nb:nb + 1, :])
        merged = merged + gate * y
        lo += width_n
    o_ref[...] = x + _bdot(merged.astype(BF16), wout_ref[...])


def _merge_call(xf, u, od, os_, of, an, pw, ps, wbr, wg, bg, wout, batch, seq):
    n, d = xf.shape
    tm = min(seq, ROW_TILE)
    ns = seq // tm
    row = lambda b, s: (b * ns + s, 0)
    halo_blocks = tm // POOL_HALO
    halo = lambda b, s: (jnp.maximum((b * ns + s) * halo_blocks - 1, 0), 0)
    return pl.pallas_call(
        _merge_kernel,
        grid=(batch, ns),
        in_specs=[pl.BlockSpec((tm, d), row), pl.BlockSpec((tm, POOL_WIDTH), row),
                  pl.BlockSpec((POOL_HALO, POOL_WIDTH), halo),
                  pl.BlockSpec((tm, DIFF_WIDTH), row), pl.BlockSpec((tm, SB_WIDTH), row),
                  pl.BlockSpec((tm, FOX_WIDTH), row),
                  _full(an.shape), _full(pw.shape), _full(ps.shape), _full(wbr.shape),
                  _full(wg.shape), _full(bg.shape), _full(wout.shape)],
        out_specs=pl.BlockSpec((tm, d), row),
        out_shape=jax.ShapeDtypeStruct((n, d), F32),
        compiler_params=_cparams(2),
        name="merge_out_proj",
    )(xf, u, u, od, os_, of, an, pw, ps, wbr, wg, bg, wout)


def _ffn_kernel(x_ref, fn_ref, wup_ref, wdn_ref, o_ref, *, chunk):
    x = x_ref[...]
    h = _rms_rows(x, fn_ref[...]).astype(BF16)
    d_ff = wdn_ref.shape[0]
    acc = x
    for c in range(d_ff // chunk):
        gate = _bdot(h, wup_ref[:, c * chunk:(c + 1) * chunk])
        up = _bdot(h, wup_ref[:, d_ff + c * chunk:d_ff + (c + 1) * chunk])
        act = (gate * jax.nn.sigmoid(gate) * up).astype(BF16)
        acc = acc + _bdot(act, wdn_ref[c * chunk:(c + 1) * chunk, :])
    o_ref[...] = acc


def _ffn_call(xf, fn, wup, wdn):
    n, d = xf.shape
    tm = min(n, ROW_TILE)
    return pl.pallas_call(
        functools.partial(_ffn_kernel, chunk=256),
        grid=(n // tm,),
        in_specs=[pl.BlockSpec((tm, d), lambda i: (i, 0)), _full(fn.shape), _full(wup.shape),
                  _full(wdn.shape)],
        out_specs=pl.BlockSpec((tm, d), lambda i: (i, 0)),
        out_shape=jax.ShapeDtypeStruct((n, d), F32),
        compiler_params=_cparams(1),
        name="swiglu_ffn",
    )(xf, fn, wup, wdn)


def _group_mean_matrix():
    g = np.kron(np.eye(256 // HEAD_DIM), np.full((HEAD_DIM, HEAD_DIM), 1.0 / HEAD_DIM))
    return jnp.asarray(g, BF16)


def _lower_tri(nrows):
    return jnp.asarray(np.tril(np.ones((nrows, nrows))), BF16)


def _decay_selector():
    sel = np.zeros((LANES, 2 * FOX_HEADS * LANES), np.float32)
    for hd in range(FOX_HEADS):
        qb = hd * LANES + AUG_LANE
        kb = (FOX_HEADS + hd) * LANES + AUG_LANE
        for term in range(3):
            sel[4 * term + hd, qb + term] = 1.0
            sel[12, qb + 3 + term] = 1.0
            sel[12, kb + term] = 1.0
            sel[4 * term + hd, kb + 3 + term] = -1.0
    return jnp.asarray(sel, BF16)


def kernel(x, attn_norm, ffn_norm, w_in, b_gate, b_forget, pool_w, pool_scale, diff_q_norm,
           diff_k_norm, diff_subln, lam_q1, lam_k1, lam_q2, lam_k2, fox_q_norm, fox_k_norm,
           w_branch, w_out, w_ffn_up, w_ffn_down):
    batch, seq, d = x.shape
    depth = attn_norm.shape[0]
    assert seq % min(seq, ROW_TILE) == 0 and seq % min(seq, ATTN_Q_BLOCK) == 0
    xf = x.reshape(batch * seq, d)
    cos, sin = _rope_tables(seq)
    gmat = _group_mean_matrix()
    ltri = _lower_tri(min(seq, 256))
    sel = _decay_selector()

    for layer in range(depth):
        wt = jnp.swapaxes(w_in[layer], 0, 1)
        wf = wt[MAIN_WIDTH:MAIN_WIDTH + FOX_HEADS]
        wf_pad = jnp.concatenate([wf, wf, wf, jnp.zeros((LANES - 3 * FOX_HEADS, d), wt.dtype)], axis=0)
        w_all = jnp.concatenate([wt[:MAIN_WIDTH], wf_pad], axis=0).astype(BF16)
        bfg = b_forget[layer]
        bf_pad = jnp.concatenate([bfg, bfg, bfg, jnp.zeros((LANES - 3 * FOX_HEADS,), F32)])[None, :]
        gains = jnp.stack([jnp.tile(g[layer], 256 // HEAD_DIM)
                           for g in (diff_q_norm, diff_k_norm, fox_q_norm, fox_k_norm)])
        an = attn_norm[layer][None, :]

        u, dq, dk, dv, sq, sk, sv, fq, fk, fv = _proj_call(
            xf, an, w_all, bf_pad, cos, sin, gains, gmat, ltri, sel, batch, seq)

        lam = jnp.stack([lam_q1[layer], lam_k1[layer], lam_q2[layer], lam_k2[layer]])
        od, os_ = _diff_sb_call(dq, dk, dv, sq, sk, sv, lam, diff_subln[layer][None, :],
                                _diff_lambda_init(layer), batch, seq)
        of = _fox_call(fq, fk, fv, batch, seq)

        pw = jax.scipy.linalg.block_diag(*[pool_w[layer, g] for g in range(len(POOL_WINDOWS))])
        x1 = _merge_call(xf, u, od, os_, of, an, pw.astype(BF16), pool_scale[layer][None, :],
                         w_branch[layer].astype(BF16), wt[MAIN_WIDTH + FOX_HEADS:].astype(BF16),
                         b_gate[layer], w_out[layer].astype(BF16), batch, seq)
        xf = _ffn_call(x1, ffn_norm[layer][None, :], w_ffn_up[layer].astype(BF16),
                       w_ffn_down[layer].astype(BF16))
    return xf.reshape(batch, seq, d)
```

```python
import functools
import math

import numpy as np
import jax
import jax.numpy as jnp
from jax import lax
from jax.experimental import pallas as pl
from jax.experimental.pallas import tpu as pltpu

F32 = jnp.float32
BF16 = jnp.bfloat16

HEAD_DIM = 64
POOL_WINDOWS = (2, 4, 8, 16)
POOL_WIDTH = 256
DIFF_HEADS = 4
DIFF_QK_WIDTH = 512
DIFF_WIDTH = 512
SB_WIDTH = 256
FOX_HEADS = 4
FOX_WIDTH = 256
N_BRANCHES = 4
BRANCH_WIDTHS = (POOL_WIDTH, DIFF_WIDTH, SB_WIDTH, FOX_WIDTH)
MAIN_WIDTH = 3328
ROPE_THETA = 10000.0
NORM_EPS = 1e-6
QK_SCALE = HEAD_DIM ** -0.5
LOG2E = math.log2(math.e)
Q_SCALE = QK_SCALE * LOG2E

LANES = 128
ATTN_Q_BLOCK = 512
SOFTMAX_K_BLOCK = 512
SB_K_BLOCK = 512
SB_SUM_BLOCK = 256
SOFTPLUS_LINEAR = 64.0
DIFF_GROUPS = 1
ROW_TILE = 512
ROW_CHUNK = 128
POOL_HALO = 16
AUG_LANE = 64
VMEM_LIMIT = 56 * 1024 * 1024


def _diff_lambda_init(layer):
    return 0.8 - 0.6 * math.exp(-0.3 * layer)


def _cparams(n_axes):
    return pltpu.CompilerParams(dimension_semantics=("arbitrary",) * n_axes,
                                vmem_limit_bytes=VMEM_LIMIT)


def _full(shape):
    return pl.BlockSpec(shape, lambda *_: (0,) * len(shape))


def _rms_rows(x, gain):
    ms = jnp.mean(x * x, axis=-1, keepdims=True)
    return x * lax.rsqrt(ms + NORM_EPS) * gain


def _bdot(a, b):
    return jnp.dot(a, b, preferred_element_type=F32)


def _nt_dot(a, b):
    return lax.dot_general(a, b, (((1,), (1,)), ((), ())), preferred_element_type=F32)


def _split_head_pair(q):
    lane = lax.broadcasted_iota(jnp.int32, q.shape, 1)
    qf = q.astype(F32)
    return (jnp.where(lane < HEAD_DIM, qf, 0.0).astype(q.dtype),
            jnp.where(lane >= HEAD_DIM, qf, 0.0).astype(q.dtype))


def _split3(v):
    a1 = v.astype(BF16)
    r1 = v - a1.astype(F32)
    a2 = r1.astype(BF16)
    r2 = r1 - a2.astype(F32)
    a3 = r2.astype(BF16)
    return a1, a2, a3


def _rope_table_kernel(inv_ref, cos_ref, sin_ref):
    rows = cos_ref.shape[0]
    base = pl.program_id(0) * rows
    pos = (base + lax.broadcasted_iota(jnp.int32, (rows, LANES), 0)).astype(F32)
    ang = pos * inv_ref[...]
    lane = lax.broadcasted_iota(jnp.int32, (rows, LANES), 1)
    first_half = (lane & (HEAD_DIM // 2)) == 0
    s = jnp.sin(ang)
    cos_ref[...] = jnp.cos(ang)
    sin_ref[...] = jnp.where(first_half, -s, s)


def _rope_tables(seq):
    half = HEAD_DIM // 2
    inv_freq = ROPE_THETA ** (-jnp.arange(half, dtype=F32) / half)
    inv = jnp.tile(inv_freq, LANES // half)[None, :]
    rows = min(seq, ROW_TILE)
    return pl.pallas_call(
        _rope_table_kernel,
        grid=(seq // rows,),
        in_specs=[_full((1, LANES))],
        out_specs=[pl.BlockSpec((rows, LANES), lambda i: (i, 0))] * 2,
        out_shape=[jax.ShapeDtypeStruct((seq, LANES), F32)] * 2,
        compiler_params=_cparams(1),
        name="rope_tables",
    )(inv)


def _proj_kernel(x_ref, an_ref, w_ref, bf_ref, cos_ref, sin_ref, gains_ref, gmat_ref,
                 ltri_ref, sel_ref,
                 u_ref, dq_ref, dk_ref, dv_ref, sq_ref, sk_ref, sv_ref, fq_ref, fk_ref, fv_ref,
                 carry_ref):
    tm = x_ref.shape[0]
    h = _rms_rows(x_ref[...], an_ref[...]).astype(BF16)

    def proj(lo, width):
        return _nt_dot(h, w_ref[lo:lo + width, :])

    def group_norm(y, gain):
        ms = _bdot((y * y).astype(BF16), gmat_ref[...])
        return y * lax.rsqrt(ms + NORM_EPS) * gain

    cos = cos_ref[...]
    sin = sin_ref[...]
    lane = lax.broadcasted_iota(jnp.int32, (tm, LANES), 1)
    first_half = (lane & (HEAD_DIM // 2)) == 0
    low_head = lane < HEAD_DIM

    def rope(y):
        partner = jnp.where(first_half, pltpu.roll(y, LANES - HEAD_DIM // 2, 1),
                            pltpu.roll(y, HEAD_DIM // 2, 1))
        return y * cos + partner * sin

    u_ref[...] = proj(0, 256)

    gq = gains_ref[0:1, :]
    gk = gains_ref[1:2, :]
    for c in range(2):
        yq = group_norm(proj(256 + 256 * c, 256), gq)
        yk = group_norm(proj(768 + 256 * c, 256), gk)
        for hh in range(2):
            sl = slice(hh * LANES, (hh + 1) * LANES)
            col = slice(256 * c + hh * LANES, 256 * c + (hh + 1) * LANES)
            dq_ref[:, col] = (rope(yq[:, sl]) * Q_SCALE).astype(BF16)
            dk_ref[:, col] = rope(yk[:, sl]).astype(BF16)
        dv_ref[:, 256 * c:256 * (c + 1)] = proj(1280 + 256 * c, 256).astype(BF16)

    sq_ref[...] = (proj(1792, 256) * Q_SCALE).astype(BF16)
    sk_ref[...] = proj(2048, 256).astype(BF16)
    sv_ref[...] = proj(2304, 256).astype(BF16)
    fv_ref[...] = proj(3072, 256).astype(BF16)

    fl = proj(MAIN_WIDTH, LANES) + bf_ref[...]
    logf = jnp.minimum(fl, 0.0) - jnp.log1p(jnp.exp(-jnp.abs(fl)))

    @pl.when(pl.program_id(1) == 0)
    def _():
        carry_ref[...] = jnp.zeros_like(carry_ref)

    carry = carry_ref[0:1, :]
    sub = ltri_ref.shape[0]
    pieces = []
    for sb in range(tm // sub):
        a1, a2, a3 = _split3(logf[sb * sub:(sb + 1) * sub, :])
        ltri = ltri_ref[...]
        cum = _bdot(ltri, a1) + _bdot(ltri, a2) + _bdot(ltri, a3) + carry
        carry = cum[sub - 1:sub, :]
        pieces.append(cum)
    carry_ref[0:1, :] = carry
    cum = jnp.concatenate(pieces, axis=0) if len(pieces) > 1 else pieces[0]

    c1, c2, c3 = (c.astype(F32) for c in _split3(cum * LOG2E))
    packed = jnp.where(lane < 4, c1, jnp.where(lane < 8, c2, jnp.where(lane < 12, c3,
                       jnp.where(lane == 12, 1.0, 0.0))))
    aug = _bdot(packed.astype(BF16), sel_ref[...])

    yq = group_norm(proj(2560, 256), gains_ref[2:3, :]) * Q_SCALE
    yk = group_norm(proj(2816, 256), gains_ref[3:4, :])
    for src, dst, off in ((yq, fq_ref, 0), (yk, fk_ref, 4 * LANES)):
        for pair in range(2):
            v = src[:, pair * LANES:(pair + 1) * LANES]
            heads = (v, pltpu.roll(v, HEAD_DIM, 1))
            for hh in range(2):
                col = (2 * pair + hh) * LANES
                dst[:, col:col + LANES] = jnp.where(
                    low_head, heads[hh], aug[:, off + col:off + col + LANES]).astype(BF16)


def _proj_call(xf, an, w_all, bf_pad, cos, sin, gains, gmat, ltri, sel, batch, seq):
    n, d = xf.shape
    tm = min(seq, ROW_TILE)
    ns = seq // tm
    row = lambda b, s: (b * ns + s, 0)
    widths = (256, 512, 512, 512, 256, 256, 256, 512, 512, 256)
    dtypes = (F32,) + (BF16,) * 9
    return pl.pallas_call(
        _proj_kernel,
        grid=(batch, ns),
        in_specs=[pl.BlockSpec((tm, d), row), _full(an.shape), _full(w_all.shape), _full(bf_pad.shape),
                  pl.BlockSpec((tm, LANES), lambda b, s: (s, 0)),
                  pl.BlockSpec((tm, LANES), lambda b, s: (s, 0)),
                  _full(gains.shape), _full(gmat.shape), _full(ltri.shape), _full(sel.shape)],
        out_specs=[pl.BlockSpec((tm, w), row) for w in widths],
        out_shape=[jax.ShapeDtypeStruct((n, w), t) for w, t in zip(widths, dtypes)],
        scratch_shapes=[pltpu.VMEM((8, LANES), F32)],
        compiler_params=_cparams(2),
        name="norm_in_proj",
    )(xf, an, w_all, bf_pad, cos, sin, gains, gmat, ltri, sel)


def _key_minus_query(row0, rows, tq, tk):
    r = (row0 + lax.broadcasted_iota(jnp.int32, (rows, tk), 0)) & (tq - 1)
    c = lax.broadcasted_iota(jnp.int32, (rows, tk), 1)
    return c - r


def _softmax_sweep(streams, k_ref, v_ref, scratch, i, tq, tk):
    s_scr, p_scr, m_scr, l_scr, acc_scr = scratch
    total = tq * len(streams)
    n_lane_chunks = tk // LANES
    m_scr[...] = jnp.full(m_scr.shape, -1e30, F32)
    l_scr[...] = jnp.zeros(l_scr.shape, F32)
    acc_scr[...] = jnp.zeros(acc_scr.shape, F32)

    def step(j, limit):
        start = pl.multiple_of(j * tk, tk)
        for n, (q, cols, _) in enumerate(streams):
            s = _nt_dot(q, k_ref[pl.ds(start, tk), cols])
            if limit is not None:
                s = jnp.where(_key_minus_query(0, tq, tq, tk) <= limit, s, -jnp.inf)
            s_scr[n * tq:(n + 1) * tq, :] = s
        for n, (_, _, vcols) in enumerate(streams):
            for rc in range(n * tq // ROW_CHUNK, (n + 1) * tq // ROW_CHUNK):
                rows = slice(rc * ROW_CHUNK, (rc + 1) * ROW_CHUNK)
                mx = s_scr[rows, 0:LANES]
                for c in range(1, n_lane_chunks):
                    mx = jnp.maximum(mx, s_scr[rows, c * LANES:(c + 1) * LANES])
                m_old = m_scr[rows, :]
                m_new = jnp.maximum(m_old, mx.max(axis=1, keepdims=True))
                alpha = jnp.exp2(m_old - m_new)
                m_scr[rows, :] = m_new
                psum = None
                for c in range(n_lane_chunks):
                    p = jnp.exp2(s_scr[rows, c * LANES:(c + 1) * LANES] - m_new)
                    psum = p if psum is None else psum + p
                    p_scr[rows, c * LANES:(c + 1) * LANES] = p.astype(BF16)
                l_scr[rows, :] = alpha * l_scr[rows, :] + psum
                acc_scr[rows, :] = alpha * acc_scr[rows, :]
            acc_scr[n * tq:(n + 1) * tq, :] += _bdot(p_scr[n * tq:(n + 1) * tq, :],
                                                     v_ref[pl.ds(start, tk), vcols])

    ratio = tq // tk

    def body(j, _):
        step(j, None)
        return 0

    lax.fori_loop(0, ratio * i, body, 0)
    for d in range(ratio):
        step(ratio * i + d, -d * tk)
    return acc_scr[...], l_scr[...].sum(axis=1, keepdims=True)


def _softmax_scratch(n_rows, tk, v_width):
    return [pltpu.VMEM((n_rows, tk), F32), pltpu.VMEM((n_rows, tk), BF16),
            pltpu.VMEM((n_rows, LANES), F32), pltpu.VMEM((n_rows, LANES), F32),
            pltpu.VMEM((n_rows, v_width), F32)]


def _diff_kernel(q_ref, k_ref, v_ref, lam_ref, sub_ref, o_ref, *scratch, lambda_init, tk):
    tq = q_ref.shape[0]
    i = pl.program_id(2)
    n_heads = q_ref.shape[1] // LANES
    streams = []
    for hd in range(n_heads):
        cols = slice(hd * LANES, (hd + 1) * LANES)
        for qh in _split_head_pair(q_ref[:, cols]):
            streams.append((qh, cols, cols))
    acc, l = _softmax_sweep(streams, k_ref, v_ref, scratch, i, tq, tk)
    o = acc / l
    lp = lam_ref[...]
    lam = (jnp.exp(jnp.sum(lp[0:1, :] * lp[1:2, :], axis=1, keepdims=True))
           - jnp.exp(jnp.sum(lp[2:3, :] * lp[3:4, :], axis=1, keepdims=True)) + lambda_init)
    for hd in range(n_heads):
        r0 = 2 * hd * tq
        od = o[r0:r0 + tq, :] - lam * o[r0 + tq:r0 + 2 * tq, :]
        o_ref[:, hd * LANES:(hd + 1) * LANES] = (
            _rms_rows(od, sub_ref[...]) * (1.0 - lambda_init)).astype(o_ref.dtype)


def _fox_kernel(q_ref, k_ref, v_ref, o_ref, *scratch, tk):
    tq = q_ref.shape[0]
    i = pl.program_id(2)
    n_heads = q_ref.shape[1] // LANES
    streams = [(q_ref[:, hd * LANES:(hd + 1) * LANES], slice(hd * LANES, (hd + 1) * LANES),
                slice((hd // 2) * LANES, (hd // 2 + 1) * LANES)) for hd in range(n_heads)]
    acc, l = _softmax_sweep(streams, k_ref, v_ref, scratch, i, tq, tk)
    o = acc / l
    lane = lax.broadcasted_iota(jnp.int32, (tq, LANES), 1)
    for pair in range(n_heads // 2):
        r0 = 2 * pair * tq
        o_ref[:, pair * LANES:(pair + 1) * LANES] = jnp.where(
            lane < HEAD_DIM, o[r0:r0 + tq, :], o[r0 + tq:r0 + 2 * tq, :]).astype(o_ref.dtype)


def _sb_kernel(q_ref, k_ref, v_ref, o_ref, z_scr, sp_scr, c_scr, w_scr, later_scr, acc_scr, *, tk):
    tq = q_ref.shape[0]
    i = pl.program_id(2)
    ratio = tq // tk
    n_pairs = q_ref.shape[1] // LANES
    heads = []
    for pair in range(n_pairs):
        kv_cols = slice(pair * LANES, (pair + 1) * LANES)
        heads += [(qh, kv_cols) for qh in _split_head_pair(q_ref[:, kv_cols])]
    sub = min(SB_SUM_BLOCK, tk)
    n_sub = tk // sub
    lanes_per_sub = sub // LANES
    neg_suffix = jnp.where(lax.broadcasted_iota(jnp.int32, (sub, sub), 0)
                           >= lax.broadcasted_iota(jnp.int32, (sub, sub), 1), -1.0, 0.0).astype(BF16)
    later_scr[...] = jnp.zeros(later_scr.shape, F32)
    acc_scr[...] = jnp.zeros(acc_scr.shape, F32)

    def step(j, limit):
        start = pl.multiple_of(j * tk, tk)
        for n, (qh, kv_cols) in enumerate(heads):
            z_scr[n * tq:(n + 1) * tq, :] = _nt_dot(qh, k_ref[pl.ds(start, tk), kv_cols])
        for n, (_, kv_cols) in enumerate(heads):
            head_rows = slice(n * tq, (n + 1) * tq)
            chunks = range(n * tq // ROW_CHUNK, (n + 1) * tq // ROW_CHUNK)
            for rc in chunks:
                rows = slice(rc * ROW_CHUNK, (rc + 1) * ROW_CHUNK)
                z = z_scr[rows, :]
                sp = jnp.where(z < SOFTPLUS_LINEAR, jnp.log2(1.0 + jnp.exp2(z)), z)
                if limit is not None:
                    sp = jnp.where(_key_minus_query(rc * ROW_CHUNK, ROW_CHUNK, tq, tk) < limit, sp, 0.0)
                sp_scr[rows, :] = sp.astype(BF16)
            for b in range(n_sub):
                c_scr[head_rows, b * sub:(b + 1) * sub] = _bdot(
                    sp_scr[head_rows, b * sub:(b + 1) * sub], neg_suffix)
            for rc in chunks:
                rows = slice(rc * ROW_CHUNK, (rc + 1) * ROW_CHUNK)
                later = later_scr[rows, :]
                for b in reversed(range(n_sub)):
                    for c in range(b * lanes_per_sub, (b + 1) * lanes_per_sub):
                        cols = slice(c * LANES, (c + 1) * LANES)
                        w = jnp.exp2(z_scr[rows, cols] + c_scr[rows, cols] + later)
                        if limit is not None:
                            keep = _key_minus_query(rc * ROW_CHUNK, ROW_CHUNK, tq, tk)[:, cols] < limit
                            w = jnp.where(keep, w, 0.0)
                        w_scr[rows, cols] = w.astype(BF16)
                    later = later + c_scr[rows, b * sub:b * sub + 1]
                later_scr[rows, :] = later
            acc_scr[head_rows, :] += _bdot(w_scr[head_rows, :], v_ref[pl.ds(start, tk), kv_cols])

    for d in reversed(range(ratio)):
        step(ratio * i + d, -d * tk)

    def body(n, _):
        step(ratio * i - 1 - n, None)
        return 0

    lax.fori_loop(0, ratio * i, body, 0)
    lane = lax.broadcasted_iota(jnp.int32, (tq, LANES), 1)
    for pair in range(n_pairs):
        r0 = 2 * pair * tq
        o_ref[:, pair * LANES:(pair + 1) * LANES] = jnp.where(
            lane < HEAD_DIM, acc_scr[r0:r0 + tq, :], acc_scr[r0 + tq:r0 + 2 * tq, :]).astype(o_ref.dtype)


def _sb_scratch(n_rows, tk):
    return [pltpu.VMEM((n_rows, tk), F32), pltpu.VMEM((n_rows, tk), BF16),
            pltpu.VMEM((n_rows, tk), F32), pltpu.VMEM((n_rows, tk), BF16),
            pltpu.VMEM((n_rows, LANES), F32), pltpu.VMEM((n_rows, LANES), F32)]


def _attention_call(body, name, q, k, v, extra, batch, seq, n_groups, tk, scratch_fn):
    n = q.shape[0]
    tq = min(seq, ATTN_Q_BLOCK)
    tk = min(tk, tq)
    nq = seq // tq
    q_width, v_width = q.shape[1] // n_groups, v.shape[1] // n_groups
    return pl.pallas_call(
        functools.partial(body, tk=tk),
        grid=(batch, n_groups, nq),
        in_specs=[pl.BlockSpec((tq, q_width), lambda b, g, i: (b * nq + i, g)),
                  pl.BlockSpec((seq, q_width), lambda b, g, i: (b, g), pipeline_mode=pl.Buffered(1)),
                  pl.BlockSpec((seq, v_width), lambda b, g, i: (b, g), pipeline_mode=pl.Buffered(1))]
                 + [_full(e.shape) for e in extra],
        out_specs=pl.BlockSpec((tq, v_width), lambda b, g, i: (b * nq + i, g)),
        out_shape=jax.ShapeDtypeStruct((n, v.shape[1]), BF16),
        scratch_shapes=scratch_fn(2 * tq * v_width // LANES, tk),
        compiler_params=_cparams(3),
        name=name,
    )(q, k, v, *extra)


def _merge_kernel(x_ref, u_ref, halo_ref, od_ref, os_ref, of_ref, an_ref, pw_ref, ps_ref,
                  wbr_ref, wg_ref, bg_ref, wout_ref, o_ref):
    tm, d = x_ref.shape
    x = x_ref[...]
    h = _rms_rows(x, an_ref[...]).astype(BF16)

    u = u_ref[...]
    first_tile = pl.program_id(1) == 0
    halo = jnp.where(first_tile, 0.0, halo_ref[...])
    ext = jnp.concatenate([halo, u], axis=0)
    lane = lax.broadcasted_iota(jnp.int32, (tm, POOL_WIDTH), 1)
    group = lane >> 6
    win = jnp.zeros_like(u)
    width = jnp.zeros((tm, POOL_WIDTH), jnp.int32)
    span = 1
    for g, w in enumerate(POOL_WINDOWS):
        while span < w:
            ext = ext + pltpu.roll(ext, span, 0)
            span *= 2
        win = jnp.where(group == g, ext[POOL_HALO:, :], win)
        width = jnp.where(group == g, w, width)
    pos = pl.program_id(1) * tm + lax.broadcasted_iota(jnp.int32, (tm, POOL_WIDTH), 0)
    count = jnp.minimum(pos + 1, width).astype(F32)
    pooled = win / count - u
    o_pool = (_bdot(pooled.astype(BF16), pw_ref[...]) * ps_ref[...]).astype(BF16)

    merged = jnp.zeros((tm, d), F32)
    lo = 0
    for nb, o in enumerate((o_pool, od_ref[...], os_ref[...], of_ref[...])):
        width_n = BRANCH_WIDTHS[nb]
        y = _bdot(o, wbr_ref[lo:lo + width_n, :])
        gate = jax.nn.sigmoid(_nt_dot(h, wg_ref[nb * d:(nb + 1) * d, :]) + bg_ref[nb:nb + 1, :])
        merged = merged + gate * y
        lo += width_n
    o_ref[...] = x + _bdot(merged.astype(BF16), wout_ref[...])


def _merge_call(xf, u, od, os_, of, an, pw, ps, wbr, wg, bg, wout, batch, seq):
    n, d = xf.shape
    tm = min(seq, ROW_TILE)
    ns = seq // tm
    row = lambda b, s: (b * ns + s, 0)
    halo_blocks = tm // POOL_HALO
    halo = lambda b, s: (jnp.maximum((b * ns + s) * halo_blocks - 1, 0), 0)
    return pl.pallas_call(
        _merge_kernel,
        grid=(batch, ns),
        in_specs=[pl.BlockSpec((tm, d), row), pl.BlockSpec((tm, POOL_WIDTH), row),
                  pl.BlockSpec((POOL_HALO, POOL_WIDTH), halo),
                  pl.BlockSpec((tm, DIFF_WIDTH), row), pl.BlockSpec((tm, SB_WIDTH), row),
                  pl.BlockSpec((tm, FOX_WIDTH), row),
                  _full(an.shape), _full(pw.shape), _full(ps.shape), _full(wbr.shape),
                  _full(wg.shape), _full(bg.shape), _full(wout.shape)],
        out_specs=pl.BlockSpec((tm, d), row),
        out_shape=jax.ShapeDtypeStruct((n, d), F32),
        compiler_params=_cparams(2),
        name="merge_out_proj",
    )(xf, u, u, od, os_, of, an, pw, ps, wbr, wg, bg, wout)


def _ffn_kernel(x_ref, fn_ref, wup_ref, wdn_ref, o_ref, *, chunk):
    x = x_ref[...]
    h = _rms_rows(x, fn_ref[...]).astype(BF16)
    d_ff = wdn_ref.shape[0]
    acc = x
    for c in range(d_ff // chunk):
        gate = _bdot(h, wup_ref[:, c * chunk:(c + 1) * chunk])
        up = _bdot(h, wup_ref[:, d_ff + c * chunk:d_ff + (c + 1) * chunk])
        act = (gate * jax.nn.sigmoid(gate) * up).astype(BF16)
        acc = acc + _bdot(act, wdn_ref[c * chunk:(c + 1) * chunk, :])
    o_ref[...] = acc


def _ffn_call(xf, fn, wup, wdn):
    n, d = xf.shape
    tm = min(n, ROW_TILE)
    return pl.pallas_call(
        functools.partial(_ffn_kernel, chunk=256),
        grid=(n // tm,),
        in_specs=[pl.BlockSpec((tm, d), lambda i: (i, 0)), _full(fn.shape), _full(wup.shape),
                  _full(wdn.shape)],
        out_specs=pl.BlockSpec((tm, d), lambda i: (i, 0)),
        out_shape=jax.ShapeDtypeStruct((n, d), F32),
        compiler_params=_cparams(1),
        name="swiglu_ffn",
    )(xf, fn, wup, wdn)


def _group_mean_matrix():
    g = np.kron(np.eye(256 // HEAD_DIM), np.full((HEAD_DIM, HEAD_DIM), 1.0 / HEAD_DIM))
    return jnp.asarray(g, BF16)


def _lower_tri(nrows):
    return jnp.asarray(np.tril(np.ones((nrows, nrows))), BF16)


def _decay_selector():
    sel = np.zeros((LANES, 2 * FOX_HEADS * LANES), np.float32)
    for hd in range(FOX_HEADS):
        qb = hd * LANES + AUG_LANE
        kb = (FOX_HEADS + hd) * LANES + AUG_LANE
        for term in range(3):
            sel[4 * term + hd, qb + term] = 1.0
            sel[12, qb + 3 + term] = 1.0
            sel[12, kb + term] = 1.0
            sel[4 * term + hd, kb + 3 + term] = -1.0
    return jnp.asarray(sel, BF16)


def kernel(x, attn_norm, ffn_norm, w_in, b_gate, b_forget, pool_w, pool_scale, diff_q_norm,
           diff_k_norm, diff_subln, lam_q1, lam_k1, lam_q2, lam_k2, fox_q_norm, fox_k_norm,
           w_branch, w_out, w_ffn_up, w_ffn_down):
    batch, seq, d = x.shape
    depth = attn_norm.shape[0]
    assert seq % min(seq, ROW_TILE) == 0 and seq % min(seq, ATTN_Q_BLOCK) == 0
    xf = x.reshape(batch * seq, d)
    cos, sin = _rope_tables(seq)
    gmat = _group_mean_matrix()
    ltri = _lower_tri(min(seq, 256))
    sel = _decay_selector()
    softmax_scratch = functools.partial(_softmax_scratch, v_width=LANES)

    for layer in range(depth):
        wt = jnp.swapaxes(w_in[layer], 0, 1)
        wf = wt[MAIN_WIDTH:MAIN_WIDTH + FOX_HEADS]
        wf_pad = jnp.concatenate([wf, wf, wf, jnp.zeros((LANES - 3 * FOX_HEADS, d), wt.dtype)], axis=0)
        w_all = jnp.concatenate([wt[:MAIN_WIDTH], wf_pad], axis=0).astype(BF16)
        bfg = b_forget[layer]
        bf_pad = jnp.concatenate([bfg, bfg, bfg, jnp.zeros((LANES - 3 * FOX_HEADS,), F32)])[None, :]
        gains = jnp.stack([jnp.tile(g[layer], 256 // HEAD_DIM)
                           for g in (diff_q_norm, diff_k_norm, fox_q_norm, fox_k_norm)])
        an = attn_norm[layer][None, :]

        u, dq, dk, dv, sq, sk, sv, fq, fk, fv = _proj_call(
            xf, an, w_all, bf_pad, cos, sin, gains, gmat, ltri, sel, batch, seq)

        lam = jnp.stack([lam_q1[layer], lam_k1[layer], lam_q2[layer], lam_k2[layer]])
        od = _attention_call(
            functools.partial(_diff_kernel, lambda_init=_diff_lambda_init(layer)), "diff_attention",
            dq, dk, dv, (lam, diff_subln[layer][None, :]), batch, seq, DIFF_GROUPS,
            SOFTMAX_K_BLOCK, softmax_scratch)
        os_ = _attention_call(_sb_kernel, "stick_breaking_attention", sq, sk, sv, (), batch, seq,
                              1, SB_K_BLOCK, _sb_scratch)
        of = _attention_call(_fox_kernel, "forgetting_attention", fq, fk, fv, (), batch, seq,
                             1, SOFTMAX_K_BLOCK, softmax_scratch)

        pw = jax.scipy.linalg.block_diag(*[pool_w[layer, g] for g in range(len(POOL_WINDOWS))])
        x1 = _merge_call(xf, u, od, os_, of, an, pw.astype(BF16), pool_scale[layer][None, :],
                         w_branch[layer].astype(BF16), wt[MAIN_WIDTH + FOX_HEADS:].astype(BF16),
                         b_gate[layer], w_out[layer].astype(BF16), batch, seq)
        xf = _ffn_call(x1, ffn_norm[layer][None, :], w_ffn_up[layer].astype(BF16),
                       w_ffn_down[layer].astype(BF16))
    return xf.reshape(batch, seq, d)
```

```python
import functools
import math

import numpy as np
import jax
import jax.numpy as jnp
from jax import lax
from jax.experimental import pallas as pl
from jax.experimental.pallas import tpu as pltpu

F32 = jnp.float32
BF16 = jnp.bfloat16

HEAD_DIM = 64
POOL_WINDOWS = (2, 4, 8, 16)
POOL_WIDTH = 256
DIFF_HEADS = 4
DIFF_QK_WIDTH = 512
DIFF_WIDTH = 512
SB_WIDTH = 256
FOX_HEADS = 4
FOX_WIDTH = 256
N_BRANCHES = 4
BRANCH_WIDTHS = (POOL_WIDTH, DIFF_WIDTH, SB_WIDTH, FOX_WIDTH)
MAIN_WIDTH = 3328
ROPE_THETA = 10000.0
NORM_EPS = 1e-6
QK_SCALE = HEAD_DIM ** -0.5
LOG2E = math.log2(math.e)
Q_SCALE = QK_SCALE * LOG2E

LANES = 128
ATTN_Q_BLOCK = 512
SOFTMAX_K_BLOCK = 512
SB_K_BLOCK = 512
SB_SUM_BLOCK = 256
DIFF_GROUPS = 1
ROW_TILE = 512
ROW_CHUNK = 128
POOL_HALO = 16
AUG_LANE = 64
VMEM_LIMIT = 56 * 1024 * 1024


def _diff_lambda_init(layer):
    return 0.8 - 0.6 * math.exp(-0.3 * layer)


def _cparams(n_axes):
    return pltpu.CompilerParams(dimension_semantics=("arbitrary",) * n_axes,
                                vmem_limit_bytes=VMEM_LIMIT)


def _full(shape):
    return pl.BlockSpec(shape, lambda *_: (0,) * len(shape))


def _rms_rows(x, gain):
    ms = jnp.mean(x * x, axis=-1, keepdims=True)
    return x * lax.rsqrt(ms + NORM_EPS) * gain


def _bdot(a, b):
    return jnp.dot(a, b, preferred_element_type=F32)


def _nt_dot(a, b):
    return lax.dot_general(a, b, (((1,), (1,)), ((), ())), preferred_element_type=F32)


def _split_head_pair(q):
    lane = lax.broadcasted_iota(jnp.int32, q.shape, 1)
    qf = q.astype(F32)
    return (jnp.where(lane < HEAD_DIM, qf, 0.0).astype(q.dtype),
            jnp.where(lane >= HEAD_DIM, qf, 0.0).astype(q.dtype))


def _split3(v):
    a1 = v.astype(BF16)
    r1 = v - a1.astype(F32)
    a2 = r1.astype(BF16)
    r2 = r1 - a2.astype(F32)
    a3 = r2.astype(BF16)
    return a1, a2, a3


def _rope_table_kernel(inv_ref, cos_ref, sin_ref):
    rows = cos_ref.shape[0]
    base = pl.program_id(0) * rows
    pos = (base + lax.broadcasted_iota(jnp.int32, (rows, LANES), 0)).astype(F32)
    ang = pos * inv_ref[...]
    lane = lax.broadcasted_iota(jnp.int32, (rows, LANES), 1)
    first_half = (lane & (HEAD_DIM // 2)) == 0
    s = jnp.sin(ang)
    cos_ref[...] = jnp.cos(ang)
    sin_ref[...] = jnp.where(first_half, -s, s)


def _rope_tables(seq):
    half = HEAD_DIM // 2
    inv_freq = ROPE_THETA ** (-jnp.arange(half, dtype=F32) / half)
    inv = jnp.tile(inv_freq, LANES // half)[None, :]
    rows = min(seq, ROW_TILE)
    return pl.pallas_call(
        _rope_table_kernel,
        grid=(seq // rows,),
        in_specs=[_full((1, LANES))],
        out_specs=[pl.BlockSpec((rows, LANES), lambda i: (i, 0))] * 2,
        out_shape=[jax.ShapeDtypeStruct((seq, LANES), F32)] * 2,
        compiler_params=_cparams(1),
        name="rope_tables",
    )(inv)


def _proj_kernel(x_ref, an_ref, w_ref, bf_ref, cos_ref, sin_ref, gains_ref, gmat_ref,
                 ltri_ref, sel_ref,
                 u_ref, dq_ref, dk_ref, dv_ref, sq_ref, sk_ref, sv_ref, fq_ref, fk_ref, fv_ref,
                 carry_ref):
    tm = x_ref.shape[0]
    h = _rms_rows(x_ref[...], an_ref[...]).astype(BF16)

    def proj(lo, width):
        return _nt_dot(h, w_ref[lo:lo + width, :])

    def group_norm(y, gain):
        ms = _bdot((y * y).astype(BF16), gmat_ref[...])
        return y * lax.rsqrt(ms + NORM_EPS) * gain

    cos = cos_ref[...]
    sin = sin_ref[...]
    lane = lax.broadcasted_iota(jnp.int32, (tm, LANES), 1)
    first_half = (lane & (HEAD_DIM // 2)) == 0
    low_head = lane < HEAD_DIM

    def rope(y):
        partner = jnp.where(first_half, pltpu.roll(y, LANES - HEAD_DIM // 2, 1),
                            pltpu.roll(y, HEAD_DIM // 2, 1))
        return y * cos + partner * sin

    u_ref[...] = proj(0, 256)

    gq = gains_ref[0:1, :]
    gk = gains_ref[1:2, :]
    for c in range(2):
        yq = group_norm(proj(256 + 256 * c, 256), gq)
        yk = group_norm(proj(768 + 256 * c, 256), gk)
        for hh in range(2):
            sl = slice(hh * LANES, (hh + 1) * LANES)
            col = slice(256 * c + hh * LANES, 256 * c + (hh + 1) * LANES)
            dq_ref[:, col] = (rope(yq[:, sl]) * Q_SCALE).astype(BF16)
            dk_ref[:, col] = rope(yk[:, sl]).astype(BF16)
        dv_ref[:, 256 * c:256 * (c + 1)] = proj(1280 + 256 * c, 256).astype(BF16)

    sq_ref[...] = (proj(1792, 256) * Q_SCALE).astype(BF16)
    sk_ref[...] = proj(2048, 256).astype(BF16)
    sv_ref[...] = proj(2304, 256).astype(BF16)
    fv_ref[...] = proj(3072, 256).astype(BF16)

    fl = proj(MAIN_WIDTH, LANES) + bf_ref[...]
    logf = jnp.minimum(fl, 0.0) - jnp.log1p(jnp.exp(-jnp.abs(fl)))

    @pl.when(pl.program_id(1) == 0)
    def _():
        carry_ref[...] = jnp.zeros_like(carry_ref)

    carry = carry_ref[0:1, :]
    sub = ltri_ref.shape[0]
    pieces = []
    for sb in range(tm // sub):
        a1, a2, a3 = _split3(logf[sb * sub:(sb + 1) * sub, :])
        ltri = ltri_ref[...]
        cum = _bdot(ltri, a1) + _bdot(ltri, a2) + _bdot(ltri, a3) + carry
        carry = cum[sub - 1:sub, :]
        pieces.append(cum)
    carry_ref[0:1, :] = carry
    cum = jnp.concatenate(pieces, axis=0) if len(pieces) > 1 else pieces[0]

    c1, c2, c3 = (c.astype(F32) for c in _split3(cum * LOG2E))
    packed = jnp.where(lane < 4, c1, jnp.where(lane < 8, c2, jnp.where(lane < 12, c3,
                       jnp.where(lane == 12, 1.0, 0.0))))
    aug = _bdot(packed.astype(BF16), sel_ref[...])

    yq = group_norm(proj(2560, 256), gains_ref[2:3, :]) * Q_SCALE
    yk = group_norm(proj(2816, 256), gains_ref[3:4, :])
    for src, dst, off in ((yq, fq_ref, 0), (yk, fk_ref, 4 * LANES)):
        for pair in range(2):
            v = src[:, pair * LANES:(pair + 1) * LANES]
            heads = (v, pltpu.roll(v, HEAD_DIM, 1))
            for hh in range(2):
                col = (2 * pair + hh) * LANES
                dst[:, col:col + LANES] = jnp.where(
                    low_head, heads[hh], aug[:, off + col:off + col + LANES]).astype(BF16)


def _proj_call(xf, an, w_all, bf_pad, cos, sin, gains, gmat, ltri, sel, batch, seq):
    n, d = xf.shape
    tm = min(seq, ROW_TILE)
    ns = seq // tm
    row = lambda b, s: (b * ns + s, 0)
    widths = (256, 512, 512, 512, 256, 256, 256, 512, 512, 256)
    dtypes = (F32,) + (BF16,) * 9
    return pl.pallas_call(
        _proj_kernel,
        grid=(batch, ns),
        in_specs=[pl.BlockSpec((tm, d), row), _full(an.shape), _full(w_all.shape), _full(bf_pad.shape),
                  pl.BlockSpec((tm, LANES), lambda b, s: (s, 0)),
                  pl.BlockSpec((tm, LANES), lambda b, s: (s, 0)),
                  _full(gains.shape), _full(gmat.shape), _full(ltri.shape), _full(sel.shape)],
        out_specs=[pl.BlockSpec((tm, w), row) for w in widths],
        out_shape=[jax.ShapeDtypeStruct((n, w), t) for w, t in zip(widths, dtypes)],
        scratch_shapes=[pltpu.VMEM((8, LANES), F32)],
        compiler_params=_cparams(2),
        name="norm_in_proj",
    )(xf, an, w_all, bf_pad, cos, sin, gains, gmat, ltri, sel)


def _key_minus_query(row0, rows, tq, tk):
    r = (row0 + lax.broadcasted_iota(jnp.int32, (rows, tk), 0)) & (tq - 1)
    c = lax.broadcasted_iota(jnp.int32, (rows, tk), 1)
    return c - r


def _softmax_sweep(streams, k_ref, v_ref, scratch, i, tq, tk):
    s_scr, p_scr, m_scr, l_scr, acc_scr = scratch
    total = tq * len(streams)
    n_lane_chunks = tk // LANES
    m_scr[...] = jnp.full(m_scr.shape, -1e30, F32)
    l_scr[...] = jnp.zeros(l_scr.shape, F32)
    acc_scr[...] = jnp.zeros(acc_scr.shape, F32)

    def step(j, limit):
        start = pl.multiple_of(j * tk, tk)
        for n, (q, cols, _) in enumerate(streams):
            s = _nt_dot(q, k_ref[pl.ds(start, tk), cols])
            if limit is not None:
                s = jnp.where(_key_minus_query(0, tq, tq, tk) <= limit, s, -jnp.inf)
            s_scr[n * tq:(n + 1) * tq, :] = s
        for n, (_, _, vcols) in enumerate(streams):
            for rc in range(n * tq // ROW_CHUNK, (n + 1) * tq // ROW_CHUNK):
                rows = slice(rc * ROW_CHUNK, (rc + 1) * ROW_CHUNK)
                mx = s_scr[rows, 0:LANES]
                for c in range(1, n_lane_chunks):
                    mx = jnp.maximum(mx, s_scr[rows, c * LANES:(c + 1) * LANES])
                m_old = m_scr[rows, :]
                m_new = jnp.maximum(m_old, mx.max(axis=1, keepdims=True))
                alpha = jnp.exp2(m_old - m_new)
                m_scr[rows, :] = m_new
                psum = None
                for c in range(n_lane_chunks):
                    p = jnp.exp2(s_scr[rows, c * LANES:(c + 1) * LANES] - m_new)
                    psum = p if psum is None else psum + p
                    p_scr[rows, c * LANES:(c + 1) * LANES] = p.astype(BF16)
                l_scr[rows, :] = alpha * l_scr[rows, :] + psum
                acc_scr[rows, :] = alpha * acc_scr[rows, :]
            acc_scr[n * tq:(n + 1) * tq, :] += _bdot(p_scr[n * tq:(n + 1) * tq, :],
                                                     v_ref[pl.ds(start, tk), vcols])

    ratio = tq // tk

    def body(j, _):
        step(j, None)
        return 0

    lax.fori_loop(0, ratio * i, body, 0)
    for d in range(ratio):
        step(ratio * i + d, -d * tk)
    return acc_scr[...], l_scr[...].sum(axis=1, keepdims=True)


def _softmax_scratch(n_rows, tk, v_width):
    return [pltpu.VMEM((n_rows, tk), F32), pltpu.VMEM((n_rows, tk), BF16),
            pltpu.VMEM((n_rows, LANES), F32), pltpu.VMEM((n_rows, LANES), F32),
            pltpu.VMEM((n_rows, v_width), F32)]


def _diff_kernel(q_ref, k_ref, v_ref, lam_ref, sub_ref, o_ref, *scratch, lambda_init, tk):
    tq = q_ref.shape[0]
    i = pl.program_id(2)
    n_heads = q_ref.shape[1] // LANES
    streams = []
    for hd in range(n_heads):
        cols = slice(hd * LANES, (hd + 1) * LANES)
        for qh in _split_head_pair(q_ref[:, cols]):
            streams.append((qh, cols, cols))
    acc, l = _softmax_sweep(streams, k_ref, v_ref, scratch, i, tq, tk)
    o = acc / l
    lp = lam_ref[...]
    lam = (jnp.exp(jnp.sum(lp[0:1, :] * lp[1:2, :], axis=1, keepdims=True))
           - jnp.exp(jnp.sum(lp[2:3, :] * lp[3:4, :], axis=1, keepdims=True)) + lambda_init)
    for hd in range(n_heads):
        r0 = 2 * hd * tq
        od = o[r0:r0 + tq, :] - lam * o[r0 + tq:r0 + 2 * tq, :]
        o_ref[:, hd * LANES:(hd + 1) * LANES] = (
            _rms_rows(od, sub_ref[...]) * (1.0 - lambda_init)).astype(o_ref.dtype)


def _fox_kernel(q_ref, k_ref, v_ref, o_ref, *scratch, tk):
    tq = q_ref.shape[0]
    i = pl.program_id(2)
    n_heads = q_ref.shape[1] // LANES
    streams = [(q_ref[:, hd * LANES:(hd + 1) * LANES], slice(hd * LANES, (hd + 1) * LANES),
                slice((hd // 2) * LANES, (hd // 2 + 1) * LANES)) for hd in range(n_heads)]
    acc, l = _softmax_sweep(streams, k_ref, v_ref, scratch, i, tq, tk)
    o = acc / l
    lane = lax.broadcasted_iota(jnp.int32, (tq, LANES), 1)
    for pair in range(n_heads // 2):
        r0 = 2 * pair * tq
        o_ref[:, pair * LANES:(pair + 1) * LANES] = jnp.where(
            lane < HEAD_DIM, o[r0:r0 + tq, :], o[r0 + tq:r0 + 2 * tq, :]).astype(o_ref.dtype)


def _sb_kernel(q_ref, k_ref, v_ref, o_ref, z_scr, sp_scr, c_scr, w_scr, later_scr, acc_scr, *, tk):
    tq = q_ref.shape[0]
    i = pl.program_id(2)
    ratio = tq // tk
    n_pairs = q_ref.shape[1] // LANES
    heads = []
    for pair in range(n_pairs):
        kv_cols = slice(pair * LANES, (pair + 1) * LANES)
        heads += [(qh, kv_cols) for qh in _split_head_pair(q_ref[:, kv_cols])]
    sub = min(SB_SUM_BLOCK, tk)
    n_sub = tk // sub
    lanes_per_sub = sub // LANES
    neg_suffix = jnp.where(lax.broadcasted_iota(jnp.int32, (sub, sub), 0)
                           >= lax.broadcasted_iota(jnp.int32, (sub, sub), 1), -1.0, 0.0).astype(BF16)
    later_scr[...] = jnp.zeros(later_scr.shape, F32)
    acc_scr[...] = jnp.zeros(acc_scr.shape, F32)

    def step(j, limit):
        start = pl.multiple_of(j * tk, tk)
        for n, (qh, kv_cols) in enumerate(heads):
            z_scr[n * tq:(n + 1) * tq, :] = _nt_dot(qh, k_ref[pl.ds(start, tk), kv_cols])
        for n, (_, kv_cols) in enumerate(heads):
            head_rows = slice(n * tq, (n + 1) * tq)
            chunks = range(n * tq // ROW_CHUNK, (n + 1) * tq // ROW_CHUNK)
            for rc in chunks:
                rows = slice(rc * ROW_CHUNK, (rc + 1) * ROW_CHUNK)
                z = z_scr[rows, :]
                neg_abs = lax.bitcast_convert_type(
                    lax.bitcast_convert_type(z, jnp.int32) | jnp.int32(-2 ** 31), F32)
                sp = jnp.maximum(z, 0.0) + jnp.log2(1.0 + jnp.exp2(neg_abs))
                if limit is not None:
                    sp = jnp.where(_key_minus_query(rc * ROW_CHUNK, ROW_CHUNK, tq, tk) < limit, sp, 0.0)
                sp_scr[rows, :] = sp.astype(BF16)
            for b in range(n_sub):
                c_scr[head_rows, b * sub:(b + 1) * sub] = _bdot(
                    sp_scr[head_rows, b * sub:(b + 1) * sub], neg_suffix)
            for rc in chunks:
                rows = slice(rc * ROW_CHUNK, (rc + 1) * ROW_CHUNK)
                later = later_scr[rows, :]
                for b in reversed(range(n_sub)):
                    for c in range(b * lanes_per_sub, (b + 1) * lanes_per_sub):
                        cols = slice(c * LANES, (c + 1) * LANES)
                        w = jnp.exp2(z_scr[rows, cols] + c_scr[rows, cols] + later)
                        if limit is not None:
                            keep = _key_minus_query(rc * ROW_CHUNK, ROW_CHUNK, tq, tk)[:, cols] < limit
                            w = jnp.where(keep, w, 0.0)
                        w_scr[rows, cols] = w.astype(BF16)
                    later = later + c_scr[rows, b * sub:b * sub + 1]
                later_scr[rows, :] = later
            acc_scr[head_rows, :] += _bdot(w_scr[head_rows, :], v_ref[pl.ds(start, tk), kv_cols])

    for d in reversed(range(ratio)):
        step(ratio * i + d, -d * tk)

    def body(n, _):
        step(ratio * i - 1 - n, None)
        return 0

    lax.fori_loop(0, ratio * i, body, 0)
    lane = lax.broadcasted_iota(jnp.int32, (tq, LANES), 1)
    for pair in range(n_pairs):
        r0 = 2 * pair * tq
        o_ref[:, pair * LANES:(pair + 1) * LANES] = jnp.where(
            lane < HEAD_DIM, acc_scr[r0:r0 + tq, :], acc_scr[r0 + tq:r0 + 2 * tq, :]).astype(o_ref.dtype)


def _sb_scratch(n_rows, tk):
    return [pltpu.VMEM((n_rows, tk), F32), pltpu.VMEM((n_rows, tk), BF16),
            pltpu.VMEM((n_rows, tk), F32), pltpu.VMEM((n_rows, tk), BF16),
            pltpu.VMEM((n_rows, LANES), F32), pltpu.VMEM((n_rows, LANES), F32)]


def _attention_call(body, name, q, k, v, extra, batch, seq, n_groups, tk, scratch_fn,
                    kv_buffers=2):
    n = q.shape[0]
    tq = min(seq, ATTN_Q_BLOCK)
    tk = min(tk, tq)
    nq = seq // tq
    q_width, v_width = q.shape[1] // n_groups, v.shape[1] // n_groups
    kv_mode = pl.Buffered(kv_buffers) if kv_buffers != 2 else None
    return pl.pallas_call(
        functools.partial(body, tk=tk),
        grid=(batch, n_groups, nq),
        in_specs=[pl.BlockSpec((tq, q_width), lambda b, g, i: (b * nq + i, g)),
                  pl.BlockSpec((seq, q_width), lambda b, g, i: (b, g), pipeline_mode=kv_mode),
                  pl.BlockSpec((seq, v_width), lambda b, g, i: (b, g), pipeline_mode=kv_mode)]
                 + [_full(e.shape) for e in extra],
        out_specs=pl.BlockSpec((tq, v_width), lambda b, g, i: (b * nq + i, g)),
        out_shape=jax.ShapeDtypeStruct((n, v.shape[1]), BF16),
        scratch_shapes=scratch_fn(2 * tq * v_width // LANES, tk),
        compiler_params=_cparams(3),
        name=name,
    )(q, k, v, *extra)


def _merge_kernel(x_ref, u_ref, halo_ref, od_ref, os_ref, of_ref, an_ref, pw_ref, ps_ref,
                  wbr_ref, wg_ref, bg_ref, wout_ref, o_ref):
    tm, d = x_ref.shape
    x = x_ref[...]
    h = _rms_rows(x, an_ref[...]).astype(BF16)

    u = u_ref[...]
    first_tile = pl.program_id(1) == 0
    halo = jnp.where(first_tile, 0.0, halo_ref[...])
    ext = jnp.concatenate([halo, u], axis=0)
    lane = lax.broadcasted_iota(jnp.int32, (tm, POOL_WIDTH), 1)
    group = lane >> 6
    win = jnp.zeros_like(u)
    width = jnp.zeros((tm, POOL_WIDTH), jnp.int32)
    span = 1
    for g, w in enumerate(POOL_WINDOWS):
        while span < w:
            ext = ext + pltpu.roll(ext, span, 0)
            span *= 2
        win = jnp.where(group == g, ext[POOL_HALO:, :], win)
        width = jnp.where(group == g, w, width)
    pos = pl.program_id(1) * tm + lax.broadcasted_iota(jnp.int32, (tm, POOL_WIDTH), 0)
    count = jnp.minimum(pos + 1, width).astype(F32)
    pooled = win / count - u
    o_pool = (_bdot(pooled.astype(BF16), pw_ref[...]) * ps_ref[...]).astype(BF16)

    merged = jnp.zeros((tm, d), F32)
    lo = 0
    for nb, o in enumerate((o_pool, od_ref[...], os_ref[...], of_ref[...])):
        width_n = BRANCH_WIDTHS[nb]
        y = _bdot(o, wbr_ref[lo:lo + width_n, :])
        gate = jax.nn.sigmoid(_nt_dot(h, wg_ref[nb * d:(nb + 1) * d, :]) + bg_ref[nb:nb + 1, :])
        merged = merged + gate * y
        lo += width_n
    o_ref[...] = x + _bdot(merged.astype(BF16), wout_ref[...])


def _merge_call(xf, u, od, os_, of, an, pw, ps, wbr, wg, bg, wout, batch, seq):
    n, d = xf.shape
    tm = min(seq, ROW_TILE)
    ns = seq // tm
    row = lambda b, s: (b * ns + s, 0)
    halo_blocks = tm // POOL_HALO
    halo = lambda b, s: (jnp.maximum((b * ns + s) * halo_blocks - 1, 0), 0)
    return pl.pallas_call(
        _merge_kernel,
        grid=(batch, ns),
        in_specs=[pl.BlockSpec((tm, d), row), pl.BlockSpec((tm, POOL_WIDTH), row),
                  pl.BlockSpec((POOL_HALO, POOL_WIDTH), halo),
                  pl.BlockSpec((tm, DIFF_WIDTH), row), pl.BlockSpec((tm, SB_WIDTH), row),
                  pl.BlockSpec((tm, FOX_WIDTH), row),
                  _full(an.shape), _full(pw.shape), _full(ps.shape), _full(wbr.shape),
                  _full(wg.shape), _full(bg.shape), _full(wout.shape)],
        out_specs=pl.BlockSpec((tm, d), row),
        out_shape=jax.ShapeDtypeStruct((n, d), F32),
        compiler_params=_cparams(2),
        name="merge_out_proj",
    )(xf, u, u, od, os_, of, an, pw, ps, wbr, wg, bg, wout)


def _ffn_kernel(x_ref, fn_ref, wup_ref, wdn_ref, o_ref, *, chunk):
    x = x_ref[...]
    h = _rms_rows(x, fn_ref[...]).astype(BF16)
    d_ff = wdn_ref.shape[0]
    acc = x
    for c in range(d_ff // chunk):
        gate = _bdot(h, wup_ref[:, c * chunk:(c + 1) * chunk])
        up = _bdot(h, wup_ref[:, d_ff + c * chunk:d_ff + (c + 1) * chunk])
        act = (gate * jax.nn.sigmoid(gate) * up).astype(BF16)
        acc = acc + _bdot(act, wdn_ref[c * chunk:(c + 1) * chunk, :])
    o_ref[...] = acc


def _ffn_call(xf, fn, wup, wdn):
    n, d = xf.shape
    tm = min(n, ROW_TILE)
    return pl.pallas_call(
        functools.partial(_ffn_kernel, chunk=256),
        grid=(n // tm,),
        in_specs=[pl.BlockSpec((tm, d), lambda i: (i, 0)), _full(fn.shape), _full(wup.shape),
                  _full(wdn.shape)],
        out_specs=pl.BlockSpec((tm, d), lambda i: (i, 0)),
        out_shape=jax.ShapeDtypeStruct((n, d), F32),
        compiler_params=_cparams(1),
        name="swiglu_ffn",
    )(xf, fn, wup, wdn)


def _group_mean_matrix():
    g = np.kron(np.eye(256 // HEAD_DIM), np.full((HEAD_DIM, HEAD_DIM), 1.0 / HEAD_DIM))
    return jnp.asarray(g, BF16)


def _lower_tri(nrows):
    return jnp.asarray(np.tril(np.ones((nrows, nrows))), BF16)


def _decay_selector():
    sel = np.zeros((LANES, 2 * FOX_HEADS * LANES), np.float32)
    for hd in range(FOX_HEADS):
        qb = hd * LANES + AUG_LANE
        kb = (FOX_HEADS + hd) * LANES + AUG_LANE
        for term in range(3):
            sel[4 * term + hd, qb + term] = 1.0
            sel[12, qb + 3 + term] = 1.0
            sel[12, kb + term] = 1.0
            sel[4 * term + hd, kb + 3 + term] = -1.0
    return jnp.asarray(sel, BF16)


def kernel(x, attn_norm, ffn_norm, w_in, b_gate, b_forget, pool_w, pool_scale, diff_q_norm,
           diff_k_norm, diff_subln, lam_q1, lam_k1, lam_q2, lam_k2, fox_q_norm, fox_k_norm,
           w_branch, w_out, w_ffn_up, w_ffn_down):
    batch, seq, d = x.shape
    depth = attn_norm.shape[0]
    assert seq % min(seq, ROW_TILE) == 0 and seq % min(seq, ATTN_Q_BLOCK) == 0
    xf = x.reshape(batch * seq, d)
    cos, sin = _rope_tables(seq)
    gmat = _group_mean_matrix()
    ltri = _lower_tri(min(seq, 256))
    sel = _decay_selector()
    softmax_scratch = functools.partial(_softmax_scratch, v_width=LANES)

    for layer in range(depth):
        wt = jnp.swapaxes(w_in[layer], 0, 1)
        wf = wt[MAIN_WIDTH:MAIN_WIDTH + FOX_HEADS]
        wf_pad = jnp.concatenate([wf, wf, wf, jnp.zeros((LANES - 3 * FOX_HEADS, d), wt.dtype)], axis=0)
        w_all = jnp.concatenate([wt[:MAIN_WIDTH], wf_pad], axis=0).astype(BF16)
        bfg = b_forget[layer]
        bf_pad = jnp.concatenate([bfg, bfg, bfg, jnp.zeros((LANES - 3 * FOX_HEADS,), F32)])[None, :]
        gains = jnp.stack([jnp.tile(g[layer], 256 // HEAD_DIM)
                           for g in (diff_q_norm, diff_k_norm, fox_q_norm, fox_k_norm)])
        an = attn_norm[layer][None, :]

        u, dq, dk, dv, sq, sk, sv, fq, fk, fv = _proj_call(
            xf, an, w_all, bf_pad, cos, sin, gains, gmat, ltri, sel, batch, seq)

        lam = jnp.stack([lam_q1[layer], lam_k1[layer], lam_q2[layer], lam_k2[layer]])
        od = _attention_call(
            functools.partial(_diff_kernel, lambda_init=_diff_lambda_init(layer)), "diff_attention",
            dq, dk, dv, (lam, diff_subln[layer][None, :]), batch, seq, DIFF_GROUPS,
            SOFTMAX_K_BLOCK, softmax_scratch, kv_buffers=1)
        os_ = _attention_call(_sb_kernel, "stick_breaking_attention", sq, sk, sv, (), batch, seq,
                              1, SB_K_BLOCK, _sb_scratch)
        of = _attention_call(_fox_kernel, "forgetting_attention", fq, fk, fv, (), batch, seq,
                             1, SOFTMAX_K_BLOCK, softmax_scratch)

        pw = jax.scipy.linalg.block_diag(*[pool_w[layer, g] for g in range(len(POOL_WINDOWS))])
        x1 = _merge_call(xf, u, od, os_, of, an, pw.astype(BF16), pool_scale[layer][None, :],
                         w_branch[layer].astype(BF16), wt[MAIN_WIDTH + FOX_HEADS:].astype(BF16),
                         b_gate[layer], w_out[layer].astype(BF16), batch, seq)
        xf = _ffn_call(x1, ffn_norm[layer][None, :], w_ffn_up[layer].astype(BF16),
                       w_ffn_down[layer].astype(BF16))
    return xf.reshape(batch, seq, d)
```

```python
import functools
import math

import numpy as np
import jax
import jax.numpy as jnp
from jax import lax
from jax.experimental import pallas as pl
from jax.experimental.pallas import tpu as pltpu

F32 = jnp.float32
BF16 = jnp.bfloat16

HEAD_DIM = 64
POOL_WINDOWS = (2, 4, 8, 16)
POOL_WIDTH = 256
DIFF_HEADS = 4
DIFF_QK_WIDTH = 512
DIFF_WIDTH = 512
SB_WIDTH = 256
FOX_HEADS = 4
FOX_WIDTH = 256
N_BRANCHES = 4
BRANCH_WIDTHS = (POOL_WIDTH, DIFF_WIDTH, SB_WIDTH, FOX_WIDTH)
MAIN_WIDTH = 3328
ROPE_THETA = 10000.0
NORM_EPS = 1e-6
QK_SCALE = HEAD_DIM ** -0.5
LOG2E = math.log2(math.e)
Q_SCALE = QK_SCALE * LOG2E

LANES = 128
ATTN_Q_BLOCK = 512
SOFTMAX_K_BLOCK = 512
SB_K_BLOCK = 512
SB_SUM_BLOCK = 256
DIFF_GROUPS = 1
ROW_TILE = 512
ROW_CHUNK = 128
POOL_HALO = 16
AUG_LANE = 64
VMEM_LIMIT = 56 * 1024 * 1024


def _diff_lambda_init(layer):
    return 0.8 - 0.6 * math.exp(-0.3 * layer)


def _cparams(n_axes):
    return pltpu.CompilerParams(dimension_semantics=("arbitrary",) * n_axes,
                                vmem_limit_bytes=VMEM_LIMIT)


def _full(shape):
    return pl.BlockSpec(shape, lambda *_: (0,) * len(shape))


def _rms_rows(x, gain):
    ms = jnp.mean(x * x, axis=-1, keepdims=True)
    return x * lax.rsqrt(ms + NORM_EPS) * gain


def _bdot(a, b):
    return jnp.dot(a, b, preferred_element_type=F32)


def _nt_dot(a, b):
    return lax.dot_general(a, b, (((1,), (1,)), ((), ())), preferred_element_type=F32)


def _split_head_pair(q):
    lane = lax.broadcasted_iota(jnp.int32, q.shape, 1)
    qf = q.astype(F32)
    return (jnp.where(lane < HEAD_DIM, qf, 0.0).astype(q.dtype),
            jnp.where(lane >= HEAD_DIM, qf, 0.0).astype(q.dtype))


def _split3(v):
    a1 = v.astype(BF16)
    r1 = v - a1.astype(F32)
    a2 = r1.astype(BF16)
    r2 = r1 - a2.astype(F32)
    a3 = r2.astype(BF16)
    return a1, a2, a3


def _rope_table_kernel(inv_ref, cos_ref, sin_ref):
    rows = cos_ref.shape[0]
    base = pl.program_id(0) * rows
    pos = (base + lax.broadcasted_iota(jnp.int32, (rows, LANES), 0)).astype(F32)
    ang = pos * inv_ref[...]
    lane = lax.broadcasted_iota(jnp.int32, (rows, LANES), 1)
    first_half = (lane & (HEAD_DIM // 2)) == 0
    s = jnp.sin(ang)
    cos_ref[...] = jnp.cos(ang)
    sin_ref[...] = jnp.where(first_half, -s, s)


def _rope_tables(seq):
    half = HEAD_DIM // 2
    inv_freq = ROPE_THETA ** (-jnp.arange(half, dtype=F32) / half)
    inv = jnp.tile(inv_freq, LANES // half)[None, :]
    rows = min(seq, ROW_TILE)
    return pl.pallas_call(
        _rope_table_kernel,
        grid=(seq // rows,),
        in_specs=[_full((1, LANES))],
        out_specs=[pl.BlockSpec((rows, LANES), lambda i: (i, 0))] * 2,
        out_shape=[jax.ShapeDtypeStruct((seq, LANES), F32)] * 2,
        compiler_params=_cparams(1),
        name="rope_tables",
    )(inv)


def _proj_kernel(x_ref, an_ref, w_ref, bf_ref, cos_ref, sin_ref, gains_ref, gmat_ref,
                 ltri_ref, sel_ref,
                 u_ref, dq_ref, dk_ref, dv_ref, sq_ref, sk_ref, sv_ref, fq_ref, fk_ref, fv_ref,
                 carry_ref):
    tm = x_ref.shape[0]
    h = _rms_rows(x_ref[...], an_ref[...]).astype(BF16)

    def proj(lo, width):
        return _nt_dot(h, w_ref[lo:lo + width, :])

    def group_norm(y, gain):
        ms = _bdot((y * y).astype(BF16), gmat_ref[...])
        return y * lax.rsqrt(ms + NORM_EPS) * gain

    cos = cos_ref[...]
    sin = sin_ref[...]
    lane = lax.broadcasted_iota(jnp.int32, (tm, LANES), 1)
    first_half = (lane & (HEAD_DIM // 2)) == 0
    low_head = lane < HEAD_DIM

    def rope(y):
        partner = jnp.where(first_half, pltpu.roll(y, LANES - HEAD_DIM // 2, 1),
                            pltpu.roll(y, HEAD_DIM // 2, 1))
        return y * cos + partner * sin

    u_ref[...] = proj(0, 256)

    gq = gains_ref[0:1, :]
    gk = gains_ref[1:2, :]
    for c in range(2):
        yq = group_norm(proj(256 + 256 * c, 256), gq)
        yk = group_norm(proj(768 + 256 * c, 256), gk)
        for hh in range(2):
            sl = slice(hh * LANES, (hh + 1) * LANES)
            col = slice(256 * c + hh * LANES, 256 * c + (hh + 1) * LANES)
            dq_ref[:, col] = (rope(yq[:, sl]) * Q_SCALE).astype(BF16)
            dk_ref[:, col] = rope(yk[:, sl]).astype(BF16)
        dv_ref[:, 256 * c:256 * (c + 1)] = proj(1280 + 256 * c, 256).astype(BF16)

    sq_ref[...] = (proj(1792, 256) * Q_SCALE).astype(BF16)
    sk_ref[...] = proj(2048, 256).astype(BF16)
    sv_ref[...] = proj(2304, 256).astype(BF16)
    fv_ref[...] = proj(3072, 256).astype(BF16)

    fl = proj(MAIN_WIDTH, LANES) + bf_ref[...]
    logf = jnp.minimum(fl, 0.0) - jnp.log1p(jnp.exp(-jnp.abs(fl)))

    @pl.when(pl.program_id(1) == 0)
    def _():
        carry_ref[...] = jnp.zeros_like(carry_ref)

    carry = carry_ref[0:1, :]
    sub = ltri_ref.shape[0]
    pieces = []
    for sb in range(tm // sub):
        a1, a2, a3 = _split3(logf[sb * sub:(sb + 1) * sub, :])
        ltri = ltri_ref[...]
        cum = _bdot(ltri, a1) + _bdot(ltri, a2) + _bdot(ltri, a3) + carry
        carry = cum[sub - 1:sub, :]
        pieces.append(cum)
    carry_ref[0:1, :] = carry
    cum = jnp.concatenate(pieces, axis=0) if len(pieces) > 1 else pieces[0]

    c1, c2, c3 = (c.astype(F32) for c in _split3(cum * LOG2E))
    packed = jnp.where(lane < 4, c1, jnp.where(lane < 8, c2, jnp.where(lane < 12, c3,
                       jnp.where(lane == 12, 1.0, 0.0))))
    aug = _bdot(packed.astype(BF16), sel_ref[...])

    yq = group_norm(proj(2560, 256), gains_ref[2:3, :]) * Q_SCALE
    yk = group_norm(proj(2816, 256), gains_ref[3:4, :])
    for src, dst, off in ((yq, fq_ref, 0), (yk, fk_ref, 4 * LANES)):
        for pair in range(2):
            v = src[:, pair * LANES:(pair + 1) * LANES]
            heads = (v, pltpu.roll(v, HEAD_DIM, 1))
            for hh in range(2):
                col = (2 * pair + hh) * LANES
                dst[:, col:col + LANES] = jnp.where(
                    low_head, heads[hh], aug[:, off + col:off + col + LANES]).astype(BF16)


def _proj_call(xf, an, w_all, bf_pad, cos, sin, gains, gmat, ltri, sel, batch, seq):
    n, d = xf.shape
    tm = min(seq, ROW_TILE)
    ns = seq // tm
    row = lambda b, s: (b * ns + s, 0)
    widths = (256, 512, 512, 512, 256, 256, 256, 512, 512, 256)
    dtypes = (F32,) + (BF16,) * 9
    return pl.pallas_call(
        _proj_kernel,
        grid=(batch, ns),
        in_specs=[pl.BlockSpec((tm, d), row), _full(an.shape), _full(w_all.shape), _full(bf_pad.shape),
                  pl.BlockSpec((tm, LANES), lambda b, s: (s, 0)),
                  pl.BlockSpec((tm, LANES), lambda b, s: (s, 0)),
                  _full(gains.shape), _full(gmat.shape), _full(ltri.shape), _full(sel.shape)],
        out_specs=[pl.BlockSpec((tm, w), row) for w in widths],
        out_shape=[jax.ShapeDtypeStruct((n, w), t) for w, t in zip(widths, dtypes)],
        scratch_shapes=[pltpu.VMEM((8, LANES), F32)],
        compiler_params=_cparams(2),
        name="norm_in_proj",
    )(xf, an, w_all, bf_pad, cos, sin, gains, gmat, ltri, sel)


def _key_minus_query(row0, rows, tq, tk):
    r = (row0 + lax.broadcasted_iota(jnp.int32, (rows, tk), 0)) & (tq - 1)
    c = lax.broadcasted_iota(jnp.int32, (rows, tk), 1)
    return c - r


def _softmax_sweep(streams, k_ref, v_ref, scratch, i, tq, tk):
    s_scr, p_scr, m_scr, l_scr, acc_scr = scratch
    total = tq * len(streams)
    n_lane_chunks = tk // LANES
    m_scr[...] = jnp.full(m_scr.shape, -1e30, F32)
    l_scr[...] = jnp.zeros(l_scr.shape, F32)
    acc_scr[...] = jnp.zeros(acc_scr.shape, F32)

    def step(j, limit):
        start = pl.multiple_of(j * tk, tk)
        for n, (q, cols, _) in enumerate(streams):
            s = _nt_dot(q, k_ref[pl.ds(start, tk), cols])
            if limit is not None:
                s = jnp.where(_key_minus_query(0, tq, tq, tk) <= limit, s, -jnp.inf)
            s_scr[n * tq:(n + 1) * tq, :] = s
        for n, (_, _, vcols) in enumerate(streams):
            for rc in range(n * tq // ROW_CHUNK, (n + 1) * tq // ROW_CHUNK):
                rows = slice(rc * ROW_CHUNK, (rc + 1) * ROW_CHUNK)
                mx = s_scr[rows, 0:LANES]
                for c in range(1, n_lane_chunks):
                    mx = jnp.maximum(mx, s_scr[rows, c * LANES:(c + 1) * LANES])
                m_old = m_scr[rows, :]
                m_new = jnp.maximum(m_old, mx.max(axis=1, keepdims=True))
                alpha = jnp.exp2(m_old - m_new)
                m_scr[rows, :] = m_new
                psum = None
                for c in range(n_lane_chunks):
                    p = jnp.exp2(s_scr[rows, c * LANES:(c + 1) * LANES] - m_new)
                    psum = p if psum is None else psum + p
                    p_scr[rows, c * LANES:(c + 1) * LANES] = p.astype(BF16)
                l_scr[rows, :] = alpha * l_scr[rows, :] + psum
                acc_scr[rows, :] = alpha * acc_scr[rows, :]
            acc_scr[n * tq:(n + 1) * tq, :] += _bdot(p_scr[n * tq:(n + 1) * tq, :],
                                                     v_ref[pl.ds(start, tk), vcols])

    ratio = tq // tk

    def body(j, _):
        step(j, None)
        return 0

    lax.fori_loop(0, ratio * i, body, 0)
    for d in range(ratio):
        step(ratio * i + d, -d * tk)
    return acc_scr[...], l_scr[...].sum(axis=1, keepdims=True)


def _softmax_scratch(n_rows, tk, v_width):
    return [pltpu.VMEM((n_rows, tk), F32), pltpu.VMEM((n_rows, tk), BF16),
            pltpu.VMEM((n_rows, LANES), F32), pltpu.VMEM((n_rows, LANES), F32),
            pltpu.VMEM((n_rows, v_width), F32)]


def _diff_kernel(q_ref, k_ref, v_ref, lam_ref, sub_ref, o_ref, *scratch, lambda_init, tk):
    tq = q_ref.shape[0]
    i = pl.program_id(2)
    n_heads = q_ref.shape[1] // LANES
    streams = []
    for hd in range(n_heads):
        cols = slice(hd * LANES, (hd + 1) * LANES)
        for qh in _split_head_pair(q_ref[:, cols]):
            streams.append((qh, cols, cols))
    acc, l = _softmax_sweep(streams, k_ref, v_ref, scratch, i, tq, tk)
    o = acc / l
    lp = lam_ref[...]
    lam = (jnp.exp(jnp.sum(lp[0:1, :] * lp[1:2, :], axis=1, keepdims=True))
           - jnp.exp(jnp.sum(lp[2:3, :] * lp[3:4, :], axis=1, keepdims=True)) + lambda_init)
    for hd in range(n_heads):
        r0 = 2 * hd * tq
        od = o[r0:r0 + tq, :] - lam * o[r0 + tq:r0 + 2 * tq, :]
        o_ref[:, hd * LANES:(hd + 1) * LANES] = (
            _rms_rows(od, sub_ref[...]) * (1.0 - lambda_init)).astype(o_ref.dtype)


def _fox_kernel(q_ref, k_ref, v_ref, o_ref, *scratch, tk):
    tq = q_ref.shape[0]
    i = pl.program_id(2)
    n_heads = q_ref.shape[1] // LANES
    streams = [(q_ref[:, hd * LANES:(hd + 1) * LANES], slice(hd * LANES, (hd + 1) * LANES),
                slice((hd // 2) * LANES, (hd // 2 + 1) * LANES)) for hd in range(n_heads)]
    acc, l = _softmax_sweep(streams, k_ref, v_ref, scratch, i, tq, tk)
    o = acc / l
    lane = lax.broadcasted_iota(jnp.int32, (tq, LANES), 1)
    for pair in range(n_heads // 2):
        r0 = 2 * pair * tq
        o_ref[:, pair * LANES:(pair + 1) * LANES] = jnp.where(
            lane < HEAD_DIM, o[r0:r0 + tq, :], o[r0 + tq:r0 + 2 * tq, :]).astype(o_ref.dtype)


def _sb_kernel(q_ref, k_ref, v_ref, o_ref, z_scr, sp_scr, c_scr, w_scr, first_scr, later_scr, acc_scr,
               *, tk):
    tq = q_ref.shape[0]
    i = pl.program_id(2)
    ratio = tq // tk
    n_pairs = q_ref.shape[1] // LANES
    heads = []
    for pair in range(n_pairs):
        kv_cols = slice(pair * LANES, (pair + 1) * LANES)
        heads += [(qh, kv_cols) for qh in _split_head_pair(q_ref[:, kv_cols])]
    sub = min(SB_SUM_BLOCK, tk)
    n_sub = tk // sub
    lanes_per_sub = sub // LANES
    neg_later = jnp.where(lax.broadcasted_iota(jnp.int32, (sub, sub), 0)
                          > lax.broadcasted_iota(jnp.int32, (sub, sub), 1), -1.0, 0.0).astype(BF16)
    later_scr[...] = jnp.zeros(later_scr.shape, F32)
    acc_scr[...] = jnp.zeros(acc_scr.shape, F32)

    def step(j, limit):
        start = pl.multiple_of(j * tk, tk)
        for n, (qh, kv_cols) in enumerate(heads):
            z_scr[n * tq:(n + 1) * tq, :] = _nt_dot(qh, k_ref[pl.ds(start, tk), kv_cols])
        for n, (_, kv_cols) in enumerate(heads):
            head_rows = slice(n * tq, (n + 1) * tq)
            chunks = range(n * tq // ROW_CHUNK, (n + 1) * tq // ROW_CHUNK)
            for rc in chunks:
                rows = slice(rc * ROW_CHUNK, (rc + 1) * ROW_CHUNK)
                z = z_scr[rows, :]
                neg_abs = lax.bitcast_convert_type(
                    lax.bitcast_convert_type(z, jnp.int32) | jnp.int32(-2 ** 31), F32)
                sp = jnp.maximum(z, 0.0) + jnp.log2(1.0 + jnp.exp2(neg_abs))
                if limit is not None:
                    sp = jnp.where(_key_minus_query(rc * ROW_CHUNK, ROW_CHUNK, tq, tk) < limit, sp, 0.0)
                sp_scr[rows, :] = sp.astype(BF16)
                z_scr[rows, :] = z - sp
                for b in range(n_sub):
                    first_scr[rows, b * LANES:(b + 1) * LANES] = sp[:, b * sub:b * sub + LANES]
            for b in range(n_sub):
                c_scr[head_rows, b * sub:(b + 1) * sub] = _bdot(
                    sp_scr[head_rows, b * sub:(b + 1) * sub], neg_later)
            for rc in chunks:
                rows = slice(rc * ROW_CHUNK, (rc + 1) * ROW_CHUNK)
                later = later_scr[rows, :]
                for b in reversed(range(n_sub)):
                    for c in range(b * lanes_per_sub, (b + 1) * lanes_per_sub):
                        cols = slice(c * LANES, (c + 1) * LANES)
                        w = jnp.exp2(z_scr[rows, cols] + c_scr[rows, cols] + later)
                        if limit is not None:
                            keep = _key_minus_query(rc * ROW_CHUNK, ROW_CHUNK, tq, tk)[:, cols] < limit
                            w = jnp.where(keep, w, 0.0)
                        w_scr[rows, cols] = w.astype(BF16)
                    later = later + (c_scr[rows, b * sub:b * sub + 1]
                                     - first_scr[rows, b * LANES:b * LANES + 1])
                later_scr[rows, :] = later
            acc_scr[head_rows, :] += _bdot(w_scr[head_rows, :], v_ref[pl.ds(start, tk), kv_cols])

    for d in reversed(range(ratio)):
        step(ratio * i + d, -d * tk)

    def body(n, _):
        step(ratio * i - 1 - n, None)
        return 0

    lax.fori_loop(0, ratio * i, body, 0)
    lane = lax.broadcasted_iota(jnp.int32, (tq, LANES), 1)
    for pair in range(n_pairs):
        r0 = 2 * pair * tq
        o_ref[:, pair * LANES:(pair + 1) * LANES] = jnp.where(
            lane < HEAD_DIM, acc_scr[r0:r0 + tq, :], acc_scr[r0 + tq:r0 + 2 * tq, :]).astype(o_ref.dtype)


def _sb_scratch(n_rows, tk):
    n_sub = tk // min(SB_SUM_BLOCK, tk)
    return [pltpu.VMEM((n_rows, tk), F32), pltpu.VMEM((n_rows, tk), BF16),
            pltpu.VMEM((n_rows, tk), F32), pltpu.VMEM((n_rows, tk), BF16),
            pltpu.VMEM((n_rows, n_sub * LANES), F32),
            pltpu.VMEM((n_rows, LANES), F32), pltpu.VMEM((n_rows, LANES), F32)]


def _attention_call(body, name, q, k, v, extra, batch, seq, n_groups, tk, scratch_fn,
                    kv_buffers=2):
    n = q.shape[0]
    tq = min(seq, ATTN_Q_BLOCK)
    tk = min(tk, tq)
    nq = seq // tq
    q_width, v_width = q.shape[1] // n_groups, v.shape[1] // n_groups
    kv_mode = pl.Buffered(kv_buffers) if kv_buffers != 2 else None
    return pl.pallas_call(
        functools.partial(body, tk=tk),
        grid=(batch, n_groups, nq),
        in_specs=[pl.BlockSpec((tq, q_width), lambda b, g, i: (b * nq + i, g)),
                  pl.BlockSpec((seq, q_width), lambda b, g, i: (b, g), pipeline_mode=kv_mode),
                  pl.BlockSpec((seq, v_width), lambda b, g, i: (b, g), pipeline_mode=kv_mode)]
                 + [_full(e.shape) for e in extra],
        out_specs=pl.BlockSpec((tq, v_width), lambda b, g, i: (b * nq + i, g)),
        out_shape=jax.ShapeDtypeStruct((n, v.shape[1]), BF16),
        scratch_shapes=scratch_fn(2 * tq * v_width // LANES, tk),
        compiler_params=_cparams(3),
        name=name,
    )(q, k, v, *extra)


def _merge_kernel(x_ref, u_ref, halo_ref, od_ref, os_ref, of_ref, an_ref, pw_ref, ps_ref,
                  wbr_ref, wg_ref, bg_ref, wout_ref, o_ref):
    tm, d = x_ref.shape
    x = x_ref[...]
    h = _rms_rows(x, an_ref[...]).astype(BF16)

    u = u_ref[...]
    first_tile = pl.program_id(1) == 0
    halo = jnp.where(first_tile, 0.0, halo_ref[...])
    ext = jnp.concatenate([halo, u], axis=0)
    lane = lax.broadcasted_iota(jnp.int32, (tm, POOL_WIDTH), 1)
    group = lane >> 6
    win = jnp.zeros_like(u)
    width = jnp.zeros((tm, POOL_WIDTH), jnp.int32)
    span = 1
    for g, w in enumerate(POOL_WINDOWS):
        while span < w:
            ext = ext + pltpu.roll(ext, span, 0)
            span *= 2
        win = jnp.where(group == g, ext[POOL_HALO:, :], win)
        width = jnp.where(group == g, w, width)
    pos = pl.program_id(1) * tm + lax.broadcasted_iota(jnp.int32, (tm, POOL_WIDTH), 0)
    count = jnp.minimum(pos + 1, width).astype(F32)
    pooled = win / count - u
    o_pool = (_bdot(pooled.astype(BF16), pw_ref[...]) * ps_ref[...]).astype(BF16)

    merged = jnp.zeros((tm, d), F32)
    lo = 0
    for nb, o in enumerate((o_pool, od_ref[...], os_ref[...], of_ref[...])):
        width_n = BRANCH_WIDTHS[nb]
        y = _bdot(o, wbr_ref[lo:lo + width_n, :])
        gate = jax.nn.sigmoid(_nt_dot(h, wg_ref[nb * d:(nb + 1) * d, :]) + bg_ref[nb:nb + 1, :])
        merged = merged + gate * y
        lo += width_n
    o_ref[...] = x + _bdot(merged.astype(BF16), wout_ref[...])


def _merge_call(xf, u, od, os_, of, an, pw, ps, wbr, wg, bg, wout, batch, seq):
    n, d = xf.shape
    tm = min(seq, ROW_TILE)
    ns = seq // tm
    row = lambda b, s: (b * ns + s, 0)
    halo_blocks = tm // POOL_HALO
    halo = lambda b, s: (jnp.maximum((b * ns + s) * halo_blocks - 1, 0), 0)
    return pl.pallas_call(
        _merge_kernel,
        grid=(batch, ns),
        in_specs=[pl.BlockSpec((tm, d), row), pl.BlockSpec((tm, POOL_WIDTH), row),
                  pl.BlockSpec((POOL_HALO, POOL_WIDTH), halo),
                  pl.BlockSpec((tm, DIFF_WIDTH), row), pl.BlockSpec((tm, SB_WIDTH), row),
                  pl.BlockSpec((tm, FOX_WIDTH), row),
                  _full(an.shape), _full(pw.shape), _full(ps.shape), _full(wbr.shape),
                  _full(wg.shape), _full(bg.shape), _full(wout.shape)],
        out_specs=pl.BlockSpec((tm, d), row),
        out_shape=jax.ShapeDtypeStruct((n, d), F32),
        compiler_params=_cparams(2),
        name="merge_out_proj",
    )(xf, u, u, od, os_, of, an, pw, ps, wbr, wg, bg, wout)


def _ffn_kernel(x_ref, fn_ref, wup_ref, wdn_ref, o_ref, *, chunk):
    x = x_ref[...]
    h = _rms_rows(x, fn_ref[...]).astype(BF16)
    d_ff = wdn_ref.shape[0]
    acc = x
    for c in range(d_ff // chunk):
        gate = _bdot(h, wup_ref[:, c * chunk:(c + 1) * chunk])
        up = _bdot(h, wup_ref[:, d_ff + c * chunk:d_ff + (c + 1) * chunk])
        act = (gate * jax.nn.sigmoid(gate) * up).astype(BF16)
        acc = acc + _bdot(act, wdn_ref[c * chunk:(c + 1) * chunk, :])
    o_ref[...] = acc


def _ffn_call(xf, fn, wup, wdn):
    n, d = xf.shape
    tm = min(n, ROW_TILE)
    return pl.pallas_call(
        functools.partial(_ffn_kernel, chunk=256),
        grid=(n // tm,),
        in_specs=[pl.BlockSpec((tm, d), lambda i: (i, 0)), _full(fn.shape), _full(wup.shape),
                  _full(wdn.shape)],
        out_specs=pl.BlockSpec((tm, d), lambda i: (i, 0)),
        out_shape=jax.ShapeDtypeStruct((n, d), F32),
        compiler_params=_cparams(1),
        name="swiglu_ffn",
    )(xf, fn, wup, wdn)


def _group_mean_matrix():
    g = np.kron(np.eye(256 // HEAD_DIM), np.full((HEAD_DIM, HEAD_DIM), 1.0 / HEAD_DIM))
    return jnp.asarray(g, BF16)


def _lower_tri(nrows):
    return jnp.asarray(np.tril(np.ones((nrows, nrows))), BF16)


def _decay_selector():
    sel = np.zeros((LANES, 2 * FOX_HEADS * LANES), np.float32)
    for hd in range(FOX_HEADS):
        qb = hd * LANES + AUG_LANE
        kb = (FOX_HEADS + hd) * LANES + AUG_LANE
        for term in range(3):
            sel[4 * term + hd, qb + term] = 1.0
            sel[12, qb + 3 + term] = 1.0
            sel[12, kb + term] = 1.0
            sel[4 * term + hd, kb + 3 + term] = -1.0
    return jnp.asarray(sel, BF16)


def kernel(x, attn_norm, ffn_norm, w_in, b_gate, b_forget, pool_w, pool_scale, diff_q_norm,
           diff_k_norm, diff_subln, lam_q1, lam_k1, lam_q2, lam_k2, fox_q_norm, fox_k_norm,
           w_branch, w_out, w_ffn_up, w_ffn_down):
    batch, seq, d = x.shape
    depth = attn_norm.shape[0]
    assert seq % min(seq, ROW_TILE) == 0 and seq % min(seq, ATTN_Q_BLOCK) == 0
    xf = x.reshape(batch * seq, d)
    cos, sin = _rope_tables(seq)
    gmat = _group_mean_matrix()
    ltri = _lower_tri(min(seq, 256))
    sel = _decay_selector()
    softmax_scratch = functools.partial(_softmax_scratch, v_width=LANES)

    for layer in range(depth):
        wt = jnp.swapaxes(w_in[layer], 0, 1)
        wf = wt[MAIN_WIDTH:MAIN_WIDTH + FOX_HEADS]
        wf_pad = jnp.concatenate([wf, wf, wf, jnp.zeros((LANES - 3 * FOX_HEADS, d), wt.dtype)], axis=0)
        w_all = jnp.concatenate([wt[:MAIN_WIDTH], wf_pad], axis=0).astype(BF16)
        bfg = b_forget[layer]
        bf_pad = jnp.concatenate([bfg, bfg, bfg, jnp.zeros((LANES - 3 * FOX_HEADS,), F32)])[None, :]
        gains = jnp.stack([jnp.tile(g[layer], 256 // HEAD_DIM)
                           for g in (diff_q_norm, diff_k_norm, fox_q_norm, fox_k_norm)])
        an = attn_norm[layer][None, :]

        u, dq, dk, dv, sq, sk, sv, fq, fk, fv = _proj_call(
            xf, an, w_all, bf_pad, cos, sin, gains, gmat, ltri, sel, batch, seq)

        lam = jnp.stack([lam_q1[layer], lam_k1[layer], lam_q2[layer], lam_k2[layer]])
        od = _attention_call(
            functools.partial(_diff_kernel, lambda_init=_diff_lambda_init(layer)), "diff_attention",
            dq, dk, dv, (lam, diff_subln[layer][None, :]), batch, seq, DIFF_GROUPS,
            SOFTMAX_K_BLOCK, softmax_scratch, kv_buffers=1)
        os_ = _attention_call(_sb_kernel, "stick_breaking_attention", sq, sk, sv, (), batch, seq,
                              1, SB_K_BLOCK, _sb_scratch)
        of = _attention_call(_fox_kernel, "forgetting_attention", fq, fk, fv, (), batch, seq,
                             1, SOFTMAX_K_BLOCK, softmax_scratch)

        pw = jax.scipy.linalg.block_diag(*[pool_w[layer, g] for g in range(len(POOL_WINDOWS))])
        x1 = _merge_call(xf, u, od, os_, of, an, pw.astype(BF16), pool_scale[layer][None, :],
                         w_branch[layer].astype(BF16), wt[MAIN_WIDTH + FOX_HEADS:].astype(BF16),
                         b_gate[layer], w_out[layer].astype(BF16), batch, seq)
        xf = _ffn_call(x1, ffn_norm[layer][None, :], w_ffn_up[layer].astype(BF16),
                       w_ffn_down[layer].astype(BF16))
    return xf.reshape(batch, seq, d)
```

```python
import functools
import math

import numpy as np
import jax
import jax.numpy as jnp
from jax import lax
from jax.experimental import pallas as pl
from jax.experimental.pallas import tpu as pltpu

F32 = jnp.float32
BF16 = jnp.bfloat16

HEAD_DIM = 64
POOL_WINDOWS = (2, 4, 8, 16)
POOL_WIDTH = 256
DIFF_WIDTH = 512
SB_WIDTH = 256
FOX_HEADS = 4
FOX_WIDTH = 256
BRANCH_WIDTHS = (POOL_WIDTH, DIFF_WIDTH, SB_WIDTH, FOX_WIDTH)
MAIN_WIDTH = 3328
ROPE_THETA = 10000.0
NORM_EPS = 1e-6
QK_SCALE = HEAD_DIM ** -0.5
LOG2E = math.log2(math.e)
Q_SCALE = QK_SCALE * LOG2E

LANES = 128
ATTN_Q_BLOCK = 512
SOFTMAX_K_BLOCK = 512
SB_K_BLOCK = 512
SB_SUM_BLOCK = 256
DIFF_GROUPS = 1
ROW_TILE = 512
ROW_CHUNK = 128
POOL_HALO = 16
AUG_LANE = 64
VMEM_LIMIT = 56 * 1024 * 1024


def _diff_lambda_init(layer):
    return 0.8 - 0.6 * math.exp(-0.3 * layer)


def _cparams(n_axes):
    return pltpu.CompilerParams(dimension_semantics=("arbitrary",) * n_axes,
                                vmem_limit_bytes=VMEM_LIMIT)


def _full(shape):
    return pl.BlockSpec(shape, lambda *_: (0,) * len(shape))


def _rms_rows(x, gain):
    ms = jnp.mean(x * x, axis=-1, keepdims=True)
    return x * lax.rsqrt(ms + NORM_EPS) * gain


def _bdot(a, b):
    return jnp.dot(a, b, preferred_element_type=F32)


def _nt_dot(a, b):
    return lax.dot_general(a, b, (((1,), (1,)), ((), ())), preferred_element_type=F32)


def _split_head_pair(q):
    lane = lax.broadcasted_iota(jnp.int32, q.shape, 1)
    qf = q.astype(F32)
    return (jnp.where(lane < HEAD_DIM, qf, 0.0).astype(q.dtype),
            jnp.where(lane >= HEAD_DIM, qf, 0.0).astype(q.dtype))


def _split3(v):
    a1 = v.astype(BF16)
    r1 = v - a1.astype(F32)
    a2 = r1.astype(BF16)
    r2 = r1 - a2.astype(F32)
    a3 = r2.astype(BF16)
    return a1, a2, a3


def _rope_table_kernel(inv_ref, cos_ref, sin_ref):
    rows = cos_ref.shape[0]
    base = pl.program_id(0) * rows
    pos = (base + lax.broadcasted_iota(jnp.int32, (rows, LANES), 0)).astype(F32)
    ang = pos * inv_ref[...]
    lane = lax.broadcasted_iota(jnp.int32, (rows, LANES), 1)
    first_half = (lane & (HEAD_DIM // 2)) == 0
    s = jnp.sin(ang)
    cos_ref[...] = jnp.cos(ang)
    sin_ref[...] = jnp.where(first_half, -s, s)


def _rope_tables(seq):
    half = HEAD_DIM // 2
    inv_freq = ROPE_THETA ** (-jnp.arange(half, dtype=F32) / half)
    inv = jnp.tile(inv_freq, LANES // half)[None, :]
    rows = min(seq, ROW_TILE)
    return pl.pallas_call(
        _rope_table_kernel,
        grid=(seq // rows,),
        in_specs=[_full((1, LANES))],
        out_specs=[pl.BlockSpec((rows, LANES), lambda i: (i, 0))] * 2,
        out_shape=[jax.ShapeDtypeStruct((seq, LANES), F32)] * 2,
        compiler_params=_cparams(1),
        name="rope_tables",
    )(inv)


def _proj_kernel(x_ref, an_ref, w_ref, wf_ref, bf_ref, cos_ref, sin_ref, gains_ref, gmat_ref,
                 ltri_ref, sel_ref,
                 u_ref, dq_ref, dk_ref, dv_ref, sq_ref, sk_ref, sv_ref, fq_ref, fk_ref, fv_ref,
                 carry_ref):
    tm = x_ref.shape[0]
    h = _rms_rows(x_ref[...], an_ref[...]).astype(BF16)

    def proj(lo, width):
        return _nt_dot(h, w_ref[lo:lo + width, :])

    def group_norm(y, gain):
        ms = _bdot((y * y).astype(BF16), gmat_ref[...])
        return y * lax.rsqrt(ms + NORM_EPS) * gain

    cos = cos_ref[...]
    sin = sin_ref[...]
    lane = lax.broadcasted_iota(jnp.int32, (tm, LANES), 1)
    first_half = (lane & (HEAD_DIM // 2)) == 0
    low_head = lane < HEAD_DIM

    def rope(y):
        partner = jnp.where(first_half, pltpu.roll(y, LANES - HEAD_DIM // 2, 1),
                            pltpu.roll(y, HEAD_DIM // 2, 1))
        return y * cos + partner * sin

    u_ref[...] = proj(0, 256)

    gq = gains_ref[0:1, :]
    gk = gains_ref[1:2, :]
    for c in range(2):
        yq = group_norm(proj(256 + 256 * c, 256), gq)
        yk = group_norm(proj(768 + 256 * c, 256), gk)
        for hh in range(2):
            sl = slice(hh * LANES, (hh + 1) * LANES)
            col = slice(256 * c + hh * LANES, 256 * c + (hh + 1) * LANES)
            dq_ref[:, col] = (rope(yq[:, sl]) * Q_SCALE).astype(BF16)
            dk_ref[:, col] = rope(yk[:, sl]).astype(BF16)
        dv_ref[:, 256 * c:256 * (c + 1)] = proj(1280 + 256 * c, 256).astype(BF16)

    sq_ref[...] = (proj(1792, 256) * Q_SCALE).astype(BF16)
    sk_ref[...] = proj(2048, 256).astype(BF16)
    sv_ref[...] = proj(2304, 256).astype(BF16)
    fv_ref[...] = proj(3072, 256).astype(BF16)

    fl = _nt_dot(h, wf_ref[...]) + bf_ref[...]
    logf = jnp.minimum(fl, 0.0) - jnp.log1p(jnp.exp(-jnp.abs(fl)))

    @pl.when(pl.program_id(1) == 0)
    def _():
        carry_ref[...] = jnp.zeros_like(carry_ref)

    carry = carry_ref[0:1, :]
    sub = ltri_ref.shape[0]
    pieces = []
    for sb in range(tm // sub):
        a1, a2, a3 = _split3(logf[sb * sub:(sb + 1) * sub, :])
        ltri = ltri_ref[...]
        cum = _bdot(ltri, a1) + _bdot(ltri, a2) + _bdot(ltri, a3) + carry
        carry = cum[sub - 1:sub, :]
        pieces.append(cum)
    carry_ref[0:1, :] = carry
    cum = jnp.concatenate(pieces, axis=0) if len(pieces) > 1 else pieces[0]

    c1, c2, c3 = (c.astype(F32) for c in _split3(cum * LOG2E))
    packed = jnp.where(lane < 4, c1, jnp.where(lane < 8, c2, jnp.where(lane < 12, c3,
                       jnp.where(lane == 12, 1.0, 0.0))))
    aug = _bdot(packed.astype(BF16), sel_ref[...])

    yq = group_norm(proj(2560, 256), gains_ref[2:3, :]) * Q_SCALE
    yk = group_norm(proj(2816, 256), gains_ref[3:4, :])
    for src, dst, off in ((yq, fq_ref, 0), (yk, fk_ref, 4 * LANES)):
        for pair in range(2):
            v = src[:, pair * LANES:(pair + 1) * LANES]
            heads = (v, pltpu.roll(v, HEAD_DIM, 1))
            for hh in range(2):
                col = (2 * pair + hh) * LANES
                dst[:, col:col + LANES] = jnp.where(
                    low_head, heads[hh], aug[:, off + col:off + col + LANES]).astype(BF16)


def _proj_call(xf, an, w_main, wf_pad, bf_pad, cos, sin, gains, gmat, ltri, sel, batch, seq):
    n, d = xf.shape
    tm = min(seq, ROW_TILE)
    ns = seq // tm
    row = lambda b, s: (b * ns + s, 0)
    widths = (256, 512, 512, 512, 256, 256, 256, 512, 512, 256)
    dtypes = (F32,) + (BF16,) * 9
    return pl.pallas_call(
        _proj_kernel,
        grid=(batch, ns),
        in_specs=[pl.BlockSpec((tm, d), row), _full(an.shape), _full(w_main.shape),
                  _full(wf_pad.shape), _full(bf_pad.shape),
                  pl.BlockSpec((tm, LANES), lambda b, s: (s, 0)),
                  pl.BlockSpec((tm, LANES), lambda b, s: (s, 0)),
                  _full(gains.shape), _full(gmat.shape), _full(ltri.shape), _full(sel.shape)],
        out_specs=[pl.BlockSpec((tm, w), row) for w in widths],
        out_shape=[jax.ShapeDtypeStruct((n, w), t) for w, t in zip(widths, dtypes)],
        scratch_shapes=[pltpu.VMEM((8, LANES), F32)],
        compiler_params=_cparams(2),
        name="norm_in_proj",
    )(xf, an, w_main, wf_pad, bf_pad, cos, sin, gains, gmat, ltri, sel)


def _key_minus_query(row0, rows, tq, tk):
    r = (row0 + lax.broadcasted_iota(jnp.int32, (rows, tk), 0)) & (tq - 1)
    c = lax.broadcasted_iota(jnp.int32, (rows, tk), 1)
    return c - r


def _softmax_sweep(streams, k_ref, v_ref, scratch, i, tq, tk):
    s_scr, p_scr, m_scr, l_scr, acc_scr = scratch
    n_lane_chunks = tk // LANES
    m_scr[...] = jnp.full(m_scr.shape, -1e30, F32)
    l_scr[...] = jnp.zeros(l_scr.shape, F32)
    acc_scr[...] = jnp.zeros(acc_scr.shape, F32)

    def step(j, limit):
        start = pl.multiple_of(j * tk, tk)
        for n, (q, cols, _) in enumerate(streams):
            s = _nt_dot(q, k_ref[pl.ds(start, tk), cols])
            if limit is not None:
                s = jnp.where(_key_minus_query(0, tq, tq, tk) <= limit, s, -jnp.inf)
            s_scr[n * tq:(n + 1) * tq, :] = s
        for n, (_, _, vcols) in enumerate(streams):
            for rc in range(n * tq // ROW_CHUNK, (n + 1) * tq // ROW_CHUNK):
                rows = slice(rc * ROW_CHUNK, (rc + 1) * ROW_CHUNK)
                mx = s_scr[rows, 0:LANES]
                for c in range(1, n_lane_chunks):
                    mx = jnp.maximum(mx, s_scr[rows, c * LANES:(c + 1) * LANES])
                m_old = m_scr[rows, :]
                m_new = jnp.maximum(m_old, mx.max(axis=1, keepdims=True))
                alpha = jnp.exp2(m_old - m_new)
                m_scr[rows, :] = m_new
                psum = None
                for c in range(n_lane_chunks):
                    p = jnp.exp2(s_scr[rows, c * LANES:(c + 1) * LANES] - m_new)
                    psum = p if psum is None else psum + p
                    p_scr[rows, c * LANES:(c + 1) * LANES] = p.astype(BF16)
                l_scr[rows, :] = alpha * l_scr[rows, :] + psum
                acc_scr[rows, :] = alpha * acc_scr[rows, :]
            acc_scr[n * tq:(n + 1) * tq, :] += _bdot(p_scr[n * tq:(n + 1) * tq, :],
                                                     v_ref[pl.ds(start, tk), vcols])

    ratio = tq // tk

    def body(j, _):
        step(j, None)
        return 0

    lax.fori_loop(0, ratio * i, body, 0)
    for d in range(ratio):
        step(ratio * i + d, -d * tk)
    return acc_scr[...], l_scr[...].sum(axis=1, keepdims=True)


def _softmax_scratch(n_rows, tk, v_width):
    return [pltpu.VMEM((n_rows, tk), F32), pltpu.VMEM((n_rows, tk), BF16),
            pltpu.VMEM((n_rows, LANES), F32), pltpu.VMEM((n_rows, LANES), F32),
            pltpu.VMEM((n_rows, v_width), F32)]


def _diff_kernel(q_ref, k_ref, v_ref, lam_ref, sub_ref, o_ref, *scratch, lambda_init, tk):
    tq = q_ref.shape[0]
    i = pl.program_id(2)
    n_heads = q_ref.shape[1] // LANES
    streams = []
    for hd in range(n_heads):
        cols = slice(hd * LANES, (hd + 1) * LANES)
        for qh in _split_head_pair(q_ref[:, cols]):
            streams.append((qh, cols, cols))
    acc, l = _softmax_sweep(streams, k_ref, v_ref, scratch, i, tq, tk)
    o = acc / l
    lp = lam_ref[...]
    lam = (jnp.exp(jnp.sum(lp[0:1, :] * lp[1:2, :], axis=1, keepdims=True))
           - jnp.exp(jnp.sum(lp[2:3, :] * lp[3:4, :], axis=1, keepdims=True)) + lambda_init)
    for hd in range(n_heads):
        r0 = 2 * hd * tq
        od = o[r0:r0 + tq, :] - lam * o[r0 + tq:r0 + 2 * tq, :]
        o_ref[:, hd * LANES:(hd + 1) * LANES] = (
            _rms_rows(od, sub_ref[...]) * (1.0 - lambda_init)).astype(o_ref.dtype)


def _fox_kernel(q_ref, k_ref, v_ref, o_ref, *scratch, tk):
    tq = q_ref.shape[0]
    i = pl.program_id(2)
    n_heads = q_ref.shape[1] // LANES
    streams = [(q_ref[:, hd * LANES:(hd + 1) * LANES], slice(hd * LANES, (hd + 1) * LANES),
                slice((hd // 2) * LANES, (hd // 2 + 1) * LANES)) for hd in range(n_heads)]
    acc, l = _softmax_sweep(streams, k_ref, v_ref, scratch, i, tq, tk)
    o = acc / l
    lane = lax.broadcasted_iota(jnp.int32, (tq, LANES), 1)
    for pair in range(n_heads // 2):
        r0 = 2 * pair * tq
        o_ref[:, pair * LANES:(pair + 1) * LANES] = jnp.where(
            lane < HEAD_DIM, o[r0:r0 + tq, :], o[r0 + tq:r0 + 2 * tq, :]).astype(o_ref.dtype)


def _sb_kernel(q_ref, k_ref, v_ref, o_ref, z_scr, sp_scr, c_scr, w_scr, first_scr, later_scr, acc_scr,
               *, tk):
    tq = q_ref.shape[0]
    i = pl.program_id(2)
    ratio = tq // tk
    n_pairs = q_ref.shape[1] // LANES
    heads = []
    for pair in range(n_pairs):
        kv_cols = slice(pair * LANES, (pair + 1) * LANES)
        heads += [(qh, kv_cols) for qh in _split_head_pair(q_ref[:, kv_cols])]
    sub = min(SB_SUM_BLOCK, tk)
    n_sub = tk // sub
    lanes_per_sub = sub // LANES
    neg_later = jnp.where(lax.broadcasted_iota(jnp.int32, (sub, sub), 0)
                          > lax.broadcasted_iota(jnp.int32, (sub, sub), 1), -1.0, 0.0).astype(BF16)
    later_scr[...] = jnp.zeros(later_scr.shape, F32)
    acc_scr[...] = jnp.zeros(acc_scr.shape, F32)

    def step(j, limit):
        start = pl.multiple_of(j * tk, tk)
        for n, (qh, kv_cols) in enumerate(heads):
            z_scr[n * tq:(n + 1) * tq, :] = _nt_dot(qh, k_ref[pl.ds(start, tk), kv_cols])
        for n, (_, kv_cols) in enumerate(heads):
            head_rows = slice(n * tq, (n + 1) * tq)
            chunks = range(n * tq // ROW_CHUNK, (n + 1) * tq // ROW_CHUNK)
            for rc in chunks:
                rows = slice(rc * ROW_CHUNK, (rc + 1) * ROW_CHUNK)
                z = z_scr[rows, :]
                neg_abs = lax.bitcast_convert_type(
                    lax.bitcast_convert_type(z, jnp.int32) | jnp.int32(-2 ** 31), F32)
                sp = jnp.maximum(z, 0.0) + jnp.log2(1.0 + jnp.exp2(neg_abs))
                if limit is not None:
                    sp = jnp.where(_key_minus_query(rc * ROW_CHUNK, ROW_CHUNK, tq, tk) < limit, sp, 0.0)
                sp_scr[rows, :] = sp.astype(BF16)
                z_scr[rows, :] = z - sp
                for b in range(n_sub):
                    first_scr[rows, b * LANES:(b + 1) * LANES] = sp[:, b * sub:b * sub + LANES]
            for b in range(n_sub):
                c_scr[head_rows, b * sub:(b + 1) * sub] = _bdot(
                    sp_scr[head_rows, b * sub:(b + 1) * sub], neg_later)
            for rc in chunks:
                rows = slice(rc * ROW_CHUNK, (rc + 1) * ROW_CHUNK)
                later = later_scr[rows, :]
                for b in reversed(range(n_sub)):
                    for c in range(b * lanes_per_sub, (b + 1) * lanes_per_sub):
                        cols = slice(c * LANES, (c + 1) * LANES)
                        w = jnp.exp2(z_scr[rows, cols] + c_scr[rows, cols] + later)
                        if limit is not None:
                            keep = _key_minus_query(rc * ROW_CHUNK, ROW_CHUNK, tq, tk)[:, cols] < limit
                            w = jnp.where(keep, w, 0.0)
                        w_scr[rows, cols] = w.astype(BF16)
                    later = later + (c_scr[rows, b * sub:b * sub + 1]
                                     - first_scr[rows, b * LANES:b * LANES + 1])
                later_scr[rows, :] = later
            acc_scr[head_rows, :] += _bdot(w_scr[head_rows, :], v_ref[pl.ds(start, tk), kv_cols])

    for d in reversed(range(ratio)):
        step(ratio * i + d, -d * tk)

    def body(n, _):
        step(ratio * i - 1 - n, None)
        return 0

    lax.fori_loop(0, ratio * i, body, 0)
    lane = lax.broadcasted_iota(jnp.int32, (tq, LANES), 1)
    for pair in range(n_pairs):
        r0 = 2 * pair * tq
        o_ref[:, pair * LANES:(pair + 1) * LANES] = jnp.where(
            lane < HEAD_DIM, acc_scr[r0:r0 + tq, :], acc_scr[r0 + tq:r0 + 2 * tq, :]).astype(o_ref.dtype)


def _sb_scratch(n_rows, tk):
    n_sub = tk // min(SB_SUM_BLOCK, tk)
    return [pltpu.VMEM((n_rows, tk), F32), pltpu.VMEM((n_rows, tk), BF16),
            pltpu.VMEM((n_rows, tk), F32), pltpu.VMEM((n_rows, tk), BF16),
            pltpu.VMEM((n_rows, n_sub * LANES), F32),
            pltpu.VMEM((n_rows, LANES), F32), pltpu.VMEM((n_rows, LANES), F32)]


def _attention_call(body, name, q, k, v, extra, batch, seq, n_groups, tk, scratch_fn,
                    kv_buffers=2):
    n = q.shape[0]
    tq = min(seq, ATTN_Q_BLOCK)
    tk = min(tk, tq)
    nq = seq // tq
    q_width, v_width = q.shape[1] // n_groups, v.shape[1] // n_groups
    kv_mode = pl.Buffered(kv_buffers) if kv_buffers != 2 else None
    return pl.pallas_call(
        functools.partial(body, tk=tk),
        grid=(batch, n_groups, nq),
        in_specs=[pl.BlockSpec((tq, q_width), lambda b, g, i: (b * nq + i, g)),
                  pl.BlockSpec((seq, q_width), lambda b, g, i: (b, g), pipeline_mode=kv_mode),
                  pl.BlockSpec((seq, v_width), lambda b, g, i: (b, g), pipeline_mode=kv_mode)]
                 + [_full(e.shape) for e in extra],
        out_specs=pl.BlockSpec((tq, v_width), lambda b, g, i: (b * nq + i, g)),
        out_shape=jax.ShapeDtypeStruct((n, v.shape[1]), BF16),
        scratch_shapes=scratch_fn(2 * tq * v_width // LANES, tk),
        compiler_params=_cparams(3),
        name=name,
    )(q, k, v, *extra)


def _merge_kernel(x_ref, u_ref, halo_ref, od_ref, os_ref, of_ref, an_ref, pw_ref, ps_ref,
                  wbr_ref, wg_ref, bg_ref, wout_ref, o_ref):
    tm, d = x_ref.shape
    x = x_ref[...]
    h = _rms_rows(x, an_ref[...]).astype(BF16)

    u = u_ref[...]
    first_tile = pl.program_id(1) == 0
    halo = jnp.where(first_tile, 0.0, halo_ref[...])
    ext = jnp.concatenate([halo, u], axis=0)
    lane = lax.broadcasted_iota(jnp.int32, (tm, POOL_WIDTH), 1)
    group = lane >> 6
    win = jnp.zeros_like(u)
    width = jnp.zeros((tm, POOL_WIDTH), jnp.int32)
    span = 1
    for g, w in enumerate(POOL_WINDOWS):
        while span < w:
            ext = ext + pltpu.roll(ext, span, 0)
            span *= 2
        win = jnp.where(group == g, ext[POOL_HALO:, :], win)
        width = jnp.where(group == g, w, width)
    pos = pl.program_id(1) * tm + lax.broadcasted_iota(jnp.int32, (tm, POOL_WIDTH), 0)
    count = jnp.minimum(pos + 1, width).astype(F32)
    pooled = win / count - u
    o_pool = (_bdot(pooled.astype(BF16), pw_ref[...]) * ps_ref[...]).astype(BF16)

    merged = jnp.zeros((tm, d), F32)
    lo = 0
    for nb, o in enumerate((o_pool, od_ref[...], os_ref[...], of_ref[...])):
        width_n = BRANCH_WIDTHS[nb]
        y = _bdot(o, wbr_ref[lo:lo + width_n, :])
        gate = jax.nn.sigmoid(_nt_dot(h, wg_ref[nb * d:(nb + 1) * d, :]) + bg_ref[nb:nb + 1, :])
        merged = merged + gate * y
        lo += width_n
    o_ref[...] = x + _bdot(merged.astype(BF16), wout_ref[...])


def _merge_call(xf, u, od, os_, of, an, pw, ps, wbr, wg, bg, wout, batch, seq):
    n, d = xf.shape
    tm = min(seq, ROW_TILE)
    ns = seq // tm
    row = lambda b, s: (b * ns + s, 0)
    halo_blocks = tm // POOL_HALO
    halo = lambda b, s: (jnp.maximum((b * ns + s) * halo_blocks - 1, 0), 0)
    return pl.pallas_call(
        _merge_kernel,
        grid=(batch, ns),
        in_specs=[pl.BlockSpec((tm, d), row), pl.BlockSpec((tm, POOL_WIDTH), row),
                  pl.BlockSpec((POOL_HALO, POOL_WIDTH), halo),
                  pl.BlockSpec((tm, DIFF_WIDTH), row), pl.BlockSpec((tm, SB_WIDTH), row),
                  pl.BlockSpec((tm, FOX_WIDTH), row),
                  _full(an.shape), _full(pw.shape), _full(ps.shape), _full(wbr.shape),
                  _full(wg.shape), _full(bg.shape), _full(wout.shape)],
        out_specs=pl.BlockSpec((tm, d), row),
        out_shape=jax.ShapeDtypeStruct((n, d), F32),
        compiler_params=_cparams(2),
        name="merge_out_proj",
    )(xf, u, u, od, os_, of, an, pw, ps, wbr, wg, bg, wout)


def _ffn_kernel(x_ref, fn_ref, wup_ref, wdn_ref, o_ref, *, chunk):
    x = x_ref[...]
    h = _rms_rows(x, fn_ref[...]).astype(BF16)
    d_ff = wdn_ref.shape[0]
    acc = x
    for c in range(d_ff // chunk):
        gate = _bdot(h, wup_ref[:, c * chunk:(c + 1) * chunk])
        up = _bdot(h, wup_ref[:, d_ff + c * chunk:d_ff + (c + 1) * chunk])
        act = (gate * jax.nn.sigmoid(gate) * up).astype(BF16)
        acc = acc + _bdot(act, wdn_ref[c * chunk:(c + 1) * chunk, :])
    o_ref[...] = acc


def _ffn_call(xf, fn, wup, wdn):
    n, d = xf.shape
    tm = min(n, ROW_TILE)
    return pl.pallas_call(
        functools.partial(_ffn_kernel, chunk=256),
        grid=(n // tm,),
        in_specs=[pl.BlockSpec((tm, d), lambda i: (i, 0)), _full(fn.shape), _full(wup.shape),
                  _full(wdn.shape)],
        out_specs=pl.BlockSpec((tm, d), lambda i: (i, 0)),
        out_shape=jax.ShapeDtypeStruct((n, d), F32),
        compiler_params=_cparams(1),
        name="swiglu_ffn",
    )(xf, fn, wup, wdn)


def _group_mean_matrix():
    g = np.kron(np.eye(256 // HEAD_DIM), np.full((HEAD_DIM, HEAD_DIM), 1.0 / HEAD_DIM))
    return jnp.asarray(g, BF16)


def _lower_tri(nrows):
    return jnp.asarray(np.tril(np.ones((nrows, nrows))), BF16)


def _decay_selector():
    sel = np.zeros((LANES, 2 * FOX_HEADS * LANES), np.float32)
    for hd in range(FOX_HEADS):
        qb = hd * LANES + AUG_LANE
        kb = (FOX_HEADS + hd) * LANES + AUG_LANE
        for term in range(3):
            sel[4 * term + hd, qb + term] = 1.0
            sel[12, qb + 3 + term] = 1.0
            sel[12, kb + term] = 1.0
            sel[4 * term + hd, kb + 3 + term] = -1.0
    return jnp.asarray(sel, BF16)


def kernel(x, attn_norm, ffn_norm, w_in, b_gate, b_forget, pool_w, pool_scale, diff_q_norm,
           diff_k_norm, diff_subln, lam_q1, lam_k1, lam_q2, lam_k2, fox_q_norm, fox_k_norm,
           w_branch, w_out, w_ffn_up, w_ffn_down):
    batch, seq, d = x.shape
    depth = attn_norm.shape[0]
    assert seq % min(seq, ROW_TILE) == 0 and seq % min(seq, ATTN_Q_BLOCK) == 0
    xf = x.reshape(batch * seq, d)
    cos, sin = _rope_tables(seq)
    gmat = _group_mean_matrix()
    ltri = _lower_tri(min(seq, 256))
    sel = _decay_selector()
    softmax_scratch = functools.partial(_softmax_scratch, v_width=LANES)

    for layer in range(depth):
        wt = jnp.swapaxes(w_in[layer], 0, 1)
        wf = wt[MAIN_WIDTH:MAIN_WIDTH + FOX_HEADS]
        wf_pad = jnp.concatenate([wf, wf, wf, jnp.zeros((LANES - 3 * FOX_HEADS, d), wt.dtype)], axis=0)
        bfg = b_forget[layer]
        bf_pad = jnp.concatenate([bfg, bfg, bfg, jnp.zeros((LANES - 3 * FOX_HEADS,), F32)])[None, :]
        gains = jnp.stack([jnp.tile(g[layer], 256 // HEAD_DIM)
                           for g in (diff_q_norm, diff_k_norm, fox_q_norm, fox_k_norm)])
        an = attn_norm[layer][None, :]

        u, dq, dk, dv, sq, sk, sv, fq, fk, fv = _proj_call(
            xf, an, wt[:MAIN_WIDTH].astype(BF16), wf_pad.astype(BF16), bf_pad, cos, sin, gains, gmat,
            ltri, sel, batch, seq)

        lam = jnp.stack([lam_q1[layer], lam_k1[layer], lam_q2[layer], lam_k2[layer]])
        od = _attention_call(
            functools.partial(_diff_kernel, lambda_init=_diff_lambda_init(layer)), "diff_attention",
            dq, dk, dv, (lam, diff_subln[layer][None, :]), batch, seq, DIFF_GROUPS,
            SOFTMAX_K_BLOCK, softmax_scratch, kv_buffers=1)
        os_ = _attention_call(_sb_kernel, "stick_breaking_attention", sq, sk, sv, (), batch, seq,
                              1, SB_K_BLOCK, _sb_scratch)
        of = _attention_call(_fox_kernel, "forgetting_attention", fq, fk, fv, (), batch, seq,
                             1, SOFTMAX_K_BLOCK, softmax_scratch)

        pw = jax.scipy.linalg.block_diag(*[pool_w[layer, g] for g in range(len(POOL_WINDOWS))])
        x1 = _merge_call(xf, u, od, os_, of, an, pw.astype(BF16), pool_scale[layer][None, :],
                         w_branch[layer].astype(BF16), wt[MAIN_WIDTH + FOX_HEADS:].astype(BF16),
                         b_gate[layer], w_out[layer].astype(BF16), batch, seq)
        xf = _ffn_call(x1, ffn_norm[layer][None, :], w_ffn_up[layer].astype(BF16),
                       w_ffn_down[layer].astype(BF16))
    return xf.reshape(batch, seq, d)
```

```python
import functools
import math

import numpy as np
import jax
import jax.numpy as jnp
from jax import lax
from jax.experimental import pallas as pl
from jax.experimental.pallas import tpu as pltpu

F32 = jnp.float32
BF16 = jnp.bfloat16

HEAD_DIM = 64
POOL_WINDOWS = (2, 4, 8, 16)
POOL_WIDTH = 256
DIFF_WIDTH = 512
SB_WIDTH = 256
FOX_HEADS = 4
FOX_WIDTH = 256
BRANCH_WIDTHS = (POOL_WIDTH, DIFF_WIDTH, SB_WIDTH, FOX_WIDTH)
MAIN_WIDTH = 3328
ROPE_THETA = 10000.0
NORM_EPS = 1e-6
QK_SCALE = HEAD_DIM ** -0.5
LOG2E = math.log2(math.e)
Q_SCALE = QK_SCALE * LOG2E

LANES = 128
ATTN_Q_BLOCK = 512
SOFTMAX_K_BLOCK = 512
SB_K_BLOCK = 512
SB_SUM_BLOCK = 256
DIFF_GROUPS = 1
ROW_TILE = 512
ROW_CHUNK = 128
POOL_HALO = 16
AUG_LANE = 64
VMEM_LIMIT = 56 * 1024 * 1024


def _diff_lambda_init(layer):
    return 0.8 - 0.6 * math.exp(-0.3 * layer)


def _cparams(n_axes):
    return pltpu.CompilerParams(dimension_semantics=("arbitrary",) * n_axes,
                                vmem_limit_bytes=VMEM_LIMIT)


def _full(shape):
    return pl.BlockSpec(shape, lambda *_: (0,) * len(shape))


def _rms_rows(x, gain):
    ms = jnp.mean(x * x, axis=-1, keepdims=True)
    return x * lax.rsqrt(ms + NORM_EPS) * gain


def _bdot(a, b):
    return jnp.dot(a, b, preferred_element_type=F32)


def _nt_dot(a, b):
    return lax.dot_general(a, b, (((1,), (1,)), ((), ())), preferred_element_type=F32)


def _split_head_pair(q):
    lane = lax.broadcasted_iota(jnp.int32, q.shape, 1)
    qf = q.astype(F32)
    return (jnp.where(lane < HEAD_DIM, qf, 0.0).astype(q.dtype),
            jnp.where(lane >= HEAD_DIM, qf, 0.0).astype(q.dtype))


def _split3(v):
    a1 = v.astype(BF16)
    r1 = v - a1.astype(F32)
    a2 = r1.astype(BF16)
    r2 = r1 - a2.astype(F32)
    a3 = r2.astype(BF16)
    return a1, a2, a3


def _rope_table_kernel(inv_ref, cos_ref, sin_ref):
    rows = cos_ref.shape[0]
    base = pl.program_id(0) * rows
    pos = (base + lax.broadcasted_iota(jnp.int32, (rows, LANES), 0)).astype(F32)
    ang = pos * inv_ref[...]
    lane = lax.broadcasted_iota(jnp.int32, (rows, LANES), 1)
    first_half = (lane & (HEAD_DIM // 2)) == 0
    s = jnp.sin(ang)
    cos_ref[...] = jnp.cos(ang)
    sin_ref[...] = jnp.where(first_half, -s, s)


def _rope_tables(seq):
    half = HEAD_DIM // 2
    inv_freq = ROPE_THETA ** (-jnp.arange(half, dtype=F32) / half)
    inv = jnp.tile(inv_freq, LANES // half)[None, :]
    rows = min(seq, ROW_TILE)
    return pl.pallas_call(
        _rope_table_kernel,
        grid=(seq // rows,),
        in_specs=[_full((1, LANES))],
        out_specs=[pl.BlockSpec((rows, LANES), lambda i: (i, 0))] * 2,
        out_shape=[jax.ShapeDtypeStruct((seq, LANES), F32)] * 2,
        compiler_params=_cparams(1),
        name="rope_tables",
    )(inv)


def _proj_kernel(x_ref, an_ref, w_ref, wf_ref, bf_ref, cos_ref, sin_ref, gains_ref, gmat_ref,
                 ltri_ref, sel_ref,
                 u_ref, dq_ref, dk_ref, dv_ref, sq_ref, sk_ref, sv_ref, fq_ref, fk_ref, fv_ref,
                 carry_ref):
    tm = x_ref.shape[0]
    h = _rms_rows(x_ref[...], an_ref[...]).astype(BF16)

    def proj(lo, width):
        return _nt_dot(h, w_ref[lo:lo + width, :])

    def group_norm(y, gain):
        ms = _bdot((y * y).astype(BF16), gmat_ref[...])
        return y * lax.rsqrt(ms + NORM_EPS) * gain

    cos = cos_ref[...]
    sin = sin_ref[...]
    lane = lax.broadcasted_iota(jnp.int32, (tm, LANES), 1)
    first_half = (lane & (HEAD_DIM // 2)) == 0
    low_head = lane < HEAD_DIM

    def rope(y):
        partner = jnp.where(first_half, pltpu.roll(y, LANES - HEAD_DIM // 2, 1),
                            pltpu.roll(y, HEAD_DIM // 2, 1))
        return y * cos + partner * sin

    u_ref[...] = proj(0, 256)

    gq = gains_ref[0:1, :]
    gk = gains_ref[1:2, :]
    for c in range(2):
        yq = group_norm(proj(256 + 256 * c, 256), gq)
        yk = group_norm(proj(768 + 256 * c, 256), gk)
        for hh in range(2):
            sl = slice(hh * LANES, (hh + 1) * LANES)
            col = slice(256 * c + hh * LANES, 256 * c + (hh + 1) * LANES)
            dq_ref[:, col] = (rope(yq[:, sl]) * Q_SCALE).astype(BF16)
            dk_ref[:, col] = rope(yk[:, sl]).astype(BF16)
        dv_ref[:, 256 * c:256 * (c + 1)] = proj(1280 + 256 * c, 256).astype(BF16)

    sq_ref[...] = (proj(1792, 256) * Q_SCALE).astype(BF16)
    sk_ref[...] = proj(2048, 256).astype(BF16)
    sv_ref[...] = proj(2304, 256).astype(BF16)
    fv_ref[...] = proj(3072, 256).astype(BF16)

    fl = _nt_dot(h, wf_ref[...]) + bf_ref[...]
    logf = jnp.minimum(fl, 0.0) - jnp.log1p(jnp.exp(-jnp.abs(fl)))

    @pl.when(pl.program_id(1) == 0)
    def _():
        carry_ref[...] = jnp.zeros_like(carry_ref)

    carry = carry_ref[0:1, :]
    sub = ltri_ref.shape[0]
    pieces = []
    for sb in range(tm // sub):
        a1, a2, a3 = _split3(logf[sb * sub:(sb + 1) * sub, :])
        ltri = ltri_ref[...]
        cum = _bdot(ltri, a1) + _bdot(ltri, a2) + _bdot(ltri, a3) + carry
        carry = cum[sub - 1:sub, :]
        pieces.append(cum)
    carry_ref[0:1, :] = carry
    cum = jnp.concatenate(pieces, axis=0) if len(pieces) > 1 else pieces[0]

    c1, c2, c3 = (c.astype(F32) for c in _split3(cum * LOG2E))
    packed = jnp.where(lane < 4, c1, jnp.where(lane < 8, c2, jnp.where(lane < 12, c3,
                       jnp.where(lane == 12, 1.0, 0.0))))
    aug = _bdot(packed.astype(BF16), sel_ref[...])

    yq = group_norm(proj(2560, 256), gains_ref[2:3, :]) * Q_SCALE
    yk = group_norm(proj(2816, 256), gains_ref[3:4, :])
    for src, dst, off in ((yq, fq_ref, 0), (yk, fk_ref, 4 * LANES)):
        for pair in range(2):
            v = src[:, pair * LANES:(pair + 1) * LANES]
            heads = (v, pltpu.roll(v, HEAD_DIM, 1))
            for hh in range(2):
                col = (2 * pair + hh) * LANES
                dst[:, col:col + LANES] = jnp.where(
                    low_head, heads[hh], aug[:, off + col:off + col + LANES]).astype(BF16)


def _proj_call(xf, an, w_main, wf_pad, bf_pad, cos, sin, gains, gmat, ltri, sel, batch, seq):
    n, d = xf.shape
    tm = min(seq, ROW_TILE)
    ns = seq // tm
    row = lambda b, s: (b * ns + s, 0)
    widths = (256, 512, 512, 512, 256, 256, 256, 512, 512, 256)
    dtypes = (F32,) + (BF16,) * 9
    return pl.pallas_call(
        _proj_kernel,
        grid=(batch, ns),
        in_specs=[pl.BlockSpec((tm, d), row), _full(an.shape), _full(w_main.shape),
                  _full(wf_pad.shape), _full(bf_pad.shape),
                  pl.BlockSpec((tm, LANES), lambda b, s: (s, 0)),
                  pl.BlockSpec((tm, LANES), lambda b, s: (s, 0)),
                  _full(gains.shape), _full(gmat.shape), _full(ltri.shape), _full(sel.shape)],
        out_specs=[pl.BlockSpec((tm, w), row) for w in widths],
        out_shape=[jax.ShapeDtypeStruct((n, w), t) for w, t in zip(widths, dtypes)],
        scratch_shapes=[pltpu.VMEM((8, LANES), F32)],
        compiler_params=_cparams(2),
        name="norm_in_proj",
    )(xf, an, w_main, wf_pad, bf_pad, cos, sin, gains, gmat, ltri, sel)


def _key_minus_query(row0, rows, tq, tk):
    r = (row0 + lax.broadcasted_iota(jnp.int32, (rows, tk), 0)) & (tq - 1)
    c = lax.broadcasted_iota(jnp.int32, (rows, tk), 1)
    return c - r


def _softmax_sweep(streams, k_ref, v_ref, scratch, i, tq, tk):
    s_scr, p_scr, m_scr, l_scr, acc_scr = scratch
    n_lane_chunks = tk // LANES
    m_scr[...] = jnp.full(m_scr.shape, -1e30, F32)
    l_scr[...] = jnp.zeros(l_scr.shape, F32)
    acc_scr[...] = jnp.zeros(acc_scr.shape, F32)

    def step(j, limit):
        start = pl.multiple_of(j * tk, tk)
        for n, (q, cols, _) in enumerate(streams):
            s = _nt_dot(q, k_ref[pl.ds(start, tk), cols])
            if limit is not None:
                s = jnp.where(_key_minus_query(0, tq, tq, tk) <= limit, s, -jnp.inf)
            s_scr[n * tq:(n + 1) * tq, :] = s
        for n, (_, _, vcols) in enumerate(streams):
            for rc in range(n * tq // ROW_CHUNK, (n + 1) * tq // ROW_CHUNK):
                rows = slice(rc * ROW_CHUNK, (rc + 1) * ROW_CHUNK)
                mx = s_scr[rows, 0:LANES]
                for c in range(1, n_lane_chunks):
                    mx = jnp.maximum(mx, s_scr[rows, c * LANES:(c + 1) * LANES])
                m_old = m_scr[rows, :]
                m_new = jnp.maximum(m_old, mx.max(axis=1, keepdims=True))
                alpha = jnp.exp2(m_old - m_new)
                m_scr[rows, :] = m_new
                psum = None
                for c in range(n_lane_chunks):
                    p = jnp.exp2(s_scr[rows, c * LANES:(c + 1) * LANES] - m_new)
                    psum = p if psum is None else psum + p
                    p_scr[rows, c * LANES:(c + 1) * LANES] = p.astype(BF16)
                l_scr[rows, :] = alpha * l_scr[rows, :] + psum
                acc_scr[rows, :] = alpha * acc_scr[rows, :]
            acc_scr[n * tq:(n + 1) * tq, :] += _bdot(p_scr[n * tq:(n + 1) * tq, :],
                                                     v_ref[pl.ds(start, tk), vcols])

    ratio = tq // tk

    def body(j, _):
        step(j, None)
        return 0

    lax.fori_loop(0, ratio * i, body, 0)
    for d in range(ratio):
        step(ratio * i + d, -d * tk)
    return acc_scr[...], l_scr[...].sum(axis=1, keepdims=True)


def _softmax_scratch(n_rows, tk, v_width):
    return [pltpu.VMEM((n_rows, tk), F32), pltpu.VMEM((n_rows, tk), BF16),
            pltpu.VMEM((n_rows, LANES), F32), pltpu.VMEM((n_rows, LANES), F32),
            pltpu.VMEM((n_rows, v_width), F32)]


def _diff_kernel(q_ref, k_ref, v_ref, lam_ref, sub_ref, o_ref, *scratch, lambda_init, tk):
    tq = q_ref.shape[0]
    i = pl.program_id(2)
    n_heads = q_ref.shape[1] // LANES
    streams = []
    for hd in range(n_heads):
        cols = slice(hd * LANES, (hd + 1) * LANES)
        for qh in _split_head_pair(q_ref[:, cols]):
            streams.append((qh, cols, cols))
    acc, l = _softmax_sweep(streams, k_ref, v_ref, scratch, i, tq, tk)
    o = acc / l
    lp = lam_ref[...]
    lam = (jnp.exp(jnp.sum(lp[0:1, :] * lp[1:2, :], axis=1, keepdims=True))
           - jnp.exp(jnp.sum(lp[2:3, :] * lp[3:4, :], axis=1, keepdims=True)) + lambda_init)
    for hd in range(n_heads):
        r0 = 2 * hd * tq
        od = o[r0:r0 + tq, :] - lam * o[r0 + tq:r0 + 2 * tq, :]
        o_ref[:, hd * LANES:(hd + 1) * LANES] = (
            _rms_rows(od, sub_ref[...]) * (1.0 - lambda_init)).astype(o_ref.dtype)


def _fox_kernel(q_ref, k_ref, v_ref, o_ref, *scratch, tk):
    tq = q_ref.shape[0]
    i = pl.program_id(2)
    n_heads = q_ref.shape[1] // LANES
    streams = [(q_ref[:, hd * LANES:(hd + 1) * LANES], slice(hd * LANES, (hd + 1) * LANES),
                slice((hd // 2) * LANES, (hd // 2 + 1) * LANES)) for hd in range(n_heads)]
    acc, l = _softmax_sweep(streams, k_ref, v_ref, scratch, i, tq, tk)
    o = acc / l
    lane = lax.broadcasted_iota(jnp.int32, (tq, LANES), 1)
    for pair in range(n_heads // 2):
        r0 = 2 * pair * tq
        o_ref[:, pair * LANES:(pair + 1) * LANES] = jnp.where(
            lane < HEAD_DIM, o[r0:r0 + tq, :], o[r0 + tq:r0 + 2 * tq, :]).astype(o_ref.dtype)


def _sb_kernel(q_ref, k_ref, v_ref, o_ref, z_scr, sp_scr, c_scr, w_scr, first_scr, later_scr, acc_scr,
               *, tk):
    tq = q_ref.shape[0]
    i = pl.program_id(2)
    ratio = tq // tk
    n_pairs = q_ref.shape[1] // LANES
    heads = []
    for pair in range(n_pairs):
        kv_cols = slice(pair * LANES, (pair + 1) * LANES)
        heads += [(qh, kv_cols) for qh in _split_head_pair(q_ref[:, kv_cols])]
    sub = min(SB_SUM_BLOCK, tk)
    n_sub = tk // sub
    lanes_per_sub = sub // LANES
    neg_later = jnp.where(lax.broadcasted_iota(jnp.int32, (sub, sub), 0)
                          > lax.broadcasted_iota(jnp.int32, (sub, sub), 1), -1.0, 0.0).astype(BF16)
    later_scr[...] = jnp.zeros(later_scr.shape, F32)
    acc_scr[...] = jnp.zeros(acc_scr.shape, F32)

    def step(j, limit):
        start = pl.multiple_of(j * tk, tk)
        for n, (qh, kv_cols) in enumerate(heads):
            z_scr[n * tq:(n + 1) * tq, :] = _nt_dot(qh, k_ref[pl.ds(start, tk), kv_cols])
        for n, (_, kv_cols) in enumerate(heads):
            head_rows = slice(n * tq, (n + 1) * tq)
            chunks = range(n * tq // ROW_CHUNK, (n + 1) * tq // ROW_CHUNK)
            for rc in chunks:
                rows = slice(rc * ROW_CHUNK, (rc + 1) * ROW_CHUNK)
                z = z_scr[rows, :]
                neg_abs = lax.bitcast_convert_type(
                    lax.bitcast_convert_type(z, jnp.int32) | jnp.int32(-2 ** 31), F32)
                sp = jnp.maximum(z, 0.0) + jnp.log2(1.0 + jnp.exp2(neg_abs))
                if limit is not None:
                    sp = jnp.where(_key_minus_query(rc * ROW_CHUNK, ROW_CHUNK, tq, tk) < limit, sp, 0.0)
                sp_scr[rows, :] = sp.astype(BF16)
                z_scr[rows, :] = z - sp
                for b in range(n_sub):
                    first_scr[rows, b * LANES:(b + 1) * LANES] = sp[:, b * sub:b * sub + LANES]
            for b in range(n_sub):
                c_scr[head_rows, b * sub:(b + 1) * sub] = _bdot(
                    sp_scr[head_rows, b * sub:(b + 1) * sub], neg_later)
            for rc in chunks:
                rows = slice(rc * ROW_CHUNK, (rc + 1) * ROW_CHUNK)
                later = later_scr[rows, :]
                for b in reversed(range(n_sub)):
                    for c in range(b * lanes_per_sub, (b + 1) * lanes_per_sub):
                        cols = slice(c * LANES, (c + 1) * LANES)
                        w = jnp.exp2(z_scr[rows, cols] + c_scr[rows, cols] + later)
                        if limit is not None:
                            keep = _key_minus_query(rc * ROW_CHUNK, ROW_CHUNK, tq, tk)[:, cols] < limit
                            w = jnp.where(keep, w, 0.0)
                        w_scr[rows, cols] = w.astype(BF16)
                    later = later + (c_scr[rows, b * sub:b * sub + 1]
                                     - first_scr[rows, b * LANES:b * LANES + 1])
                later_scr[rows, :] = later
            acc_scr[head_rows, :] += _bdot(w_scr[head_rows, :], v_ref[pl.ds(start, tk), kv_cols])

    for d in reversed(range(ratio)):
        step(ratio * i + d, -d * tk)

    def body(n, _):
        step(ratio * i - 1 - n, None)
        return 0

    lax.fori_loop(0, ratio * i, body, 0)
    lane = lax.broadcasted_iota(jnp.int32, (tq, LANES), 1)
    for pair in range(n_pairs):
        r0 = 2 * pair * tq
        o_ref[:, pair * LANES:(pair + 1) * LANES] = jnp.where(
            lane < HEAD_DIM, acc_scr[r0:r0 + tq, :], acc_scr[r0 + tq:r0 + 2 * tq, :]).astype(o_ref.dtype)


def _sb_scratch(n_rows, tk):
    n_sub = tk // min(SB_SUM_BLOCK, tk)
    return [pltpu.VMEM((n_rows, tk), F32), pltpu.VMEM((n_rows, tk), BF16),
            pltpu.VMEM((n_rows, tk), F32), pltpu.VMEM((n_rows, tk), BF16),
            pltpu.VMEM((n_rows, n_sub * LANES), F32),
            pltpu.VMEM((n_rows, LANES), F32), pltpu.VMEM((n_rows, LANES), F32)]


def _attention_call(body, name, q, k, v, extra, batch, seq, n_groups, tk, scratch_fn,
                    kv_buffers=2):
    n = q.shape[0]
    tq = min(seq, ATTN_Q_BLOCK)
    tk = min(tk, tq)
    nq = seq // tq
    q_width, v_width = q.shape[1] // n_groups, v.shape[1] // n_groups
    kv_mode = pl.Buffered(kv_buffers) if kv_buffers != 2 else None
    return pl.pallas_call(
        functools.partial(body, tk=tk),
        grid=(batch, n_groups, nq),
        in_specs=[pl.BlockSpec((tq, q_width), lambda b, g, i: (b * nq + i, g)),
                  pl.BlockSpec((seq, q_width), lambda b, g, i: (b, g), pipeline_mode=kv_mode),
                  pl.BlockSpec((seq, v_width), lambda b, g, i: (b, g), pipeline_mode=kv_mode)]
                 + [_full(e.shape) for e in extra],
        out_specs=pl.BlockSpec((tq, v_width), lambda b, g, i: (b * nq + i, g)),
        out_shape=jax.ShapeDtypeStruct((n, v.shape[1]), BF16),
        scratch_shapes=scratch_fn(2 * tq * v_width // LANES, tk),
        compiler_params=_cparams(3),
        name=name,
    )(q, k, v, *extra)


def _merge_tile(x_ref, u_ref, halo_ref, od_ref, os_ref, of_ref, an_ref, pw_ref, ps_ref,
                wbr_ref, wg_ref, bg_ref, wout_ref):
    tm, d = x_ref.shape
    x = x_ref[...]
    h = _rms_rows(x, an_ref[...]).astype(BF16)

    u = u_ref[...]
    first_tile = pl.program_id(1) == 0
    halo = jnp.where(first_tile, 0.0, halo_ref[...])
    ext = jnp.concatenate([halo, u], axis=0)
    lane = lax.broadcasted_iota(jnp.int32, (tm, POOL_WIDTH), 1)
    group = lane >> 6
    win = jnp.zeros_like(u)
    width = jnp.zeros((tm, POOL_WIDTH), jnp.int32)
    span = 1
    for g, w in enumerate(POOL_WINDOWS):
        while span < w:
            ext = ext + pltpu.roll(ext, span, 0)
            span *= 2
        win = jnp.where(group == g, ext[POOL_HALO:, :], win)
        width = jnp.where(group == g, w, width)
    pos = pl.program_id(1) * tm + lax.broadcasted_iota(jnp.int32, (tm, POOL_WIDTH), 0)
    count = jnp.minimum(pos + 1, width).astype(F32)
    pooled = win / count - u
    o_pool = (_bdot(pooled.astype(BF16), pw_ref[...]) * ps_ref[...]).astype(BF16)

    merged = jnp.zeros((tm, d), F32)
    lo = 0
    for nb, o in enumerate((o_pool, od_ref[...], os_ref[...], of_ref[...])):
        width_n = BRANCH_WIDTHS[nb]
        y = _bdot(o, wbr_ref[lo:lo + width_n, :])
        gate = jax.nn.sigmoid(_nt_dot(h, wg_ref[nb * d:(nb + 1) * d, :]) + bg_ref[nb:nb + 1, :])
        merged = merged + gate * y
        lo += width_n
    return x + _bdot(merged.astype(BF16), wout_ref[...])


def _ffn_tile(x, fn_ref, wup_ref, wdn_ref, chunk):
    h = _rms_rows(x, fn_ref[...]).astype(BF16)
    d_ff = wdn_ref.shape[0]
    acc = x
    for c in range(d_ff // chunk):
        gate = _bdot(h, wup_ref[:, c * chunk:(c + 1) * chunk])
        up = _bdot(h, wup_ref[:, d_ff + c * chunk:d_ff + (c + 1) * chunk])
        act = (gate * jax.nn.sigmoid(gate) * up).astype(BF16)
        acc = acc + _bdot(act, wdn_ref[c * chunk:(c + 1) * chunk, :])
    return acc


def _merge_ffn_kernel(*refs, chunk):
    *merge_refs, fn_ref, wup_ref, wdn_ref, o_ref = refs
    o_ref[...] = _ffn_tile(_merge_tile(*merge_refs), fn_ref, wup_ref, wdn_ref, chunk)


def _merge_ffn_call(xf, u, od, os_, of, an, pw, ps, wbr, wg, bg, wout, fn, wup, wdn, batch, seq):
    n, d = xf.shape
    tm = min(seq, ROW_TILE)
    ns = seq // tm
    row = lambda b, s: (b * ns + s, 0)
    halo_blocks = tm // POOL_HALO
    halo = lambda b, s: (jnp.maximum((b * ns + s) * halo_blocks - 1, 0), 0)
    once = lambda a: pl.BlockSpec(a.shape, lambda b, s: (0,) * a.ndim, pipeline_mode=pl.Buffered(1))
    return pl.pallas_call(
        functools.partial(_merge_ffn_kernel, chunk=256),
        grid=(batch, ns),
        in_specs=[pl.BlockSpec((tm, d), row), pl.BlockSpec((tm, POOL_WIDTH), row),
                  pl.BlockSpec((POOL_HALO, POOL_WIDTH), halo),
                  pl.BlockSpec((tm, DIFF_WIDTH), row), pl.BlockSpec((tm, SB_WIDTH), row),
                  pl.BlockSpec((tm, FOX_WIDTH), row),
                  _full(an.shape), once(pw), _full(ps.shape), once(wbr), once(wg), _full(bg.shape),
                  once(wout), _full(fn.shape), once(wup), once(wdn)],
        out_specs=pl.BlockSpec((tm, d), row),
        out_shape=jax.ShapeDtypeStruct((n, d), F32),
        compiler_params=_cparams(2),
        name="merge_and_ffn",
    )(xf, u, u, od, os_, of, an, pw, ps, wbr, wg, bg, wout, fn, wup, wdn)


def _group_mean_matrix():
    g = np.kron(np.eye(256 // HEAD_DIM), np.full((HEAD_DIM, HEAD_DIM), 1.0 / HEAD_DIM))
    return jnp.asarray(g, BF16)


def _lower_tri(nrows):
    return jnp.asarray(np.tril(np.ones((nrows, nrows))), BF16)


def _decay_selector():
    sel = np.zeros((LANES, 2 * FOX_HEADS * LANES), np.float32)
    for hd in range(FOX_HEADS):
        qb = hd * LANES + AUG_LANE
        kb = (FOX_HEADS + hd) * LANES + AUG_LANE
        for term in range(3):
            sel[4 * term + hd, qb + term] = 1.0
            sel[12, qb + 3 + term] = 1.0
            sel[12, kb + term] = 1.0
            sel[4 * term + hd, kb + 3 + term] = -1.0
    return jnp.asarray(sel, BF16)


def kernel(x, attn_norm, ffn_norm, w_in, b_gate, b_forget, pool_w, pool_scale, diff_q_norm,
           diff_k_norm, diff_subln, lam_q1, lam_k1, lam_q2, lam_k2, fox_q_norm, fox_k_norm,
           w_branch, w_out, w_ffn_up, w_ffn_down):
    batch, seq, d = x.shape
    depth = attn_norm.shape[0]
    assert seq % min(seq, ROW_TILE) == 0 and seq % min(seq, ATTN_Q_BLOCK) == 0
    xf = x.reshape(batch * seq, d)
    cos, sin = _rope_tables(seq)
    gmat = _group_mean_matrix()
    ltri = _lower_tri(min(seq, 256))
    sel = _decay_selector()
    softmax_scratch = functools.partial(_softmax_scratch, v_width=LANES)

    for layer in range(depth):
        wt = jnp.swapaxes(w_in[layer], 0, 1)
        wf = wt[MAIN_WIDTH:MAIN_WIDTH + FOX_HEADS]
        wf_pad = jnp.concatenate([wf, wf, wf, jnp.zeros((LANES - 3 * FOX_HEADS, d), wt.dtype)], axis=0)
        bfg = b_forget[layer]
        bf_pad = jnp.concatenate([bfg, bfg, bfg, jnp.zeros((LANES - 3 * FOX_HEADS,), F32)])[None, :]
        gains = jnp.stack([jnp.tile(g[layer], 256 // HEAD_DIM)
                           for g in (diff_q_norm, diff_k_norm, fox_q_norm, fox_k_norm)])
        an = attn_norm[layer][None, :]

        u, dq, dk, dv, sq, sk, sv, fq, fk, fv = _proj_call(
            xf, an, wt[:MAIN_WIDTH].astype(BF16), wf_pad.astype(BF16), bf_pad, cos, sin, gains, gmat,
            ltri, sel, batch, seq)

        lam = jnp.stack([lam_q1[layer], lam_k1[layer], lam_q2[layer], lam_k2[layer]])
        od = _attention_call(
            functools.partial(_diff_kernel, lambda_init=_diff_lambda_init(layer)), "diff_attention",
            dq, dk, dv, (lam, diff_subln[layer][None, :]), batch, seq, DIFF_GROUPS,
            SOFTMAX_K_BLOCK, softmax_scratch, kv_buffers=1)
        os_ = _attention_call(_sb_kernel, "stick_breaking_attention", sq, sk, sv, (), batch, seq,
                              1, SB_K_BLOCK, _sb_scratch)
        of = _attention_call(_fox_kernel, "forgetting_attention", fq, fk, fv, (), batch, seq,
                             1, SOFTMAX_K_BLOCK, softmax_scratch)

        pw = jax.scipy.linalg.block_diag(*[pool_w[layer, g] for g in range(len(POOL_WINDOWS))])
        xf = _merge_ffn_call(xf, u, od, os_, of, an, pw.astype(BF16), pool_scale[layer][None, :],
                             w_branch[layer].astype(BF16), wt[MAIN_WIDTH + FOX_HEADS:].astype(BF16),
                             b_gate[layer], w_out[layer].astype(BF16), ffn_norm[layer][None, :],
                             w_ffn_up[layer].astype(BF16), w_ffn_down[layer].astype(BF16), batch, seq)
    return xf.reshape(batch, seq, d)
```

```python
import functools
import math

import numpy as np
import jax
import jax.numpy as jnp
from jax import lax
from jax.experimental import pallas as pl
from jax.experimental.pallas import tpu as pltpu

F32 = jnp.float32
BF16 = jnp.bfloat16

HEAD_DIM = 64
POOL_WINDOWS = (2, 4, 8, 16)
POOL_WIDTH = 256
DIFF_WIDTH = 512
SB_WIDTH = 256
FOX_HEADS = 4
FOX_WIDTH = 256
BRANCH_WIDTHS = (POOL_WIDTH, DIFF_WIDTH, SB_WIDTH, FOX_WIDTH)
MAIN_WIDTH = 3328
ROPE_THETA = 10000.0
NORM_EPS = 1e-6
QK_SCALE = HEAD_DIM ** -0.5
LOG2E = math.log2(math.e)
Q_SCALE = QK_SCALE * LOG2E

LANES = 128
ATTN_Q_BLOCK = 512
SOFTMAX_K_BLOCK = 512
SB_K_BLOCK = 512
SB_SUM_BLOCK = 256
DIFF_GROUPS = 1
ROW_TILE = 512
ROW_CHUNK = 128
POOL_HALO = 16
AUG_LANE = 64
VMEM_LIMIT = 56 * 1024 * 1024


def _diff_lambda_init(layer):
    return 0.8 - 0.6 * math.exp(-0.3 * layer)


def _cparams(n_axes):
    return pltpu.CompilerParams(dimension_semantics=("arbitrary",) * n_axes,
                                vmem_limit_bytes=VMEM_LIMIT)


def _full(shape):
    return pl.BlockSpec(shape, lambda *_: (0,) * len(shape))


def _rms_rows(x, gain):
    ms = jnp.mean(x * x, axis=-1, keepdims=True)
    return x * lax.rsqrt(ms + NORM_EPS) * gain


def _bdot(a, b):
    return jnp.dot(a, b, preferred_element_type=F32)


def _nt_dot(a, b):
    return lax.dot_general(a, b, (((1,), (1,)), ((), ())), preferred_element_type=F32)


def _split_head_pair(q):
    lane = lax.broadcasted_iota(jnp.int32, q.shape, 1)
    qf = q.astype(F32)
    return (jnp.where(lane < HEAD_DIM, qf, 0.0).astype(q.dtype),
            jnp.where(lane >= HEAD_DIM, qf, 0.0).astype(q.dtype))


def _split3(v):
    a1 = v.astype(BF16)
    r1 = v - a1.astype(F32)
    a2 = r1.astype(BF16)
    r2 = r1 - a2.astype(F32)
    a3 = r2.astype(BF16)
    return a1, a2, a3


def _rope_table_kernel(inv_ref, cos_ref, sin_ref):
    rows = cos_ref.shape[0]
    base = pl.program_id(0) * rows
    pos = (base + lax.broadcasted_iota(jnp.int32, (rows, LANES), 0)).astype(F32)
    ang = pos * inv_ref[...]
    lane = lax.broadcasted_iota(jnp.int32, (rows, LANES), 1)
    first_half = (lane & (HEAD_DIM // 2)) == 0
    s = jnp.sin(ang)
    cos_ref[...] = jnp.cos(ang)
    sin_ref[...] = jnp.where(first_half, -s, s)


def _rope_tables(seq):
    half = HEAD_DIM // 2
    inv_freq = ROPE_THETA ** (-jnp.arange(half, dtype=F32) / half)
    inv = jnp.tile(inv_freq, LANES // half)[None, :]
    rows = min(seq, ROW_TILE)
    return pl.pallas_call(
        _rope_table_kernel,
        grid=(seq // rows,),
        in_specs=[_full((1, LANES))],
        out_specs=[pl.BlockSpec((rows, LANES), lambda i: (i, 0))] * 2,
        out_shape=[jax.ShapeDtypeStruct((seq, LANES), F32)] * 2,
        compiler_params=_cparams(1),
        name="rope_tables",
    )(inv)


def _proj_kernel(x_ref, an_ref, w_ref, wf_ref, bf_ref, cos_ref, sin_ref, gains_ref, gmat_ref,
                 ltri_ref, sel_ref,
                 u_ref, dq_ref, dk_ref, dv_ref, sq_ref, sk_ref, sv_ref, fq_ref, fk_ref, fv_ref,
                 carry_ref):
    tm = x_ref.shape[0]
    h = _rms_rows(x_ref[...], an_ref[...]).astype(BF16)

    def proj(lo, width):
        return _nt_dot(h, w_ref[lo:lo + width, :])

    def group_norm(y, gain):
        ms = _bdot((y * y).astype(BF16), gmat_ref[...])
        return y * lax.rsqrt(ms + NORM_EPS) * gain

    cos = cos_ref[...]
    sin = sin_ref[...]
    lane = lax.broadcasted_iota(jnp.int32, (tm, LANES), 1)
    first_half = (lane & (HEAD_DIM // 2)) == 0
    low_head = lane < HEAD_DIM

    def rope(y):
        partner = jnp.where(first_half, pltpu.roll(y, LANES - HEAD_DIM // 2, 1),
                            pltpu.roll(y, HEAD_DIM // 2, 1))
        return y * cos + partner * sin

    u_ref[...] = proj(0, 256)

    gq = gains_ref[0:1, :]
    gk = gains_ref[1:2, :]
    for c in range(2):
        yq = group_norm(proj(256 + 256 * c, 256), gq)
        yk = group_norm(proj(768 + 256 * c, 256), gk)
        for hh in range(2):
            sl = slice(hh * LANES, (hh + 1) * LANES)
            col = slice(256 * c + hh * LANES, 256 * c + (hh + 1) * LANES)
            dq_ref[:, col] = (rope(yq[:, sl]) * Q_SCALE).astype(BF16)
            dk_ref[:, col] = rope(yk[:, sl]).astype(BF16)
        dv_ref[:, 256 * c:256 * (c + 1)] = proj(1280 + 256 * c, 256).astype(BF16)

    sq_ref[...] = (proj(1792, 256) * Q_SCALE).astype(BF16)
    sk_ref[...] = proj(2048, 256).astype(BF16)
    sv_ref[...] = proj(2304, 256).astype(BF16)
    fv_ref[...] = proj(3072, 256).astype(BF16)

    fl = _nt_dot(h, wf_ref[...]) + bf_ref[...]
    logf = jnp.minimum(fl, 0.0) - jnp.log1p(jnp.exp(-jnp.abs(fl)))

    @pl.when(pl.program_id(1) == 0)
    def _():
        carry_ref[...] = jnp.zeros_like(carry_ref)

    carry = carry_ref[0:1, :]
    sub = ltri_ref.shape[0]
    pieces = []
    for sb in range(tm // sub):
        a1, a2, a3 = _split3(logf[sb * sub:(sb + 1) * sub, :])
        ltri = ltri_ref[...]
        cum = _bdot(ltri, a1) + _bdot(ltri, a2) + _bdot(ltri, a3) + carry
        carry = cum[sub - 1:sub, :]
        pieces.append(cum)
    carry_ref[0:1, :] = carry
    cum = jnp.concatenate(pieces, axis=0) if len(pieces) > 1 else pieces[0]

    c1, c2, c3 = (c.astype(F32) for c in _split3(cum * LOG2E))
    packed = jnp.where(lane < 4, c1, jnp.where(lane < 8, c2, jnp.where(lane < 12, c3,
                       jnp.where(lane == 12, 1.0, 0.0))))
    aug = _bdot(packed.astype(BF16), sel_ref[...])

    yq = group_norm(proj(2560, 256), gains_ref[2:3, :]) * Q_SCALE
    yk = group_norm(proj(2816, 256), gains_ref[3:4, :])
    for src, dst, off in ((yq, fq_ref, 0), (yk, fk_ref, 4 * LANES)):
        for pair in range(2):
            v = src[:, pair * LANES:(pair + 1) * LANES]
            heads = (v, pltpu.roll(v, HEAD_DIM, 1))
            for hh in range(2):
                col = (2 * pair + hh) * LANES
                dst[:, col:col + LANES] = jnp.where(
                    low_head, heads[hh], aug[:, off + col:off + col + LANES]).astype(BF16)


def _proj_call(xf, an, w_main, wf_pad, bf_pad, cos, sin, gains, gmat, ltri, sel, batch, seq):
    n, d = xf.shape
    tm = min(seq, ROW_TILE)
    ns = seq // tm
    row = lambda b, s: (b * ns + s, 0)
    widths = (256, 512, 512, 512, 256, 256, 256, 512, 512, 256)
    dtypes = (F32,) + (BF16,) * 9
    return pl.pallas_call(
        _proj_kernel,
        grid=(batch, ns),
        in_specs=[pl.BlockSpec((tm, d), row), _full(an.shape), _full(w_main.shape),
                  _full(wf_pad.shape), _full(bf_pad.shape),
                  pl.BlockSpec((tm, LANES), lambda b, s: (s, 0)),
                  pl.BlockSpec((tm, LANES), lambda b, s: (s, 0)),
                  _full(gains.shape), _full(gmat.shape), _full(ltri.shape), _full(sel.shape)],
        out_specs=[pl.BlockSpec((tm, w), row) for w in widths],
        out_shape=[jax.ShapeDtypeStruct((n, w), t) for w, t in zip(widths, dtypes)],
        scratch_shapes=[pltpu.VMEM((8, LANES), F32)],
        compiler_params=_cparams(2),
        name="norm_in_proj",
    )(xf, an, w_main, wf_pad, bf_pad, cos, sin, gains, gmat, ltri, sel)


def _key_minus_query(row0, rows, tq, tk):
    r = (row0 + lax.broadcasted_iota(jnp.int32, (rows, tk), 0)) & (tq - 1)
    c = lax.broadcasted_iota(jnp.int32, (rows, tk), 1)
    return c - r


def _softmax_sweep(streams, k_ref, v_ref, scratch, i, tq, tk):
    s_scr, p_scr, m_scr, l_scr, acc_scr = scratch
    n_lane_chunks = tk // LANES
    m_scr[...] = jnp.full(m_scr.shape, -1e30, F32)
    l_scr[...] = jnp.zeros(l_scr.shape, F32)
    acc_scr[...] = jnp.zeros(acc_scr.shape, F32)

    def row_pass(rc, n_chunks):
        rows = slice(rc * ROW_CHUNK, (rc + 1) * ROW_CHUNK)
        mx = s_scr[rows, 0:LANES]
        for c in range(1, n_chunks):
            mx = jnp.maximum(mx, s_scr[rows, c * LANES:(c + 1) * LANES])
        m_old = m_scr[rows, :]
        m_new = jnp.maximum(m_old, mx.max(axis=1, keepdims=True))
        alpha = jnp.exp2(m_old - m_new)
        m_scr[rows, :] = m_new
        psum = None
        for c in range(n_chunks):
            p = jnp.exp2(s_scr[rows, c * LANES:(c + 1) * LANES] - m_new)
            psum = p if psum is None else psum + p
            p_scr[rows, c * LANES:(c + 1) * LANES] = p.astype(BF16)
        l_scr[rows, :] = alpha * l_scr[rows, :] + psum
        acc_scr[rows, :] = alpha * acc_scr[rows, :]

    def step(j, limit):
        start = pl.multiple_of(j * tk, tk)
        for n, (q, cols, _) in enumerate(streams):
            s = _nt_dot(q, k_ref[pl.ds(start, tk), cols])
            if limit is not None:
                s = jnp.where(_key_minus_query(0, tq, tq, tk) <= limit, s, -jnp.inf)
            s_scr[n * tq:(n + 1) * tq, :] = s
        for n, (_, _, vcols) in enumerate(streams):
            for rc in range(n * tq // ROW_CHUNK, (n + 1) * tq // ROW_CHUNK):
                row_pass(rc, n_lane_chunks)
            acc_scr[n * tq:(n + 1) * tq, :] += _bdot(p_scr[n * tq:(n + 1) * tq, :],
                                                     v_ref[pl.ds(start, tk), vcols])

    def diagonal_step(j):
        start = pl.multiple_of(j * tk, tk)
        half = tq // 2
        spans = ((0, half), (half, tq))
        for n, (q, cols, _) in enumerate(streams):
            for r0, nk in spans:
                s = _nt_dot(q[r0:r0 + half, :], k_ref[pl.ds(start, nk), cols])
                r = r0 + lax.broadcasted_iota(jnp.int32, (half, nk), 0)
                c = lax.broadcasted_iota(jnp.int32, (half, nk), 1)
                s_scr[n * tq + r0:n * tq + r0 + half, 0:nk] = jnp.where(c <= r, s, -jnp.inf)
        for n, (_, _, vcols) in enumerate(streams):
            for r0, nk in spans:
                lo = n * tq + r0
                for rc in range(lo // ROW_CHUNK, (lo + half) // ROW_CHUNK):
                    row_pass(rc, nk // LANES)
                acc_scr[lo:lo + half, :] += _bdot(p_scr[lo:lo + half, 0:nk],
                                                  v_ref[pl.ds(start, nk), vcols])

    ratio = tq // tk

    def body(j, _):
        step(j, None)
        return 0

    lax.fori_loop(0, ratio * i, body, 0)
    if ratio == 1 and tq % (2 * max(ROW_CHUNK, LANES)) == 0:
        diagonal_step(i)
    else:
        for d in range(ratio):
            step(ratio * i + d, -d * tk)
    return acc_scr[...], l_scr[...].sum(axis=1, keepdims=True)


def _softmax_scratch(n_rows, tk, v_width):
    return [pltpu.VMEM((n_rows, tk), F32), pltpu.VMEM((n_rows, tk), BF16),
            pltpu.VMEM((n_rows, LANES), F32), pltpu.VMEM((n_rows, LANES), F32),
            pltpu.VMEM((n_rows, v_width), F32)]


def _diff_kernel(q_ref, k_ref, v_ref, lam_ref, sub_ref, o_ref, *scratch, lambda_init, tk):
    tq = q_ref.shape[0]
    i = pl.program_id(2)
    n_heads = q_ref.shape[1] // LANES
    streams = []
    for hd in range(n_heads):
        cols = slice(hd * LANES, (hd + 1) * LANES)
        for qh in _split_head_pair(q_ref[:, cols]):
            streams.append((qh, cols, cols))
    acc, l = _softmax_sweep(streams, k_ref, v_ref, scratch, i, tq, tk)
    o = acc / l
    lp = lam_ref[...]
    lam = (jnp.exp(jnp.sum(lp[0:1, :] * lp[1:2, :], axis=1, keepdims=True))
           - jnp.exp(jnp.sum(lp[2:3, :] * lp[3:4, :], axis=1, keepdims=True)) + lambda_init)
    for hd in range(n_heads):
        r0 = 2 * hd * tq
        od = o[r0:r0 + tq, :] - lam * o[r0 + tq:r0 + 2 * tq, :]
        o_ref[:, hd * LANES:(hd + 1) * LANES] = (
            _rms_rows(od, sub_ref[...]) * (1.0 - lambda_init)).astype(o_ref.dtype)


def _fox_kernel(q_ref, k_ref, v_ref, o_ref, *scratch, tk):
    tq = q_ref.shape[0]
    i = pl.program_id(2)
    n_heads = q_ref.shape[1] // LANES
    streams = [(q_ref[:, hd * LANES:(hd + 1) * LANES], slice(hd * LANES, (hd + 1) * LANES),
                slice((hd // 2) * LANES, (hd // 2 + 1) * LANES)) for hd in range(n_heads)]
    acc, l = _softmax_sweep(streams, k_ref, v_ref, scratch, i, tq, tk)
    o = acc / l
    lane = lax.broadcasted_iota(jnp.int32, (tq, LANES), 1)
    for pair in range(n_heads // 2):
        r0 = 2 * pair * tq
        o_ref[:, pair * LANES:(pair + 1) * LANES] = jnp.where(
            lane < HEAD_DIM, o[r0:r0 + tq, :], o[r0 + tq:r0 + 2 * tq, :]).astype(o_ref.dtype)


def _sb_kernel(q_ref, k_ref, v_ref, o_ref, z_scr, sp_scr, c_scr, w_scr, first_scr, later_scr, acc_scr,
               *, tk):
    tq = q_ref.shape[0]
    i = pl.program_id(2)
    ratio = tq // tk
    n_pairs = q_ref.shape[1] // LANES
    heads = []
    for pair in range(n_pairs):
        kv_cols = slice(pair * LANES, (pair + 1) * LANES)
        heads += [(qh, kv_cols) for qh in _split_head_pair(q_ref[:, kv_cols])]
    sub = min(SB_SUM_BLOCK, tk)
    n_sub = tk // sub
    lanes_per_sub = sub // LANES
    neg_later = jnp.where(lax.broadcasted_iota(jnp.int32, (sub, sub), 0)
                          > lax.broadcasted_iota(jnp.int32, (sub, sub), 1), -1.0, 0.0).astype(BF16)
    later_scr[...] = jnp.zeros(later_scr.shape, F32)
    acc_scr[...] = jnp.zeros(acc_scr.shape, F32)

    def step(j, limit):
        start = pl.multiple_of(j * tk, tk)
        for n, (qh, kv_cols) in enumerate(heads):
            z_scr[n * tq:(n + 1) * tq, :] = _nt_dot(qh, k_ref[pl.ds(start, tk), kv_cols])
        for n, (_, kv_cols) in enumerate(heads):
            head_rows = slice(n * tq, (n + 1) * tq)
            chunks = range(n * tq // ROW_CHUNK, (n + 1) * tq // ROW_CHUNK)
            for rc in chunks:
                rows = slice(rc * ROW_CHUNK, (rc + 1) * ROW_CHUNK)
                z = z_scr[rows, :]
                neg_abs = lax.bitcast_convert_type(
                    lax.bitcast_convert_type(z, jnp.int32) | jnp.int32(-2 ** 31), F32)
                sp = jnp.maximum(z, 0.0) + jnp.log2(1.0 + jnp.exp2(neg_abs))
                if limit is not None:
                    sp = jnp.where(_key_minus_query(rc * ROW_CHUNK, ROW_CHUNK, tq, tk) < limit, sp, 0.0)
                sp_scr[rows, :] = sp.astype(BF16)
                z_scr[rows, :] = z - sp
                for b in range(n_sub):
                    first_scr[rows, b * LANES:(b + 1) * LANES] = sp[:, b * sub:b * sub + LANES]
            for b in range(n_sub):
                c_scr[head_rows, b * sub:(b + 1) * sub] = _bdot(
                    sp_scr[head_rows, b * sub:(b + 1) * sub], neg_later)
            for rc in chunks:
                rows = slice(rc * ROW_CHUNK, (rc + 1) * ROW_CHUNK)
                later = later_scr[rows, :]
                for b in reversed(range(n_sub)):
                    for c in range(b * lanes_per_sub, (b + 1) * lanes_per_sub):
                        cols = slice(c * LANES, (c + 1) * LANES)
                        w = jnp.exp2(z_scr[rows, cols] + c_scr[rows, cols] + later)
                        if limit is not None:
                            keep = _key_minus_query(rc * ROW_CHUNK, ROW_CHUNK, tq, tk)[:, cols] < limit
                            w = jnp.where(keep, w, 0.0)
                        w_scr[rows, cols] = w.astype(BF16)
                    later = later + (c_scr[rows, b * sub:b * sub + 1]
                                     - first_scr[rows, b * LANES:b * LANES + 1])
                later_scr[rows, :] = later
            acc_scr[head_rows, :] += _bdot(w_scr[head_rows, :], v_ref[pl.ds(start, tk), kv_cols])

    for d in reversed(range(ratio)):
        step(ratio * i + d, -d * tk)

    def body(n, _):
        step(ratio * i - 1 - n, None)
        return 0

    lax.fori_loop(0, ratio * i, body, 0)
    lane = lax.broadcasted_iota(jnp.int32, (tq, LANES), 1)
    for pair in range(n_pairs):
        r0 = 2 * pair * tq
        o_ref[:, pair * LANES:(pair + 1) * LANES] = jnp.where(
            lane < HEAD_DIM, acc_scr[r0:r0 + tq, :], acc_scr[r0 + tq:r0 + 2 * tq, :]).astype(o_ref.dtype)


def _sb_scratch(n_rows, tk):
    n_sub = tk // min(SB_SUM_BLOCK, tk)
    return [pltpu.VMEM((n_rows, tk), F32), pltpu.VMEM((n_rows, tk), BF16),
            pltpu.VMEM((n_rows, tk), F32), pltpu.VMEM((n_rows, tk), BF16),
            pltpu.VMEM((n_rows, n_sub * LANES), F32),
            pltpu.VMEM((n_rows, LANES), F32), pltpu.VMEM((n_rows, LANES), F32)]


def _attention_call(body, name, q, k, v, extra, batch, seq, n_groups, tk, scratch_fn,
                    kv_buffers=2):
    n = q.shape[0]
    tq = min(seq, ATTN_Q_BLOCK)
    tk = min(tk, tq)
    nq = seq // tq
    q_width, v_width = q.shape[1] // n_groups, v.shape[1] // n_groups
    kv_mode = pl.Buffered(kv_buffers) if kv_buffers != 2 else None
    return pl.pallas_call(
        functools.partial(body, tk=tk),
        grid=(batch, n_groups, nq),
        in_specs=[pl.BlockSpec((tq, q_width), lambda b, g, i: (b * nq + i, g)),
                  pl.BlockSpec((seq, q_width), lambda b, g, i: (b, g), pipeline_mode=kv_mode),
                  pl.BlockSpec((seq, v_width), lambda b, g, i: (b, g), pipeline_mode=kv_mode)]
                 + [_full(e.shape) for e in extra],
        out_specs=pl.BlockSpec((tq, v_width), lambda b, g, i: (b * nq + i, g)),
        out_shape=jax.ShapeDtypeStruct((n, v.shape[1]), BF16),
        scratch_shapes=scratch_fn(2 * tq * v_width // LANES, tk),
        compiler_params=_cparams(3),
        name=name,
    )(q, k, v, *extra)


def _merge_tile(x_ref, u_ref, halo_ref, od_ref, os_ref, of_ref, an_ref, pw_ref, ps_ref,
                wbr_ref, wg_ref, bg_ref, wout_ref):
    tm, d = x_ref.shape
    x = x_ref[...]
    h = _rms_rows(x, an_ref[...]).astype(BF16)

    u = u_ref[...]
    first_tile = pl.program_id(1) == 0
    halo = jnp.where(first_tile, 0.0, halo_ref[...])
    ext = jnp.concatenate([halo, u], axis=0)
    lane = lax.broadcasted_iota(jnp.int32, (tm, POOL_WIDTH), 1)
    group = lane >> 6
    win = jnp.zeros_like(u)
    width = jnp.zeros((tm, POOL_WIDTH), jnp.int32)
    span = 1
    for g, w in enumerate(POOL_WINDOWS):
        while span < w:
            ext = ext + pltpu.roll(ext, span, 0)
            span *= 2
        win = jnp.where(group == g, ext[POOL_HALO:, :], win)
        width = jnp.where(group == g, w, width)
    pos = pl.program_id(1) * tm + lax.broadcasted_iota(jnp.int32, (tm, POOL_WIDTH), 0)
    count = jnp.minimum(pos + 1, width).astype(F32)
    pooled = win / count - u
    o_pool = (_bdot(pooled.astype(BF16), pw_ref[...]) * ps_ref[...]).astype(BF16)

    merged = jnp.zeros((tm, d), F32)
    lo = 0
    for nb, o in enumerate((o_pool, od_ref[...], os_ref[...], of_ref[...])):
        width_n = BRANCH_WIDTHS[nb]
        y = _bdot(o, wbr_ref[lo:lo + width_n, :])
        gate = jax.nn.sigmoid(_nt_dot(h, wg_ref[nb * d:(nb + 1) * d, :]) + bg_ref[nb:nb + 1, :])
        merged = merged + gate * y
        lo += width_n
    return x + _bdot(merged.astype(BF16), wout_ref[...])


def _ffn_tile(x, fn_ref, wup_ref, wdn_ref, chunk):
    h = _rms_rows(x, fn_ref[...]).astype(BF16)
    d_ff = wdn_ref.shape[0]
    acc = x
    for c in range(d_ff // chunk):
        gate = _bdot(h, wup_ref[:, c * chunk:(c + 1) * chunk])
        up = _bdot(h, wup_ref[:, d_ff + c * chunk:d_ff + (c + 1) * chunk])
        act = (gate * jax.nn.sigmoid(gate) * up).astype(BF16)
        acc = acc + _bdot(act, wdn_ref[c * chunk:(c + 1) * chunk, :])
    return acc


def _merge_ffn_kernel(*refs, chunk):
    *merge_refs, fn_ref, wup_ref, wdn_ref, o_ref = refs
    o_ref[...] = _ffn_tile(_merge_tile(*merge_refs), fn_ref, wup_ref, wdn_ref, chunk)


def _merge_ffn_call(xf, u, od, os_, of, an, pw, ps, wbr, wg, bg, wout, fn, wup, wdn, batch, seq):
    n, d = xf.shape
    tm = min(seq, ROW_TILE)
    ns = seq // tm
    row = lambda b, s: (b * ns + s, 0)
    halo_blocks = tm // POOL_HALO
    halo = lambda b, s: (jnp.maximum((b * ns + s) * halo_blocks - 1, 0), 0)
    once = lambda a: pl.BlockSpec(a.shape, lambda b, s: (0,) * a.ndim, pipeline_mode=pl.Buffered(1))
    return pl.pallas_call(
        functools.partial(_merge_ffn_kernel, chunk=256),
        grid=(batch, ns),
        in_specs=[pl.BlockSpec((tm, d), row), pl.BlockSpec((tm, POOL_WIDTH), row),
                  pl.BlockSpec((POOL_HALO, POOL_WIDTH), halo),
                  pl.BlockSpec((tm, DIFF_WIDTH), row), pl.BlockSpec((tm, SB_WIDTH), row),
                  pl.BlockSpec((tm, FOX_WIDTH), row),
                  _full(an.shape), once(pw), _full(ps.shape), once(wbr), once(wg), _full(bg.shape),
                  once(wout), _full(fn.shape), once(wup), once(wdn)],
        out_specs=pl.BlockSpec((tm, d), row),
        out_shape=jax.ShapeDtypeStruct((n, d), F32),
        compiler_params=_cparams(2),
        name="merge_and_ffn",
    )(xf, u, u, od, os_, of, an, pw, ps, wbr, wg, bg, wout, fn, wup, wdn)


def _group_mean_matrix():
    g = np.kron(np.eye(256 // HEAD_DIM), np.full((HEAD_DIM, HEAD_DIM), 1.0 / HEAD_DIM))
    return jnp.asarray(g, BF16)


def _lower_tri(nrows):
    return jnp.asarray(np.tril(np.ones((nrows, nrows))), BF16)


def _decay_selector():
    sel = np.zeros((LANES, 2 * FOX_HEADS * LANES), np.float32)
    for hd in range(FOX_HEADS):
        qb = hd * LANES + AUG_LANE
        kb = (FOX_HEADS + hd) * LANES + AUG_LANE
        for term in range(3):
            sel[4 * term + hd, qb + term] = 1.0
            sel[12, qb + 3 + term] = 1.0
            sel[12, kb + term] = 1.0
            sel[4 * term + hd, kb + 3 + term] = -1.0
    return jnp.asarray(sel, BF16)


def kernel(x, attn_norm, ffn_norm, w_in, b_gate, b_forget, pool_w, pool_scale, diff_q_norm,
           diff_k_norm, diff_subln, lam_q1, lam_k1, lam_q2, lam_k2, fox_q_norm, fox_k_norm,
           w_branch, w_out, w_ffn_up, w_ffn_down):
    batch, seq, d = x.shape
    depth = attn_norm.shape[0]
    assert seq % min(seq, ROW_TILE) == 0 and seq % min(seq, ATTN_Q_BLOCK) == 0
    xf = x.reshape(batch * seq, d)
    cos, sin = _rope_tables(seq)
    gmat = _group_mean_matrix()
    ltri = _lower_tri(min(seq, 256))
    sel = _decay_selector()
    softmax_scratch = functools.partial(_softmax_scratch, v_width=LANES)

    for layer in range(depth):
        wt = jnp.swapaxes(w_in[layer], 0, 1)
        wf = wt[MAIN_WIDTH:MAIN_WIDTH + FOX_HEADS]
        wf_pad = jnp.concatenate([wf, wf, wf, jnp.zeros((LANES - 3 * FOX_HEADS, d), wt.dtype)], axis=0)
        bfg = b_forget[layer]
        bf_pad = jnp.concatenate([bfg, bfg, bfg, jnp.zeros((LANES - 3 * FOX_HEADS,), F32)])[None, :]
        gains = jnp.stack([jnp.tile(g[layer], 256 // HEAD_DIM)
                           for g in (diff_q_norm, diff_k_norm, fox_q_norm, fox_k_norm)])
        an = attn_norm[layer][None, :]

        u, dq, dk, dv, sq, sk, sv, fq, fk, fv = _proj_call(
            xf, an, wt[:MAIN_WIDTH].astype(BF16), wf_pad.astype(BF16), bf_pad, cos, sin, gains, gmat,
            ltri, sel, batch, seq)

        lam = jnp.stack([lam_q1[layer], lam_k1[layer], lam_q2[layer], lam_k2[layer]])
        od = _attention_call(
            functools.partial(_diff_kernel, lambda_init=_diff_lambda_init(layer)), "diff_attention",
            dq, dk, dv, (lam, diff_subln[layer][None, :]), batch, seq, DIFF_GROUPS,
            SOFTMAX_K_BLOCK, softmax_scratch, kv_buffers=1)
        os_ = _attention_call(_sb_kernel, "stick_breaking_attention", sq, sk, sv, (), batch, seq,
                              1, SB_K_BLOCK, _sb_scratch)
        of = _attention_call(_fox_kernel, "forgetting_attention", fq, fk, fv, (), batch, seq,
                             1, SOFTMAX_K_BLOCK, softmax_scratch)

        pw = jax.scipy.linalg.block_diag(*[pool_w[layer, g] for g in range(len(POOL_WINDOWS))])
        xf = _merge_ffn_call(xf, u, od, os_, of, an, pw.astype(BF16), pool_scale[layer][None, :],
                             w_branch[layer].astype(BF16), wt[MAIN_WIDTH + FOX_HEADS:].astype(BF16),
                             b_gate[layer], w_out[layer].astype(BF16), ffn_norm[layer][None, :],
                             w_ffn_up[layer].astype(BF16), w_ffn_down[layer].astype(BF16), batch, seq)
    return xf.reshape(batch, seq, d)
```

```python
import functools
import math

import numpy as np
import jax
import jax.numpy as jnp
from jax import lax
from jax.experimental import pallas as pl
from jax.experimental.pallas import tpu as pltpu

F32 = jnp.float32
BF16 = jnp.bfloat16

HEAD_DIM = 64
POOL_WINDOWS = (2, 4, 8, 16)
POOL_WIDTH = 256
DIFF_WIDTH = 512
SB_WIDTH = 256
FOX_HEADS = 4
FOX_WIDTH = 256
BRANCH_WIDTHS = (POOL_WIDTH, DIFF_WIDTH, SB_WIDTH, FOX_WIDTH)
MAIN_WIDTH = 3328
ROPE_THETA = 10000.0
NORM_EPS = 1e-6
QK_SCALE = HEAD_DIM ** -0.5
LOG2E = math.log2(math.e)
Q_SCALE = QK_SCALE * LOG2E

LANES = 128
ATTN_Q_BLOCK = 512
SOFTMAX_K_BLOCK = 512
SB_K_BLOCK = 512
SB_SUM_BLOCK = 256
DIFF_GROUPS = 1
ROW_TILE = 512
ROW_CHUNK = 128
POOL_HALO = 16
AUG_LANE = 64
VMEM_LIMIT = 56 * 1024 * 1024


def _diff_lambda_init(layer):
    return 0.8 - 0.6 * math.exp(-0.3 * layer)


def _cparams(n_axes):
    return pltpu.CompilerParams(dimension_semantics=("arbitrary",) * n_axes,
                                vmem_limit_bytes=VMEM_LIMIT)


def _full(shape):
    return pl.BlockSpec(shape, lambda *_: (0,) * len(shape))


def _rms_rows(x, gain):
    ms = jnp.mean(x * x, axis=-1, keepdims=True)
    return x * lax.rsqrt(ms + NORM_EPS) * gain


def _bdot(a, b):
    return jnp.dot(a, b, preferred_element_type=F32)


def _nt_dot(a, b):
    return lax.dot_general(a, b, (((1,), (1,)), ((), ())), preferred_element_type=F32)


def _split_head_pair(q):
    lane = lax.broadcasted_iota(jnp.int32, q.shape, 1)
    qf = q.astype(F32)
    return (jnp.where(lane < HEAD_DIM, qf, 0.0).astype(q.dtype),
            jnp.where(lane >= HEAD_DIM, qf, 0.0).astype(q.dtype))


def _split3(v):
    a1 = v.astype(BF16)
    r1 = v - a1.astype(F32)
    a2 = r1.astype(BF16)
    r2 = r1 - a2.astype(F32)
    a3 = r2.astype(BF16)
    return a1, a2, a3


def _rope_table_kernel(inv_ref, cos_ref, sin_ref):
    rows = cos_ref.shape[0]
    base = pl.program_id(0) * rows
    pos = (base + lax.broadcasted_iota(jnp.int32, (rows, LANES), 0)).astype(F32)
    ang = pos * inv_ref[...]
    lane = lax.broadcasted_iota(jnp.int32, (rows, LANES), 1)
    first_half = (lane & (HEAD_DIM // 2)) == 0
    s = jnp.sin(ang)
    cos_ref[...] = jnp.cos(ang)
    sin_ref[...] = jnp.where(first_half, -s, s)


def _rope_tables(seq):
    half = HEAD_DIM // 2
    inv_freq = ROPE_THETA ** (-jnp.arange(half, dtype=F32) / half)
    inv = jnp.tile(inv_freq, LANES // half)[None, :]
    rows = min(seq, ROW_TILE)
    return pl.pallas_call(
        _rope_table_kernel,
        grid=(seq // rows,),
        in_specs=[_full((1, LANES))],
        out_specs=[pl.BlockSpec((rows, LANES), lambda i: (i, 0))] * 2,
        out_shape=[jax.ShapeDtypeStruct((seq, LANES), F32)] * 2,
        compiler_params=_cparams(1),
        name="rope_tables",
    )(inv)


def _proj_kernel(x_ref, an_ref, w_ref, wf_ref, bf_ref, cos_ref, sin_ref, gains_ref, gmat_ref,
                 ltri_ref, sel_ref,
                 u_ref, dq_ref, dk_ref, dv_ref, sq_ref, sk_ref, sv_ref, fq_ref, fk_ref, fv_ref,
                 carry_ref):
    tm = x_ref.shape[0]
    h = _rms_rows(x_ref[...], an_ref[...]).astype(BF16)

    def proj(lo, width):
        return _nt_dot(h, w_ref[lo:lo + width, :])

    def group_norm(y, gain):
        ms = _bdot((y * y).astype(BF16), gmat_ref[...])
        return y * lax.rsqrt(ms + NORM_EPS) * gain

    cos = cos_ref[...]
    sin = sin_ref[...]
    lane = lax.broadcasted_iota(jnp.int32, (tm, LANES), 1)
    first_half = (lane & (HEAD_DIM // 2)) == 0
    low_head = lane < HEAD_DIM

    def rope(y):
        partner = jnp.where(first_half, pltpu.roll(y, LANES - HEAD_DIM // 2, 1),
                            pltpu.roll(y, HEAD_DIM // 2, 1))
        return y * cos + partner * sin

    u_ref[...] = proj(0, 256)

    gq = gains_ref[0:1, :]
    gk = gains_ref[1:2, :]
    for c in range(2):
        yq = group_norm(proj(256 + 256 * c, 256), gq)
        yk = group_norm(proj(768 + 256 * c, 256), gk)
        for hh in range(2):
            sl = slice(hh * LANES, (hh + 1) * LANES)
            col = slice(256 * c + hh * LANES, 256 * c + (hh + 1) * LANES)
            dq_ref[:, col] = (rope(yq[:, sl]) * Q_SCALE).astype(BF16)
            dk_ref[:, col] = rope(yk[:, sl]).astype(BF16)
        dv_ref[:, 256 * c:256 * (c + 1)] = proj(1280 + 256 * c, 256).astype(BF16)

    sq_ref[...] = (proj(1792, 256) * Q_SCALE).astype(BF16)
    sk_ref[...] = proj(2048, 256).astype(BF16)
    sv_ref[...] = proj(2304, 256).astype(BF16)
    fv_ref[...] = proj(3072, 256).astype(BF16)

    fl = _nt_dot(h, wf_ref[...]) + bf_ref[...]
    logf = jnp.minimum(fl, 0.0) - jnp.log1p(jnp.exp(-jnp.abs(fl)))

    @pl.when(pl.program_id(1) == 0)
    def _():
        carry_ref[...] = jnp.zeros_like(carry_ref)

    carry = carry_ref[0:1, :]
    sub = ltri_ref.shape[0]
    pieces = []
    for sb in range(tm // sub):
        a1, a2, a3 = _split3(logf[sb * sub:(sb + 1) * sub, :])
        ltri = ltri_ref[...]
        cum = _bdot(ltri, a1) + _bdot(ltri, a2) + _bdot(ltri, a3) + carry
        carry = cum[sub - 1:sub, :]
        pieces.append(cum)
    carry_ref[0:1, :] = carry
    cum = jnp.concatenate(pieces, axis=0) if len(pieces) > 1 else pieces[0]

    c1, c2, c3 = (c.astype(F32) for c in _split3(cum * LOG2E))
    packed = jnp.where(lane < 4, c1, jnp.where(lane < 8, c2, jnp.where(lane < 12, c3,
                       jnp.where(lane == 12, 1.0, 0.0))))
    aug = _bdot(packed.astype(BF16), sel_ref[...])

    yq = group_norm(proj(2560, 256), gains_ref[2:3, :]) * Q_SCALE
    yk = group_norm(proj(2816, 256), gains_ref[3:4, :])
    for src, dst, off in ((yq, fq_ref, 0), (yk, fk_ref, 4 * LANES)):
        for pair in range(2):
            v = src[:, pair * LANES:(pair + 1) * LANES]
            heads = (v, pltpu.roll(v, HEAD_DIM, 1))
            for hh in range(2):
                col = (2 * pair + hh) * LANES
                dst[:, col:col + LANES] = jnp.where(
                    low_head, heads[hh], aug[:, off + col:off + col + LANES]).astype(BF16)


def _proj_call(xf, an, w_main, wf_pad, bf_pad, cos, sin, gains, gmat, ltri, sel, batch, seq):
    n, d = xf.shape
    tm = min(seq, ROW_TILE)
    ns = seq // tm
    row = lambda b, s: (b * ns + s, 0)
    widths = (256, 512, 512, 512, 256, 256, 256, 512, 512, 256)
    dtypes = (F32,) + (BF16,) * 9
    return pl.pallas_call(
        _proj_kernel,
        grid=(batch, ns),
        in_specs=[pl.BlockSpec((tm, d), row), _full(an.shape), _full(w_main.shape),
                  _full(wf_pad.shape), _full(bf_pad.shape),
                  pl.BlockSpec((tm, LANES), lambda b, s: (s, 0)),
                  pl.BlockSpec((tm, LANES), lambda b, s: (s, 0)),
                  _full(gains.shape), _full(gmat.shape), _full(ltri.shape), _full(sel.shape)],
        out_specs=[pl.BlockSpec((tm, w), row) for w in widths],
        out_shape=[jax.ShapeDtypeStruct((n, w), t) for w, t in zip(widths, dtypes)],
        scratch_shapes=[pltpu.VMEM((8, LANES), F32)],
        compiler_params=_cparams(2),
        name="norm_in_proj",
    )(xf, an, w_main, wf_pad, bf_pad, cos, sin, gains, gmat, ltri, sel)


def _key_minus_query(row0, rows, tq, tk):
    r = (row0 + lax.broadcasted_iota(jnp.int32, (rows, tk), 0)) & (tq - 1)
    c = lax.broadcasted_iota(jnp.int32, (rows, tk), 1)
    return c - r


def _softmax_sweep(streams, k_ref, v_ref, scratch, i, tq, tk):
    s_scr, p_scr, m_scr, l_scr, acc_scr = scratch
    n_lane_chunks = tk // LANES
    m_scr[...] = jnp.full(m_scr.shape, -1e30, F32)
    l_scr[...] = jnp.zeros(l_scr.shape, F32)
    acc_scr[...] = jnp.zeros(acc_scr.shape, F32)

    def row_pass(rc, n_chunks):
        rows = slice(rc * ROW_CHUNK, (rc + 1) * ROW_CHUNK)
        mx = s_scr[rows, 0:LANES]
        for c in range(1, n_chunks):
            mx = jnp.maximum(mx, s_scr[rows, c * LANES:(c + 1) * LANES])
        m_old = m_scr[rows, :]
        m_new = jnp.maximum(m_old, mx.max(axis=1, keepdims=True))
        alpha = jnp.exp2(m_old - m_new)
        m_scr[rows, :] = m_new
        psum = None
        for c in range(n_chunks):
            p = jnp.exp2(s_scr[rows, c * LANES:(c + 1) * LANES] - m_new)
            psum = p if psum is None else psum + p
            p_scr[rows, c * LANES:(c + 1) * LANES] = p.astype(BF16)
        l_scr[rows, :] = alpha * l_scr[rows, :] + psum
        acc_scr[rows, :] = alpha * acc_scr[rows, :]

    def step(j, limit):
        start = pl.multiple_of(j * tk, tk)
        for n, (q, cols, _) in enumerate(streams):
            s = _nt_dot(q, k_ref[pl.ds(start, tk), cols])
            if limit is not None:
                s = jnp.where(_key_minus_query(0, tq, tq, tk) <= limit, s, -jnp.inf)
            s_scr[n * tq:(n + 1) * tq, :] = s
        for n, (_, _, vcols) in enumerate(streams):
            for rc in range(n * tq // ROW_CHUNK, (n + 1) * tq // ROW_CHUNK):
                row_pass(rc, n_lane_chunks)
            acc_scr[n * tq:(n + 1) * tq, :] += _bdot(p_scr[n * tq:(n + 1) * tq, :],
                                                     v_ref[pl.ds(start, tk), vcols])

    def diagonal_step(j):
        start = pl.multiple_of(j * tk, tk)
        half = tq // 2
        spans = ((0, half), (half, tq))
        for n, (q, cols, _) in enumerate(streams):
            for r0, nk in spans:
                s = _nt_dot(q[r0:r0 + half, :], k_ref[pl.ds(start, nk), cols])
                r = r0 + lax.broadcasted_iota(jnp.int32, (half, nk), 0)
                c = lax.broadcasted_iota(jnp.int32, (half, nk), 1)
                s_scr[n * tq + r0:n * tq + r0 + half, 0:nk] = jnp.where(c <= r, s, -jnp.inf)
        for n, (_, _, vcols) in enumerate(streams):
            for r0, nk in spans:
                lo = n * tq + r0
                for rc in range(lo // ROW_CHUNK, (lo + half) // ROW_CHUNK):
                    row_pass(rc, nk // LANES)
                acc_scr[lo:lo + half, :] += _bdot(p_scr[lo:lo + half, 0:nk],
                                                  v_ref[pl.ds(start, nk), vcols])

    ratio = tq // tk

    def body(j, _):
        step(j, None)
        return 0

    lax.fori_loop(0, ratio * i, body, 0)
    if ratio == 1 and tq % (2 * max(ROW_CHUNK, LANES)) == 0:
        diagonal_step(i)
    else:
        for d in range(ratio):
            step(ratio * i + d, -d * tk)
    return acc_scr[...], l_scr[...].sum(axis=1, keepdims=True)


def _softmax_scratch(n_rows, tk, v_width):
    return [pltpu.VMEM((n_rows, tk), F32), pltpu.VMEM((n_rows, tk), BF16),
            pltpu.VMEM((n_rows, LANES), F32), pltpu.VMEM((n_rows, LANES), F32),
            pltpu.VMEM((n_rows, v_width), F32)]


def _diff_kernel(q_ref, k_ref, v_ref, lam_ref, sub_ref, o_ref, *scratch, lambda_init, tk):
    tq = q_ref.shape[0]
    i = pl.program_id(2)
    n_heads = q_ref.shape[1] // LANES
    streams = []
    for hd in range(n_heads):
        cols = slice(hd * LANES, (hd + 1) * LANES)
        for qh in _split_head_pair(q_ref[:, cols]):
            streams.append((qh, cols, cols))
    acc, l = _softmax_sweep(streams, k_ref, v_ref, scratch, i, tq, tk)
    o = acc / l
    lp = lam_ref[...]
    lam = (jnp.exp(jnp.sum(lp[0:1, :] * lp[1:2, :], axis=1, keepdims=True))
           - jnp.exp(jnp.sum(lp[2:3, :] * lp[3:4, :], axis=1, keepdims=True)) + lambda_init)
    for hd in range(n_heads):
        r0 = 2 * hd * tq
        od = o[r0:r0 + tq, :] - lam * o[r0 + tq:r0 + 2 * tq, :]
        o_ref[:, hd * LANES:(hd + 1) * LANES] = (
            _rms_rows(od, sub_ref[...]) * (1.0 - lambda_init)).astype(o_ref.dtype)


def _fox_kernel(q_ref, k_ref, v_ref, o_ref, *scratch, tk):
    tq = q_ref.shape[0]
    i = pl.program_id(2)
    n_heads = q_ref.shape[1] // LANES
    streams = [(q_ref[:, hd * LANES:(hd + 1) * LANES], slice(hd * LANES, (hd + 1) * LANES),
                slice((hd // 2) * LANES, (hd // 2 + 1) * LANES)) for hd in range(n_heads)]
    acc, l = _softmax_sweep(streams, k_ref, v_ref, scratch, i, tq, tk)
    o = acc / l
    lane = lax.broadcasted_iota(jnp.int32, (tq, LANES), 1)
    for pair in range(n_heads // 2):
        r0 = 2 * pair * tq
        o_ref[:, pair * LANES:(pair + 1) * LANES] = jnp.where(
            lane < HEAD_DIM, o[r0:r0 + tq, :], o[r0 + tq:r0 + 2 * tq, :]).astype(o_ref.dtype)


def _sb_kernel(q_ref, k_ref, v_ref, o_ref, z_scr, sp_scr, c_scr, w_scr, first_scr, later_scr, acc_scr,
               *, tk):
    tq = q_ref.shape[0]
    i = pl.program_id(2)
    ratio = tq // tk
    n_pairs = q_ref.shape[1] // LANES
    heads = []
    for pair in range(n_pairs):
        kv_cols = slice(pair * LANES, (pair + 1) * LANES)
        heads += [(qh, kv_cols) for qh in _split_head_pair(q_ref[:, kv_cols])]
    sub = min(SB_SUM_BLOCK, tk)
    n_sub = tk // sub
    lanes_per_sub = sub // LANES
    neg_later = jnp.where(lax.broadcasted_iota(jnp.int32, (sub, sub), 0)
                          > lax.broadcasted_iota(jnp.int32, (sub, sub), 1), -1.0, 0.0).astype(BF16)
    later_scr[...] = jnp.zeros(later_scr.shape, F32)
    acc_scr[...] = jnp.zeros(acc_scr.shape, F32)

    def step(j, limit, spans=((0, tq, tk),)):
        start = pl.multiple_of(j * tk, tk)
        for n, (qh, kv_cols) in enumerate(heads):
            for r0, nr, nk in spans:
                z_scr[n * tq + r0:n * tq + r0 + nr, 0:nk] = _nt_dot(
                    qh[r0:r0 + nr, :], k_ref[pl.ds(start, nk), kv_cols])
        for n, (_, kv_cols) in enumerate(heads):
            for r0, nr, nk in spans:
                span_rows = slice(n * tq + r0, n * tq + r0 + nr)
                chunks = range((n * tq + r0) // ROW_CHUNK, (n * tq + r0 + nr) // ROW_CHUNK)
                subs = range(nk // sub)
                for rc in chunks:
                    rows = slice(rc * ROW_CHUNK, (rc + 1) * ROW_CHUNK)
                    z = z_scr[rows, 0:nk]
                    neg_abs = lax.bitcast_convert_type(
                        lax.bitcast_convert_type(z, jnp.int32) | jnp.int32(-2 ** 31), F32)
                    sp = jnp.maximum(z, 0.0) + jnp.log2(1.0 + jnp.exp2(neg_abs))
                    if limit is not None:
                        sp = jnp.where(_key_minus_query(rc * ROW_CHUNK, ROW_CHUNK, tq, nk) < limit,
                                       sp, 0.0)
                    sp_scr[rows, 0:nk] = sp.astype(BF16)
                    z_scr[rows, 0:nk] = z - sp
                    for b in subs:
                        first_scr[rows, b * LANES:(b + 1) * LANES] = sp[:, b * sub:b * sub + LANES]
                for b in subs:
                    c_scr[span_rows, b * sub:(b + 1) * sub] = _bdot(
                        sp_scr[span_rows, b * sub:(b + 1) * sub], neg_later)
                for rc in chunks:
                    rows = slice(rc * ROW_CHUNK, (rc + 1) * ROW_CHUNK)
                    later = later_scr[rows, :]
                    for b in reversed(subs):
                        for c in range(b * lanes_per_sub, (b + 1) * lanes_per_sub):
                            cols = slice(c * LANES, (c + 1) * LANES)
                            w = jnp.exp2(z_scr[rows, cols] + c_scr[rows, cols] + later)
                            if limit is not None:
                                keep = _key_minus_query(rc * ROW_CHUNK, ROW_CHUNK, tq, nk)[:, cols] < limit
                                w = jnp.where(keep, w, 0.0)
                            w_scr[rows, cols] = w.astype(BF16)
                        later = later + (c_scr[rows, b * sub:b * sub + 1]
                                         - first_scr[rows, b * LANES:b * LANES + 1])
                    later_scr[rows, :] = later
                acc_scr[span_rows, :] += _bdot(w_scr[span_rows, 0:nk],
                                               v_ref[pl.ds(start, nk), kv_cols])

    half = tq // 2
    if ratio == 1 and half % sub == 0 and half % ROW_CHUNK == 0:
        step(i, 0, ((0, half, half), (half, half, tq)))
    else:
        for d in reversed(range(ratio)):
            step(ratio * i + d, -d * tk)

    def body(n, _):
        step(ratio * i - 1 - n, None)
        return 0

    lax.fori_loop(0, ratio * i, body, 0)
    lane = lax.broadcasted_iota(jnp.int32, (tq, LANES), 1)
    for pair in range(n_pairs):
        r0 = 2 * pair * tq
        o_ref[:, pair * LANES:(pair + 1) * LANES] = jnp.where(
            lane < HEAD_DIM, acc_scr[r0:r0 + tq, :], acc_scr[r0 + tq:r0 + 2 * tq, :]).astype(o_ref.dtype)


def _sb_scratch(n_rows, tk):
    n_sub = tk // min(SB_SUM_BLOCK, tk)
    return [pltpu.VMEM((n_rows, tk), F32), pltpu.VMEM((n_rows, tk), BF16),
            pltpu.VMEM((n_rows, tk), F32), pltpu.VMEM((n_rows, tk), BF16),
            pltpu.VMEM((n_rows, n_sub * LANES), F32),
            pltpu.VMEM((n_rows, LANES), F32), pltpu.VMEM((n_rows, LANES), F32)]


def _attention_call(body, name, q, k, v, extra, batch, seq, n_groups, tk, scratch_fn,
                    kv_buffers=2):
    n = q.shape[0]
    tq = min(seq, ATTN_Q_BLOCK)
    tk = min(tk, tq)
    nq = seq // tq
    q_width, v_width = q.shape[1] // n_groups, v.shape[1] // n_groups
    kv_mode = pl.Buffered(kv_buffers) if kv_buffers != 2 else None
    return pl.pallas_call(
        functools.partial(body, tk=tk),
        grid=(batch, n_groups, nq),
        in_specs=[pl.BlockSpec((tq, q_width), lambda b, g, i: (b * nq + i, g)),
                  pl.BlockSpec((seq, q_width), lambda b, g, i: (b, g), pipeline_mode=kv_mode),
                  pl.BlockSpec((seq, v_width), lambda b, g, i: (b, g), pipeline_mode=kv_mode)]
                 + [_full(e.shape) for e in extra],
        out_specs=pl.BlockSpec((tq, v_width), lambda b, g, i: (b * nq + i, g)),
        out_shape=jax.ShapeDtypeStruct((n, v.shape[1]), BF16),
        scratch_shapes=scratch_fn(2 * tq * v_width // LANES, tk),
        compiler_params=_cparams(3),
        name=name,
    )(q, k, v, *extra)


def _merge_tile(x_ref, u_ref, halo_ref, od_ref, os_ref, of_ref, an_ref, pw_ref, ps_ref,
                wbr_ref, wg_ref, bg_ref, wout_ref):
    tm, d = x_ref.shape
    x = x_ref[...]
    h = _rms_rows(x, an_ref[...]).astype(BF16)

    u = u_ref[...]
    first_tile = pl.program_id(1) == 0
    halo = jnp.where(first_tile, 0.0, halo_ref[...])
    ext = jnp.concatenate([halo, u], axis=0)
    lane = lax.broadcasted_iota(jnp.int32, (tm, POOL_WIDTH), 1)
    group = lane >> 6
    win = jnp.zeros_like(u)
    width = jnp.zeros((tm, POOL_WIDTH), jnp.int32)
    span = 1
    for g, w in enumerate(POOL_WINDOWS):
        while span < w:
            ext = ext + pltpu.roll(ext, span, 0)
            span *= 2
        win = jnp.where(group == g, ext[POOL_HALO:, :], win)
        width = jnp.where(group == g, w, width)
    pos = pl.program_id(1) * tm + lax.broadcasted_iota(jnp.int32, (tm, POOL_WIDTH), 0)
    count = jnp.minimum(pos + 1, width).astype(F32)
    pooled = win / count - u
    o_pool = (_bdot(pooled.astype(BF16), pw_ref[...]) * ps_ref[...]).astype(BF16)

    merged = jnp.zeros((tm, d), F32)
    lo = 0
    for nb, o in enumerate((o_pool, od_ref[...], os_ref[...], of_ref[...])):
        width_n = BRANCH_WIDTHS[nb]
        y = _bdot(o, wbr_ref[lo:lo + width_n, :])
        gate = jax.nn.sigmoid(_nt_dot(h, wg_ref[nb * d:(nb + 1) * d, :]) + bg_ref[nb:nb + 1, :])
        merged = merged + gate * y
        lo += width_n
    return x + _bdot(merged.astype(BF16), wout_ref[...])


def _ffn_tile(x, fn_ref, wup_ref, wdn_ref, chunk):
    h = _rms_rows(x, fn_ref[...]).astype(BF16)
    d_ff = wdn_ref.shape[0]
    acc = x
    for c in range(d_ff // chunk):
        gate = _bdot(h, wup_ref[:, c * chunk:(c + 1) * chunk])
        up = _bdot(h, wup_ref[:, d_ff + c * chunk:d_ff + (c + 1) * chunk])
        act = (gate * jax.nn.sigmoid(gate) * up).astype(BF16)
        acc = acc + _bdot(act, wdn_ref[c * chunk:(c + 1) * chunk, :])
    return acc


def _merge_ffn_kernel(*refs, chunk):
    *merge_refs, fn_ref, wup_ref, wdn_ref, o_ref = refs
    o_ref[...] = _ffn_tile(_merge_tile(*merge_refs), fn_ref, wup_ref, wdn_ref, chunk)


def _merge_ffn_call(xf, u, od, os_, of, an, pw, ps, wbr, wg, bg, wout, fn, wup, wdn, batch, seq):
    n, d = xf.shape
    tm = min(seq, ROW_TILE)
    ns = seq // tm
    row = lambda b, s: (b * ns + s, 0)
    halo_blocks = tm // POOL_HALO
    halo = lambda b, s: (jnp.maximum((b * ns + s) * halo_blocks - 1, 0), 0)
    once = lambda a: pl.BlockSpec(a.shape, lambda b, s: (0,) * a.ndim, pipeline_mode=pl.Buffered(1))
    return pl.pallas_call(
        functools.partial(_merge_ffn_kernel, chunk=256),
        grid=(batch, ns),
        in_specs=[pl.BlockSpec((tm, d), row), pl.BlockSpec((tm, POOL_WIDTH), row),
                  pl.BlockSpec((POOL_HALO, POOL_WIDTH), halo),
                  pl.BlockSpec((tm, DIFF_WIDTH), row), pl.BlockSpec((tm, SB_WIDTH), row),
                  pl.BlockSpec((tm, FOX_WIDTH), row),
                  _full(an.shape), once(pw), _full(ps.shape), once(wbr), once(wg), _full(bg.shape),
                  once(wout), _full(fn.shape), once(wup), once(wdn)],
        out_specs=pl.BlockSpec((tm, d), row),
        out_shape=jax.ShapeDtypeStruct((n, d), F32),
        compiler_params=_cparams(2),
        name="merge_and_ffn",
    )(xf, u, u, od, os_, of, an, pw, ps, wbr, wg, bg, wout, fn, wup, wdn)


def _group_mean_matrix():
    g = np.kron(np.eye(256 // HEAD_DIM), np.full((HEAD_DIM, HEAD_DIM), 1.0 / HEAD_DIM))
    return jnp.asarray(g, BF16)


def _lower_tri(nrows):
    return jnp.asarray(np.tril(np.ones((nrows, nrows))), BF16)


def _decay_selector():
    sel = np.zeros((LANES, 2 * FOX_HEADS * LANES), np.float32)
    for hd in range(FOX_HEADS):
        qb = hd * LANES + AUG_LANE
        kb = (FOX_HEADS + hd) * LANES + AUG_LANE
        for term in range(3):
            sel[4 * term + hd, qb + term] = 1.0
            sel[12, qb + 3 + term] = 1.0
            sel[12, kb + term] = 1.0
            sel[4 * term + hd, kb + 3 + term] = -1.0
    return jnp.asarray(sel, BF16)


def kernel(x, attn_norm, ffn_norm, w_in, b_gate, b_forget, pool_w, pool_scale, diff_q_norm,
           diff_k_norm, diff_subln, lam_q1, lam_k1, lam_q2, lam_k2, fox_q_norm, fox_k_norm,
           w_branch, w_out, w_ffn_up, w_ffn_down):
    batch, seq, d = x.shape
    depth = attn_norm.shape[0]
    assert seq % min(seq, ROW_TILE) == 0 and seq % min(seq, ATTN_Q_BLOCK) == 0
    xf = x.reshape(batch * seq, d)
    cos, sin = _rope_tables(seq)
    gmat = _group_mean_matrix()
    ltri = _lower_tri(min(seq, 256))
    sel = _decay_selector()
    softmax_scratch = functools.partial(_softmax_scratch, v_width=LANES)

    for layer in range(depth):
        wt = jnp.swapaxes(w_in[layer], 0, 1)
        wf = wt[MAIN_WIDTH:MAIN_WIDTH + FOX_HEADS]
        wf_pad = jnp.concatenate([wf, wf, wf, jnp.zeros((LANES - 3 * FOX_HEADS, d), wt.dtype)], axis=0)
        bfg = b_forget[layer]
        bf_pad = jnp.concatenate([bfg, bfg, bfg, jnp.zeros((LANES - 3 * FOX_HEADS,), F32)])[None, :]
        gains = jnp.stack([jnp.tile(g[layer], 256 // HEAD_DIM)
                           for g in (diff_q_norm, diff_k_norm, fox_q_norm, fox_k_norm)])
        an = attn_norm[layer][None, :]

        u, dq, dk, dv, sq, sk, sv, fq, fk, fv = _proj_call(
            xf, an, wt[:MAIN_WIDTH].astype(BF16), wf_pad.astype(BF16), bf_pad, cos, sin, gains, gmat,
            ltri, sel, batch, seq)

        lam = jnp.stack([lam_q1[layer], lam_k1[layer], lam_q2[layer], lam_k2[layer]])
        od = _attention_call(
            functools.partial(_diff_kernel, lambda_init=_diff_lambda_init(layer)), "diff_attention",
            dq, dk, dv, (lam, diff_subln[layer][None, :]), batch, seq, DIFF_GROUPS,
            SOFTMAX_K_BLOCK, softmax_scratch, kv_buffers=1)
        os_ = _attention_call(_sb_kernel, "stick_breaking_attention", sq, sk, sv, (), batch, seq,
                              1, SB_K_BLOCK, _sb_scratch)
        of = _attention_call(_fox_kernel, "forgetting_attention", fq, fk, fv, (), batch, seq,
                             1, SOFTMAX_K_BLOCK, softmax_scratch)

        pw = jax.scipy.linalg.block_diag(*[pool_w[layer, g] for g in range(len(POOL_WINDOWS))])
        xf = _merge_ffn_call(xf, u, od, os_, of, an, pw.astype(BF16), pool_scale[layer][None, :],
                             w_branch[layer].astype(BF16), wt[MAIN_WIDTH + FOX_HEADS:].astype(BF16),
                             b_gate[layer], w_out[layer].astype(BF16), ffn_norm[layer][None, :],
                             w_ffn_up[layer].astype(BF16), w_ffn_down[layer].astype(BF16), batch, seq)
    return xf.reshape(batch, seq, d)
```

```python
import functools
import math

import numpy as np
import jax
import jax.numpy as jnp
from jax import lax
from jax.experimental import pallas as pl
from jax.experimental.pallas import tpu as pltpu

F32 = jnp.float32
BF16 = jnp.bfloat16

HEAD_DIM = 64
POOL_WINDOWS = (2, 4, 8, 16)
POOL_WIDTH = 256
DIFF_WIDTH = 512
SB_WIDTH = 256
FOX_HEADS = 4
FOX_WIDTH = 256
BRANCH_WIDTHS = (POOL_WIDTH, DIFF_WIDTH, SB_WIDTH, FOX_WIDTH)
MAIN_WIDTH = 3328
ROPE_THETA = 10000.0
NORM_EPS = 1e-6
QK_SCALE = HEAD_DIM ** -0.5
LOG2E = math.log2(math.e)
Q_SCALE = QK_SCALE * LOG2E

LANES = 128
ATTN_Q_BLOCK = 512
SOFTMAX_K_BLOCK = 512
SB_K_BLOCK = 512
SB_SUM_BLOCK = 256
DIFF_GROUPS = 1
ROW_TILE = 512
ROW_CHUNK = 128
POOL_HALO = 16
AUG_LANE = 64
VMEM_LIMIT = 56 * 1024 * 1024


def _diff_lambda_init(layer):
    return 0.8 - 0.6 * math.exp(-0.3 * layer)


def _cparams(n_axes):
    return pltpu.CompilerParams(dimension_semantics=("arbitrary",) * n_axes,
                                vmem_limit_bytes=VMEM_LIMIT)


def _full(shape):
    return pl.BlockSpec(shape, lambda *_: (0,) * len(shape))


def _rms_rows(x, gain):
    ms = jnp.mean(x * x, axis=-1, keepdims=True)
    return x * lax.rsqrt(ms + NORM_EPS) * gain


def _bdot(a, b):
    return jnp.dot(a, b, preferred_element_type=F32)


def _nt_dot(a, b):
    return lax.dot_general(a, b, (((1,), (1,)), ((), ())), preferred_element_type=F32)


def _split_head_pair(q):
    lane = lax.broadcasted_iota(jnp.int32, q.shape, 1)
    qf = q.astype(F32)
    return (jnp.where(lane < HEAD_DIM, qf, 0.0).astype(q.dtype),
            jnp.where(lane >= HEAD_DIM, qf, 0.0).astype(q.dtype))


def _split3(v):
    a1 = v.astype(BF16)
    r1 = v - a1.astype(F32)
    a2 = r1.astype(BF16)
    r2 = r1 - a2.astype(F32)
    a3 = r2.astype(BF16)
    return a1, a2, a3


def _rope_table_kernel(inv_ref, cos_ref, sin_ref):
    rows = cos_ref.shape[0]
    base = pl.program_id(0) * rows
    pos = (base + lax.broadcasted_iota(jnp.int32, (rows, LANES), 0)).astype(F32)
    ang = pos * inv_ref[...]
    lane = lax.broadcasted_iota(jnp.int32, (rows, LANES), 1)
    first_half = (lane & (HEAD_DIM // 2)) == 0
    s = jnp.sin(ang)
    cos_ref[...] = jnp.cos(ang)
    sin_ref[...] = jnp.where(first_half, -s, s)


def _rope_tables(seq):
    half = HEAD_DIM // 2
    inv_freq = ROPE_THETA ** (-jnp.arange(half, dtype=F32) / half)
    inv = jnp.tile(inv_freq, LANES // half)[None, :]
    rows = min(seq, ROW_TILE)
    return pl.pallas_call(
        _rope_table_kernel,
        grid=(seq // rows,),
        in_specs=[_full((1, LANES))],
        out_specs=[pl.BlockSpec((rows, LANES), lambda i: (i, 0))] * 2,
        out_shape=[jax.ShapeDtypeStruct((seq, LANES), F32)] * 2,
        compiler_params=_cparams(1),
        name="rope_tables",
    )(inv)


def _proj_kernel(x_ref, an_ref, w_ref, wf_ref, bf_ref, cos_ref, sin_ref, gains_ref, gmat_ref,
                 ltri_ref, sel_ref,
                 u_ref, dq_ref, dk_ref, dv_ref, sq_ref, sk_ref, sv_ref, fq_ref, fk_ref, fv_ref,
                 carry_ref):
    tm = x_ref.shape[0]
    h = _rms_rows(x_ref[...], an_ref[...]).astype(BF16)

    def proj(lo, width):
        return _nt_dot(h, w_ref[lo:lo + width, :])

    def group_norm(y, gain):
        ms = _bdot((y * y).astype(BF16), gmat_ref[...])
        return y * lax.rsqrt(ms + NORM_EPS) * gain

    cos = cos_ref[...]
    sin = sin_ref[...]
    lane = lax.broadcasted_iota(jnp.int32, (tm, LANES), 1)
    first_half = (lane & (HEAD_DIM // 2)) == 0
    low_head = lane < HEAD_DIM

    def rope(y):
        partner = jnp.where(first_half, pltpu.roll(y, LANES - HEAD_DIM // 2, 1),
                            pltpu.roll(y, HEAD_DIM // 2, 1))
        return y * cos + partner * sin

    u_ref[...] = proj(0, 256)

    gq = gains_ref[0:1, :]
    gk = gains_ref[1:2, :]
    for c in range(2):
        yq = group_norm(proj(256 + 256 * c, 256), gq)
        yk = group_norm(proj(768 + 256 * c, 256), gk)
        for hh in range(2):
            sl = slice(hh * LANES, (hh + 1) * LANES)
            col = slice(256 * c + hh * LANES, 256 * c + (hh + 1) * LANES)
            dq_ref[:, col] = (rope(yq[:, sl]) * Q_SCALE).astype(BF16)
            dk_ref[:, col] = rope(yk[:, sl]).astype(BF16)
        dv_ref[:, 256 * c:256 * (c + 1)] = proj(1280 + 256 * c, 256).astype(BF16)

    sq_ref[...] = (proj(1792, 256) * Q_SCALE).astype(BF16)
    sk_ref[...] = proj(2048, 256).astype(BF16)
    sv_ref[...] = proj(2304, 256).astype(BF16)
    fv_ref[...] = proj(3072, 256).astype(BF16)

    fl = _nt_dot(h, wf_ref[...]) + bf_ref[...]
    logf = jnp.minimum(fl, 0.0) - jnp.log1p(jnp.exp(-jnp.abs(fl)))

    @pl.when(pl.program_id(1) == 0)
    def _():
        carry_ref[...] = jnp.zeros_like(carry_ref)

    carry = carry_ref[0:1, :]
    sub = ltri_ref.shape[0]
    pieces = []
    for sb in range(tm // sub):
        a1, a2, a3 = _split3(logf[sb * sub:(sb + 1) * sub, :])
        ltri = ltri_ref[...]
        cum = _bdot(ltri, a1) + _bdot(ltri, a2) + _bdot(ltri, a3) + carry
        carry = cum[sub - 1:sub, :]
        pieces.append(cum)
    carry_ref[0:1, :] = carry
    cum = jnp.concatenate(pieces, axis=0) if len(pieces) > 1 else pieces[0]

    c1, c2, c3 = (c.astype(F32) for c in _split3(cum * LOG2E))
    packed = jnp.where(lane < 4, c1, jnp.where(lane < 8, c2, jnp.where(lane < 12, c3,
                       jnp.where(lane == 12, 1.0, 0.0))))
    aug = _bdot(packed.astype(BF16), sel_ref[...])

    yq = group_norm(proj(2560, 256), gains_ref[2:3, :]) * Q_SCALE
    yk = group_norm(proj(2816, 256), gains_ref[3:4, :])
    for src, dst, off in ((yq, fq_ref, 0), (yk, fk_ref, 4 * LANES)):
        for pair in range(2):
            v = src[:, pair * LANES:(pair + 1) * LANES]
            heads = (v, pltpu.roll(v, HEAD_DIM, 1))
            for hh in range(2):
                col = (2 * pair + hh) * LANES
                dst[:, col:col + LANES] = jnp.where(
                    low_head, heads[hh], aug[:, off + col:off + col + LANES]).astype(BF16)


def _proj_call(xf, an, w_main, wf_pad, bf_pad, cos, sin, gains, gmat, ltri, sel, batch, seq):
    n, d = xf.shape
    tm = min(seq, ROW_TILE)
    ns = seq // tm
    row = lambda b, s: (b * ns + s, 0)
    widths = (256, 512, 512, 512, 256, 256, 256, 512, 512, 256)
    dtypes = (F32,) + (BF16,) * 9
    return pl.pallas_call(
        _proj_kernel,
        grid=(batch, ns),
        in_specs=[pl.BlockSpec((tm, d), row), _full(an.shape), _full(w_main.shape),
                  _full(wf_pad.shape), _full(bf_pad.shape),
                  pl.BlockSpec((tm, LANES), lambda b, s: (s, 0)),
                  pl.BlockSpec((tm, LANES), lambda b, s: (s, 0)),
                  _full(gains.shape), _full(gmat.shape), _full(ltri.shape), _full(sel.shape)],
        out_specs=[pl.BlockSpec((tm, w), row) for w in widths],
        out_shape=[jax.ShapeDtypeStruct((n, w), t) for w, t in zip(widths, dtypes)],
        scratch_shapes=[pltpu.VMEM((8, LANES), F32)],
        compiler_params=_cparams(2),
        name="norm_in_proj",
    )(xf, an, w_main, wf_pad, bf_pad, cos, sin, gains, gmat, ltri, sel)


def _key_minus_query(row0, rows, tq, tk):
    r = (row0 + lax.broadcasted_iota(jnp.int32, (rows, tk), 0)) & (tq - 1)
    c = lax.broadcasted_iota(jnp.int32, (rows, tk), 1)
    return c - r


def _softmax_sweep(streams, k_ref, v_ref, scratch, i, tq, tk):
    s_scr, p_scr, m_scr, l_scr, acc_scr = scratch
    n_lane_chunks = tk // LANES
    m_scr[...] = jnp.full(m_scr.shape, -1e30, F32)
    l_scr[...] = jnp.zeros(l_scr.shape, F32)
    acc_scr[...] = jnp.zeros(acc_scr.shape, F32)

    def row_pass(rc, n_chunks):
        rows = slice(rc * ROW_CHUNK, (rc + 1) * ROW_CHUNK)
        mx = s_scr[rows, 0:LANES]
        for c in range(1, n_chunks):
            mx = jnp.maximum(mx, s_scr[rows, c * LANES:(c + 1) * LANES])
        m_old = m_scr[rows, :]
        m_new = jnp.maximum(m_old, mx.max(axis=1, keepdims=True))
        alpha = jnp.exp2(m_old - m_new)
        m_scr[rows, :] = m_new
        psum = None
        for c in range(n_chunks):
            p = jnp.exp2(s_scr[rows, c * LANES:(c + 1) * LANES] - m_new)
            psum = p if psum is None else psum + p
            p_scr[rows, c * LANES:(c + 1) * LANES] = p.astype(BF16)
        l_scr[rows, :] = alpha * l_scr[rows, :] + psum
        acc_scr[rows, :] = alpha * acc_scr[rows, :]

    groups = []
    for n, (q, cols, _) in enumerate(streams):
        if groups and groups[-1][3] == cols:
            n0, cnt, qs, _ = groups[-1]
            groups[-1] = (n0, cnt + 1, qs + [q], cols)
        else:
            groups.append((n, 1, [q], cols))
    groups = [(n0, cnt, qs[0] if cnt == 1 else jnp.concatenate(qs, axis=0), cols)
              for n0, cnt, qs, cols in groups]

    def step(j, limit):
        start = pl.multiple_of(j * tk, tk)
        for n0, cnt, q, cols in groups:
            s = _nt_dot(q, k_ref[pl.ds(start, tk), cols])
            if limit is not None:
                s = jnp.where(_key_minus_query(0, cnt * tq, tq, tk) <= limit, s, -jnp.inf)
            s_scr[n0 * tq:(n0 + cnt) * tq, :] = s
        for n, (_, _, vcols) in enumerate(streams):
            for rc in range(n * tq // ROW_CHUNK, (n + 1) * tq // ROW_CHUNK):
                row_pass(rc, n_lane_chunks)
            acc_scr[n * tq:(n + 1) * tq, :] += _bdot(p_scr[n * tq:(n + 1) * tq, :],
                                                     v_ref[pl.ds(start, tk), vcols])

    def diagonal_step(j):
        start = pl.multiple_of(j * tk, tk)
        half = tq // 2
        spans = ((0, half), (half, tq))
        for n, (q, cols, _) in enumerate(streams):
            for r0, nk in spans:
                s = _nt_dot(q[r0:r0 + half, :], k_ref[pl.ds(start, nk), cols])
                r = r0 + lax.broadcasted_iota(jnp.int32, (half, nk), 0)
                c = lax.broadcasted_iota(jnp.int32, (half, nk), 1)
                s_scr[n * tq + r0:n * tq + r0 + half, 0:nk] = jnp.where(c <= r, s, -jnp.inf)
        for n, (_, _, vcols) in enumerate(streams):
            for r0, nk in spans:
                lo = n * tq + r0
                for rc in range(lo // ROW_CHUNK, (lo + half) // ROW_CHUNK):
                    row_pass(rc, nk // LANES)
                acc_scr[lo:lo + half, :] += _bdot(p_scr[lo:lo + half, 0:nk],
                                                  v_ref[pl.ds(start, nk), vcols])

    ratio = tq // tk

    def body(j, _):
        step(j, None)
        return 0

    lax.fori_loop(0, ratio * i, body, 0)
    if ratio == 1 and tq % (2 * max(ROW_CHUNK, LANES)) == 0:
        diagonal_step(i)
    else:
        for d in range(ratio):
            step(ratio * i + d, -d * tk)
    return acc_scr[...], l_scr[...].sum(axis=1, keepdims=True)


def _softmax_scratch(n_rows, tk, v_width):
    return [pltpu.VMEM((n_rows, tk), F32), pltpu.VMEM((n_rows, tk), BF16),
            pltpu.VMEM((n_rows, LANES), F32), pltpu.VMEM((n_rows, LANES), F32),
            pltpu.VMEM((n_rows, v_width), F32)]


def _diff_kernel(q_ref, k_ref, v_ref, lam_ref, sub_ref, o_ref, *scratch, lambda_init, tk):
    tq = q_ref.shape[0]
    i = pl.program_id(2)
    n_heads = q_ref.shape[1] // LANES
    streams = []
    for hd in range(n_heads):
        cols = slice(hd * LANES, (hd + 1) * LANES)
        for qh in _split_head_pair(q_ref[:, cols]):
            streams.append((qh, cols, cols))
    acc, l = _softmax_sweep(streams, k_ref, v_ref, scratch, i, tq, tk)
    o = acc / l
    lp = lam_ref[...]
    lam = (jnp.exp(jnp.sum(lp[0:1, :] * lp[1:2, :], axis=1, keepdims=True))
           - jnp.exp(jnp.sum(lp[2:3, :] * lp[3:4, :], axis=1, keepdims=True)) + lambda_init)
    for hd in range(n_heads):
        r0 = 2 * hd * tq
        od = o[r0:r0 + tq, :] - lam * o[r0 + tq:r0 + 2 * tq, :]
        o_ref[:, hd * LANES:(hd + 1) * LANES] = (
            _rms_rows(od, sub_ref[...]) * (1.0 - lambda_init)).astype(o_ref.dtype)


def _fox_kernel(q_ref, k_ref, v_ref, o_ref, *scratch, tk):
    tq = q_ref.shape[0]
    i = pl.program_id(2)
    n_heads = q_ref.shape[1] // LANES
    streams = [(q_ref[:, hd * LANES:(hd + 1) * LANES], slice(hd * LANES, (hd + 1) * LANES),
                slice((hd // 2) * LANES, (hd // 2 + 1) * LANES)) for hd in range(n_heads)]
    acc, l = _softmax_sweep(streams, k_ref, v_ref, scratch, i, tq, tk)
    o = acc / l
    lane = lax.broadcasted_iota(jnp.int32, (tq, LANES), 1)
    for pair in range(n_heads // 2):
        r0 = 2 * pair * tq
        o_ref[:, pair * LANES:(pair + 1) * LANES] = jnp.where(
            lane < HEAD_DIM, o[r0:r0 + tq, :], o[r0 + tq:r0 + 2 * tq, :]).astype(o_ref.dtype)


def _sb_kernel(q_ref, k_ref, v_ref, o_ref, z_scr, sp_scr, c_scr, w_scr, first_scr, later_scr, acc_scr,
               *, tk):
    tq = q_ref.shape[0]
    i = pl.program_id(2)
    ratio = tq // tk
    n_pairs = q_ref.shape[1] // LANES
    heads = []
    for pair in range(n_pairs):
        kv_cols = slice(pair * LANES, (pair + 1) * LANES)
        heads += [(qh, kv_cols) for qh in _split_head_pair(q_ref[:, kv_cols])]
    sub = min(SB_SUM_BLOCK, tk)
    n_sub = tk // sub
    lanes_per_sub = sub // LANES
    neg_later = jnp.where(lax.broadcasted_iota(jnp.int32, (sub, sub), 0)
                          > lax.broadcasted_iota(jnp.int32, (sub, sub), 1), -1.0, 0.0).astype(BF16)
    later_scr[...] = jnp.zeros(later_scr.shape, F32)
    acc_scr[...] = jnp.zeros(acc_scr.shape, F32)

    def step(j, limit):
        start = pl.multiple_of(j * tk, tk)
        for n, (qh, kv_cols) in enumerate(heads):
            z_scr[n * tq:(n + 1) * tq, :] = _nt_dot(qh, k_ref[pl.ds(start, tk), kv_cols])
        for n, (_, kv_cols) in enumerate(heads):
            head_rows = slice(n * tq, (n + 1) * tq)
            chunks = range(n * tq // ROW_CHUNK, (n + 1) * tq // ROW_CHUNK)
            for rc in chunks:
                rows = slice(rc * ROW_CHUNK, (rc + 1) * ROW_CHUNK)
                z = z_scr[rows, :]
                neg_abs = lax.bitcast_convert_type(
                    lax.bitcast_convert_type(z, jnp.int32) | jnp.int32(-2 ** 31), F32)
                sp = jnp.maximum(z, 0.0) + jnp.log2(1.0 + jnp.exp2(neg_abs))
                if limit is not None:
                    sp = jnp.where(_key_minus_query(rc * ROW_CHUNK, ROW_CHUNK, tq, tk) < limit, sp, 0.0)
                sp_scr[rows, :] = sp.astype(BF16)
                z_scr[rows, :] = z - sp
                for b in range(n_sub):
                    first_scr[rows, b * LANES:(b + 1) * LANES] = sp[:, b * sub:b * sub + LANES]
            for b in range(n_sub):
                c_scr[head_rows, b * sub:(b + 1) * sub] = _bdot(
                    sp_scr[head_rows, b * sub:(b + 1) * sub], neg_later)
            for rc in chunks:
                rows = slice(rc * ROW_CHUNK, (rc + 1) * ROW_CHUNK)
                later = later_scr[rows, :]
                for b in reversed(range(n_sub)):
                    for c in range(b * lanes_per_sub, (b + 1) * lanes_per_sub):
                        cols = slice(c * LANES, (c + 1) * LANES)
                        w = jnp.exp2(z_scr[rows, cols] + c_scr[rows, cols] + later)
                        if limit is not None:
                            keep = _key_minus_query(rc * ROW_CHUNK, ROW_CHUNK, tq, tk)[:, cols] < limit
                            w = jnp.where(keep, w, 0.0)
                        w_scr[rows, cols] = w.astype(BF16)
                    later = later + (c_scr[rows, b * sub:b * sub + 1]
                                     - first_scr[rows, b * LANES:b * LANES + 1])
                later_scr[rows, :] = later
            acc_scr[head_rows, :] += _bdot(w_scr[head_rows, :], v_ref[pl.ds(start, tk), kv_cols])

    for d in reversed(range(ratio)):
        step(ratio * i + d, -d * tk)

    def body(n, _):
        step(ratio * i - 1 - n, None)
        return 0

    lax.fori_loop(0, ratio * i, body, 0)
    lane = lax.broadcasted_iota(jnp.int32, (tq, LANES), 1)
    for pair in range(n_pairs):
        r0 = 2 * pair * tq
        o_ref[:, pair * LANES:(pair + 1) * LANES] = jnp.where(
            lane < HEAD_DIM, acc_scr[r0:r0 + tq, :], acc_scr[r0 + tq:r0 + 2 * tq, :]).astype(o_ref.dtype)


def _sb_scratch(n_rows, tk):
    n_sub = tk // min(SB_SUM_BLOCK, tk)
    return [pltpu.VMEM((n_rows, tk), F32), pltpu.VMEM((n_rows, tk), BF16),
            pltpu.VMEM((n_rows, tk), F32), pltpu.VMEM((n_rows, tk), BF16),
            pltpu.VMEM((n_rows, n_sub * LANES), F32),
            pltpu.VMEM((n_rows, LANES), F32), pltpu.VMEM((n_rows, LANES), F32)]


def _attention_call(body, name, q, k, v, extra, batch, seq, n_groups, tk, scratch_fn,
                    kv_buffers=2):
    n = q.shape[0]
    tq = min(seq, ATTN_Q_BLOCK)
    tk = min(tk, tq)
    nq = seq // tq
    q_width, v_width = q.shape[1] // n_groups, v.shape[1] // n_groups
    kv_mode = pl.Buffered(kv_buffers) if kv_buffers != 2 else None
    return pl.pallas_call(
        functools.partial(body, tk=tk),
        grid=(batch, n_groups, nq),
        in_specs=[pl.BlockSpec((tq, q_width), lambda b, g, i: (b * nq + i, g)),
                  pl.BlockSpec((seq, q_width), lambda b, g, i: (b, g), pipeline_mode=kv_mode),
                  pl.BlockSpec((seq, v_width), lambda b, g, i: (b, g), pipeline_mode=kv_mode)]
                 + [_full(e.shape) for e in extra],
        out_specs=pl.BlockSpec((tq, v_width), lambda b, g, i: (b * nq + i, g)),
        out_shape=jax.ShapeDtypeStruct((n, v.shape[1]), BF16),
        scratch_shapes=scratch_fn(2 * tq * v_width // LANES, tk),
        compiler_params=_cparams(3),
        name=name,
    )(q, k, v, *extra)


def _merge_tile(x_ref, u_ref, halo_ref, od_ref, os_ref, of_ref, an_ref, pw_ref, ps_ref,
                wbr_ref, wg_ref, bg_ref, wout_ref):
    tm, d = x_ref.shape
    x = x_ref[...]
    h = _rms_rows(x, an_ref[...]).astype(BF16)

    u = u_ref[...]
    first_tile = pl.program_id(1) == 0
    halo = jnp.where(first_tile, 0.0, halo_ref[...])
    ext = jnp.concatenate([halo, u], axis=0)
    lane = lax.broadcasted_iota(jnp.int32, (tm, POOL_WIDTH), 1)
    group = lane >> 6
    win = jnp.zeros_like(u)
    width = jnp.zeros((tm, POOL_WIDTH), jnp.int32)
    span = 1
    for g, w in enumerate(POOL_WINDOWS):
        while span < w:
            ext = ext + pltpu.roll(ext, span, 0)
            span *= 2
        win = jnp.where(group == g, ext[POOL_HALO:, :], win)
        width = jnp.where(group == g, w, width)
    pos = pl.program_id(1) * tm + lax.broadcasted_iota(jnp.int32, (tm, POOL_WIDTH), 0)
    count = jnp.minimum(pos + 1, width).astype(F32)
    pooled = win / count - u
    o_pool = (_bdot(pooled.astype(BF16), pw_ref[...]) * ps_ref[...]).astype(BF16)

    merged = jnp.zeros((tm, d), F32)
    lo = 0
    for nb, o in enumerate((o_pool, od_ref[...], os_ref[...], of_ref[...])):
        width_n = BRANCH_WIDTHS[nb]
        y = _bdot(o, wbr_ref[lo:lo + width_n, :])
        gate = jax.nn.sigmoid(_nt_dot(h, wg_ref[nb * d:(nb + 1) * d, :]) + bg_ref[nb:nb + 1, :])
        merged = merged + gate * y
        lo += width_n
    return x + _bdot(merged.astype(BF16), wout_ref[...])


def _ffn_tile(x, fn_ref, wup_ref, wdn_ref, chunk):
    h = _rms_rows(x, fn_ref[...]).astype(BF16)
    d_ff = wdn_ref.shape[0]
    acc = x
    for c in range(d_ff // chunk):
        gate = _bdot(h, wup_ref[:, c * chunk:(c + 1) * chunk])
        up = _bdot(h, wup_ref[:, d_ff + c * chunk:d_ff + (c + 1) * chunk])
        act = (gate * jax.nn.sigmoid(gate) * up).astype(BF16)
        acc = acc + _bdot(act, wdn_ref[c * chunk:(c + 1) * chunk, :])
    return acc


def _merge_ffn_kernel(*refs, chunk):
    *merge_refs, fn_ref, wup_ref, wdn_ref, o_ref = refs
    o_ref[...] = _ffn_tile(_merge_tile(*merge_refs), fn_ref, wup_ref, wdn_ref, chunk)


def _merge_ffn_call(xf, u, od, os_, of, an, pw, ps, wbr, wg, bg, wout, fn, wup, wdn, batch, seq):
    n, d = xf.shape
    tm = min(seq, ROW_TILE)
    ns = seq // tm
    row = lambda b, s: (b * ns + s, 0)
    halo_blocks = tm // POOL_HALO
    halo = lambda b, s: (jnp.maximum((b * ns + s) * halo_blocks - 1, 0), 0)
    once = lambda a: pl.BlockSpec(a.shape, lambda b, s: (0,) * a.ndim, pipeline_mode=pl.Buffered(1))
    return pl.pallas_call(
        functools.partial(_merge_ffn_kernel, chunk=256),
        grid=(batch, ns),
        in_specs=[pl.BlockSpec((tm, d), row), pl.BlockSpec((tm, POOL_WIDTH), row),
                  pl.BlockSpec((POOL_HALO, POOL_WIDTH), halo),
                  pl.BlockSpec((tm, DIFF_WIDTH), row), pl.BlockSpec((tm, SB_WIDTH), row),
                  pl.BlockSpec((tm, FOX_WIDTH), row),
                  _full(an.shape), once(pw), _full(ps.shape), once(wbr), once(wg), _full(bg.shape),
                  once(wout), _full(fn.shape), once(wup), once(wdn)],
        out_specs=pl.BlockSpec((tm, d), row),
        out_shape=jax.ShapeDtypeStruct((n, d), F32),
        compiler_params=_cparams(2),
        name="merge_and_ffn",
    )(xf, u, u, od, os_, of, an, pw, ps, wbr, wg, bg, wout, fn, wup, wdn)


def _group_mean_matrix():
    g = np.kron(np.eye(256 // HEAD_DIM), np.full((HEAD_DIM, HEAD_DIM), 1.0 / HEAD_DIM))
    return jnp.asarray(g, BF16)


def _lower_tri(nrows):
    return jnp.asarray(np.tril(np.ones((nrows, nrows))), BF16)


def _decay_selector():
    sel = np.zeros((LANES, 2 * FOX_HEADS * LANES), np.float32)
    for hd in range(FOX_HEADS):
        qb = hd * LANES + AUG_LANE
        kb = (FOX_HEADS + hd) * LANES + AUG_LANE
        for term in range(3):
            sel[4 * term + hd, qb + term] = 1.0
            sel[12, qb + 3 + term] = 1.0
            sel[12, kb + term] = 1.0
            sel[4 * term + hd, kb + 3 + term] = -1.0
    return jnp.asarray(sel, BF16)


def kernel(x, attn_norm, ffn_norm, w_in, b_gate, b_forget, pool_w, pool_scale, diff_q_norm,
           diff_k_norm, diff_subln, lam_q1, lam_k1, lam_q2, lam_k2, fox_q_norm, fox_k_norm,
           w_branch, w_out, w_ffn_up, w_ffn_down):
    batch, seq, d = x.shape
    depth = attn_norm.shape[0]
    assert seq % min(seq, ROW_TILE) == 0 and seq % min(seq, ATTN_Q_BLOCK) == 0
    xf = x.reshape(batch * seq, d)
    cos, sin = _rope_tables(seq)
    gmat = _group_mean_matrix()
    ltri = _lower_tri(min(seq, 256))
    sel = _decay_selector()
    softmax_scratch = functools.partial(_softmax_scratch, v_width=LANES)

    for layer in range(depth):
        wt = jnp.swapaxes(w_in[layer], 0, 1)
        wf = wt[MAIN_WIDTH:MAIN_WIDTH + FOX_HEADS]
        wf_pad = jnp.concatenate([wf, wf, wf, jnp.zeros((LANES - 3 * FOX_HEADS, d), wt.dtype)], axis=0)
        bfg = b_forget[layer]
        bf_pad = jnp.concatenate([bfg, bfg, bfg, jnp.zeros((LANES - 3 * FOX_HEADS,), F32)])[None, :]
        gains = jnp.stack([jnp.tile(g[layer], 256 // HEAD_DIM)
                           for g in (diff_q_norm, diff_k_norm, fox_q_norm, fox_k_norm)])
        an = attn_norm[layer][None, :]

        u, dq, dk, dv, sq, sk, sv, fq, fk, fv = _proj_call(
            xf, an, wt[:MAIN_WIDTH].astype(BF16), wf_pad.astype(BF16), bf_pad, cos, sin, gains, gmat,
            ltri, sel, batch, seq)

        lam = jnp.stack([lam_q1[layer], lam_k1[layer], lam_q2[layer], lam_k2[layer]])
        od = _attention_call(
            functools.partial(_diff_kernel, lambda_init=_diff_lambda_init(layer)), "diff_attention",
            dq, dk, dv, (lam, diff_subln[layer][None, :]), batch, seq, DIFF_GROUPS,
            SOFTMAX_K_BLOCK, softmax_scratch, kv_buffers=1)
        os_ = _attention_call(_sb_kernel, "stick_breaking_attention", sq, sk, sv, (), batch, seq,
                              1, SB_K_BLOCK, _sb_scratch)
        of = _attention_call(_fox_kernel, "forgetting_attention", fq, fk, fv, (), batch, seq,
                             1, SOFTMAX_K_BLOCK, softmax_scratch)

        pw = jax.scipy.linalg.block_diag(*[pool_w[layer, g] for g in range(len(POOL_WINDOWS))])
        xf = _merge_ffn_call(xf, u, od, os_, of, an, pw.astype(BF16), pool_scale[layer][None, :],
                             w_branch[layer].astype(BF16), wt[MAIN_WIDTH + FOX_HEADS:].astype(BF16),
                             b_gate[layer], w_out[layer].astype(BF16), ffn_norm[layer][None, :],
                             w_ffn_up[layer].astype(BF16), w_ffn_down[layer].astype(BF16), batch, seq)
    return xf.reshape(batch, seq, d)
```
